```python
import math
import jax
import jax.numpy as jnp
from jax import lax
import numpy as np

D_MODEL = 1024
BATCH = 32
SEQ = 256
DEPTH = 4
DEC_BATCH = 8
DEC_SEQ = 2048
PAST_LEN = 256

GRID_W = 64
H_A = 4
HEAD_DIM_A = 128
A_WIDTH = H_A * HEAD_DIM_A
SHORT_CONV = 5
CHUNK = 64
S5_GROUP = 16
S5_STATE = 64
B_WIDTH = 512
S5_GROUPS = B_WIDTH // S5_GROUP
H_C = 8
QK_NOPE = 64
QK_ROPE = 32
V_HEAD = 64
Q_LORA = 384
KV_LORA = 256
C_WIDTH = H_C * V_HEAD
ROPE_BASE = 10000.0
Q_BLOCK = 128
N_BRANCH = 3
BRANCH_WIDTH = 512
D_FF = 2816
FFN_CONV = 3
NORM_EPS = 1e-6
IN_SPLITS = (3 * A_WIDTH, A_WIDTH, 2 * H_A, 2 * H_A, B_WIDTH, Q_LORA, KV_LORA, QK_ROPE, N_BRANCH * D_MODEL)
IN_WIDTH = 4 * A_WIDTH + 4 * H_A + B_WIDTH + Q_LORA + KV_LORA + QK_ROPE + N_BRANCH * D_MODEL

kernel_name = 'hybrid_diffusion_prefix_trunk_step'


def rmsnorm(x, gain):
    xf = x.astype(jnp.float32)
    y = xf * lax.rsqrt(jnp.mean(xf * xf, axis=-1, keepdims=True) + NORM_EPS)
    return y * gain.astype(jnp.float32)


def l2norm(x):
    return x * lax.rsqrt(jnp.sum(x * x, axis=-1, keepdims=True) + NORM_EPS)


def dwconv_centred(x, w):
    width = w.shape[0]
    pad = width // 2
    length = x.shape[1]
    xp = jnp.pad(x, ((0, 0), (pad, pad), (0, 0)))
    return sum(xp[:, i:i + length] * w[i] for i in range(width))


def split_columns(t, widths):
    parts, start = [], 0
    for wd in widths:
        parts.append(t[..., start:start + wd])
        start += wd
    return parts


def modulation(cvec, w_mod, b_mod):
    m = jax.nn.silu(cvec.astype(jnp.float32)) @ w_mod.astype(jnp.float32) + b_mod.astype(jnp.float32)
    return [t[:, None, :] for t in jnp.split(m, 6, axis=-1)]


def axial_rope(length):
    rows = length // GRID_W
    row = jnp.repeat(jnp.arange(rows, dtype=jnp.float32), GRID_W)
    col = (jnp.arange(length) % GRID_W).astype(jnp.float32)
    n_freq = QK_ROPE // 4
    inv_freq = 1.0 / (ROPE_BASE ** (jnp.arange(n_freq, dtype=jnp.float32) / n_freq))
    ang = jnp.concatenate([row[:, None] * inv_freq, col[:, None] * inv_freq], axis=-1)
    return jnp.cos(ang), jnp.sin(ang)


def apply_rope(x, cos, sin):
    half = x.shape[-1] // 2
    x1, x2 = x[..., :half], x[..., half:]
    return jnp.concatenate([x1 * cos - x2 * sin, x2 * cos + x1 * sin], axis=-1)


def chunk_gated_delta(q, k, v, g, beta, s0):
    bsz, length, nh, dk = q.shape
    dv = v.shape[-1]
    n = length // CHUNK

    def chunks(t):
        return t.reshape(bsz, n, CHUNK, nh, t.shape[-1]).transpose(1, 0, 3, 2, 4)

    qc, kc, vc = chunks(q * dk ** -0.5), chunks(k), chunks(v)
    gc = jnp.cumsum(chunks(g[..., None])[..., 0], axis=-1)
    bc = chunks(beta[..., None])
    lower = jnp.tril(jnp.ones((CHUNK, CHUNK), dtype=bool))
    diff = gc[..., :, None] - gc[..., None, :]
    decay = jnp.where(lower, jnp.exp(jnp.where(lower, diff, 0.0)), 0.0)
    kb = kc * bc
    a_strict = jnp.einsum('nbhcd,nbhed->nbhce', kb, kc) * decay
    rhs = jnp.concatenate([vc * bc, kb * jnp.exp(gc)[..., None]], axis=-1)
    sol = lax.linalg.triangular_solve(a_strict, rhs, left_side=True, lower=True, unit_diagonal=True)
    u, w = sol[..., :dv], sol[..., dv:]
    qk = jnp.einsum('nbhcd,nbhed->nbhce', qc, kc) * decay

    def step(state, inp):
        q_i, k_i, u_i, w_i, g_i, qk_i = inp
        v_new = u_i - jnp.einsum('bhcd,bhdv->bhcv', w_i, state)
        o_i = (jnp.einsum('bhcd,bhdv->bhcv', q_i * jnp.exp(g_i)[..., None], state)
               + jnp.einsum('bhce,bhev->bhcv', qk_i, v_new))
        g_last = g_i[..., -1:]
        state = (state * jnp.exp(g_last)[..., None]
                 + jnp.einsum('bhcd,bhcv->bhdv', k_i * jnp.exp(g_last - g_i)[..., None], v_new))
        return state, o_i

    s_final, o = lax.scan(step, s0, (qc, kc, u, w, gc, qk))
    o = o.transpose(1, 0, 3, 2, 4).reshape(bsz, length, nh, dv)
    return o, s_final


def gated_deltanet(qkv, z, beta_logit, alpha_logit, conv_w, a_log, dt_bias, g_out, s0):
    f32 = jnp.float32
    bsz, length, _ = qkv.shape
    qkv = jax.nn.silu(dwconv_centred(qkv.astype(f32), conv_w.astype(f32)))
    q, k, v = jnp.split(qkv, 3, axis=-1)

    def heads(t):
        return t.reshape(bsz, length, H_A, HEAD_DIM_A)

    q, k, v = l2norm(heads(q)), l2norm(heads(k)), heads(v)
    beta = jax.nn.sigmoid(beta_logit.astype(f32))
    g = -jnp.exp(a_log.astype(f32)) * jax.nn.softplus(alpha_logit.astype(f32) + dt_bias.astype(f32))

    def flip(t):
        return jnp.flip(t, axis=1)

    o_f, s_f = chunk_gated_delta(q, k, v, g[:, :, 0], beta[:, :, 0], s0[:, 0])
    o_b, s_b = chunk_gated_delta(flip(q), flip(k), flip(v), flip(g[:, :, 1]), flip(beta[:, :, 1]), s0[:, 1])
    o = rmsnorm(o_f + flip(o_b), g_out) * jax.nn.silu(heads(z.astype(f32)))
    return o.reshape(bsz, length, A_WIDTH), jnp.stack([s_f, s_b], axis=1)


def s5_diag_scan(u, lam_bar, b_bar, h0):
    bu = jnp.einsum('blgc,gpc->blgp', u.astype(jnp.complex64), b_bar)
    bu = bu.at[:, 0].add(lam_bar * h0)
    a = jnp.broadcast_to(lam_bar, bu.shape)

    def combine(e1, e2):
        a1, b1 = e1
        a2, b2 = e2
        return a1 * a2, a2 * b1 + b2

    _, h = lax.associative_scan(combine, (a, bu), axis=1)
    return h


def s5_mixer(u, lam_re, lam_im, log_dt, b_re, b_im, c_re, c_im, d_skip, w_glu, b_glu, h0):
    f32 = jnp.float32
    bsz, length, _ = u.shape
    uf = u.astype(f32).reshape(bsz, length, S5_GROUPS, S5_GROUP)
    lam = lax.complex(lam_re.astype(f32), lam_im.astype(f32))
    lam_bar = jnp.exp(lam * jnp.exp(log_dt.astype(f32))[..., None])
    b_bar = ((lam_bar - 1.0) / lam)[..., None] * lax.complex(b_re.astype(f32), b_im.astype(f32))

    def flip(t):
        return jnp.flip(t, axis=1)

    h_f = s5_diag_scan(uf, lam_bar[0], b_bar[0], h0[:, 0])
    h_b = flip(s5_diag_scan(flip(uf), lam_bar[1], b_bar[1], h0[:, 1]))
    c_mat = lax.complex(c_re.astype(f32), c_im.astype(f32))
    y = (jnp.real(jnp.einsum('blgp,gcp->blgc', h_f + h_b, c_mat))
         + d_skip.astype(f32).reshape(S5_GROUPS, S5_GROUP) * uf)
    y = jax.nn.gelu(y.reshape(bsz, length, B_WIDTH))
    y = y * jax.nn.sigmoid(y @ w_glu.astype(f32) + b_glu.astype(f32))
    final = jnp.stack([h_f[:, -1], h_b[:, 0]], axis=1)
    return y, final


def mla_keys(ckv, k_rope, w_kv_b):
    bsz, length, _ = ckv.shape
    kv = (ckv @ w_kv_b).reshape(bsz, length, H_C, QK_NOPE + V_HEAD)
    k_nope, v = kv[..., :QK_NOPE], kv[..., QK_NOPE:]
    k_r = jnp.broadcast_to(k_rope[:, :, None, :], (bsz, length, H_C, QK_ROPE))
    return jnp.concatenate([k_nope, k_r], axis=-1), v


def blocked_attention(q, k, v):
    bsz, lq, nh, dqk = q.shape
    nb = lq // Q_BLOCK
    qb = q.reshape(bsz, nb, Q_BLOCK, nh, dqk).transpose(1, 0, 2, 3, 4)
    scale = dqk ** -0.5

    def one_block(q_blk):
        s = jnp.einsum('bqhd,bkhd->bhqk', q_blk, k) * scale
        p = jax.nn.softmax(s, axis=-1)
        return jnp.einsum('bhqk,bkhv->bqhv', p, v)

    o = lax.map(one_block, qb)
    return o.transpose(1, 0, 2, 3, 4).reshape(bsz, lq, nh, v.shape[-1])


def mla_mixer(qa, kva, krope, g_q_a, w_q_b, g_kv_a, w_kv_b, cache):
    f32 = jnp.float32
    bsz, length, _ = qa.shape
    q = (rmsnorm(qa, g_q_a) @ w_q_b.astype(f32)).reshape(bsz, length, H_C, QK_NOPE + QK_ROPE)
    ckv = rmsnorm(kva, g_kv_a)
    krope = krope.astype(f32)
    w_kv_b = w_kv_b.astype(f32)
    if cache is None:
        k, v = mla_keys(ckv, krope, w_kv_b)
    else:
        cache_ckv, cache_krope = cache
        cos, sin = axial_rope(length)
        q = jnp.concatenate([q[..., :QK_NOPE], apply_rope(q[..., QK_NOPE:], cos[:, None], sin[:, None])], axis=-1)
        k_lat, v_lat = mla_keys(ckv, apply_rope(krope, cos, sin), w_kv_b)
        k_ctx, v_ctx = mla_keys(cache_ckv.astype(f32), cache_krope.astype(f32), w_kv_b)
        k = jnp.concatenate([k_ctx, k_lat], axis=1)
        v = jnp.concatenate([v_ctx, v_lat], axis=1)
    o = blocked_attention(q, k, v)
    return o.reshape(bsz, length, C_WIDTH), ckv, krope


def token_mixers(h, lp, ctx):
    f32 = jnp.float32
    bsz, length, _ = h.shape
    proj = h @ lp['w_in'].astype(f32)
    a_qkv, a_z, a_beta, a_alpha, b_u, c_qa, c_kva, c_kr, gate_logit = split_columns(proj, IN_SPLITS)
    if ctx is None:
        s0 = jnp.zeros((bsz, 2, H_A, HEAD_DIM_A, HEAD_DIM_A), f32)
        h0 = jnp.zeros((bsz, 2, S5_GROUPS, S5_STATE), jnp.complex64)
        mla_cache = None
    else:
        st_delta, st_re, st_im, c_ckv, c_krope = ctx
        s0 = st_delta.astype(f32)
        h0 = lax.complex(st_re.astype(f32), st_im.astype(f32))
        mla_cache = (c_ckv, c_krope)
    o_a, s_delta = gated_deltanet(a_qkv, a_z, a_beta.reshape(bsz, length, 2, H_A),
                                  a_alpha.reshape(bsz, length, 2, H_A), lp['conv_qkv'], lp['a_log'],
                                  lp['dt_bias'], lp['g_delta_out'], s0)
    o_b, s_s5 = s5_mixer(b_u, lp['s5_lam_re'], lp['s5_lam_im'], lp['s5_log_dt'], lp['s5_b_re'], lp['s5_b_im'],
                         lp['s5_c_re'], lp['s5_c_im'], lp['s5_d'], lp['w_glu'], lp['b_glu'], h0)
    o_c, ckv, krope = mla_mixer(c_qa, c_kva, c_kr, lp['g_q_a'], lp['w_q_b'], lp['g_kv_a'], lp['w_kv_b'], mla_cache)
    branches = jnp.stack([o_a, o_b, o_c], axis=2)
    per_branch = jnp.einsum('blnw,nwd->blnd', branches, lp['w_branch'].astype(f32))
    gates = jax.nn.sigmoid(gate_logit.astype(f32)).reshape(bsz, length, N_BRANCH, D_MODEL)
    out = jnp.sum(gates * per_branch, axis=2) @ lp['w_out'].astype(f32)
    if ctx is None:
        return out, (s_delta, jnp.real(s_s5), jnp.imag(s_s5), ckv, krope)
    return out, None


def conv_mlp(h, w_up, conv_w, conv_b, w_down):
    f32 = jnp.float32
    up = dwconv_centred(h @ w_up.astype(f32), conv_w.astype(f32)) + conv_b.astype(f32)
    gate, val = jnp.split(up, 2, axis=-1)
    return (jax.nn.silu(gate) * val) @ w_down.astype(f32)


def trunk_layer(x, mods, lp, ctx):
    shift_m, scale_m, gate_m, shift_f, scale_f, gate_f = mods
    h = rmsnorm(x, lp['g_norm_mix']) * (1.0 + scale_m) + shift_m
    mix, ctx_tensors = token_mixers(h, lp, ctx)
    x = x + gate_m * mix
    h = rmsnorm(x, lp['g_norm_ffn']) * (1.0 + scale_f) + shift_f
    x = x + gate_f * conv_mlp(h, lp['w_ffn_up'], lp['conv_ffn'], lp['b_conv_ffn'], lp['w_ffn_down'])
    return x, ctx_tensors


def setup_inputs(seed: int = 0) -> dict:
    key = jax.random.key(seed)
    keys = iter(jax.random.split(key, 48))
    f32 = jnp.float32

    def nrm(shape, scale):
        return scale * jax.random.normal(next(keys), shape, f32)

    def unif(shape, lo, hi):
        return jax.random.uniform(next(keys), shape, f32, lo, hi)

    G, P, CG = S5_GROUPS, S5_STATE, S5_GROUP
    dt_a = jnp.exp(unif((DEPTH, 2, H_A), math.log(1e-3), math.log(1e-1)))
    return {
        'x_prompt': nrm((BATCH, SEQ, D_MODEL), 1.0),
        'x_sample': nrm((DEC_BATCH, DEC_SEQ, D_MODEL), 1.0),
        'state_delta': nrm((DEC_BATCH, DEPTH, 2, H_A, HEAD_DIM_A, HEAD_DIM_A), 0.1),
        'state_s5_re': nrm((DEC_BATCH, DEPTH, 2, G, P), 0.05),
        'state_s5_im': nrm((DEC_BATCH, DEPTH, 2, G, P), 0.05),
        'cache_ckv': nrm((DEC_BATCH, DEPTH, PAST_LEN, KV_LORA), 1.0),
        'cache_krope': nrm((DEC_BATCH, DEPTH, PAST_LEN, QK_ROPE), 1.0),
        'c': nrm((DEC_BATCH, D_MODEL), 1.0),
        'c_ctx': nrm((D_MODEL,), 1.0),
        'w_mod': nrm((DEPTH, D_MODEL, 6 * D_MODEL), 0.5 * D_MODEL ** -0.5),
        'b_mod': nrm((DEPTH, 6 * D_MODEL), 0.02),
        'g_norm_mix': 1.0 + nrm((DEPTH, D_MODEL), 0.02),
        'g_norm_ffn': 1.0 + nrm((DEPTH, D_MODEL), 0.02),
        'w_in': nrm((DEPTH, D_MODEL, IN_WIDTH), D_MODEL ** -0.5),
        'conv_qkv': nrm((DEPTH, SHORT_CONV, 3 * A_WIDTH), SHORT_CONV ** -0.5),
        'a_log': jnp.log(unif((DEPTH, 2, H_A), 1.0, 16.0)),
        'dt_bias': dt_a + jnp.log(-jnp.expm1(-dt_a)),
        'g_delta_out': 1.0 + nrm((DEPTH, HEAD_DIM_A), 0.02),
        's5_lam_re': -0.5 + nrm((DEPTH, 2, G, P), 0.01),
        's5_lam_im': math.pi * jnp.arange(P, dtype=f32) + nrm((DEPTH, 2, G, P), 0.01),
        's5_log_dt': unif((DEPTH, 2, G), math.log(1e-3), math.log(1e-1)),
        's5_b_re': nrm((DEPTH, G, P, CG), (2 * CG) ** -0.5),
        's5_b_im': nrm((DEPTH, G, P, CG), (2 * CG) ** -0.5),
        's5_c_re': nrm((DEPTH, G, CG, P), P ** -0.5),
        's5_c_im': nrm((DEPTH, G, CG, P), P ** -0.5),
        's5_d': nrm((DEPTH, B_WIDTH), 1.0),
        'w_glu': nrm((DEPTH, B_WIDTH, B_WIDTH), B_WIDTH ** -0.5),
        'b_glu': nrm((DEPTH, B_WIDTH), 0.02),
        'g_q_a': 1.0 + nrm((DEPTH, Q_LORA), 0.02),
        'w_q_b': nrm((DEPTH, Q_LORA, H_C * (QK_NOPE + QK_ROPE)), Q_LORA ** -0.5),
        'g_kv_a': 1.0 + nrm((DEPTH, KV_LORA), 0.02),
        'w_kv_b': nrm((DEPTH, KV_LORA, H_C * (QK_NOPE + V_HEAD)), KV_LORA ** -0.5),
        'w_branch': nrm((DEPTH, N_BRANCH, BRANCH_WIDTH, D_MODEL), BRANCH_WIDTH ** -0.5),
        'w_out': nrm((DEPTH, D_MODEL, D_MODEL), D_MODEL ** -0.5),
        'w_ffn_up': nrm((DEPTH, D_MODEL, 2 * D_FF), D_MODEL ** -0.5),
        'conv_ffn': nrm((DEPTH, FFN_CONV, 2 * D_FF), FFN_CONV ** -0.5),
        'b_conv_ffn': nrm((DEPTH, 2 * D_FF), 0.02),
        'w_ffn_down': nrm((DEPTH, D_FF, D_MODEL), D_FF ** -0.5),
        'g_final': 1.0 + nrm((D_MODEL,), 0.02),
    }


def reference(x_prompt, x_sample, state_delta, state_s5_re, state_s5_im, cache_ckv, cache_krope, c, c_ctx,
              w_mod, b_mod, g_norm_mix, g_norm_ffn, w_in, conv_qkv, a_log, dt_bias, g_delta_out,
              s5_lam_re, s5_lam_im, s5_log_dt, s5_b_re, s5_b_im, s5_c_re, s5_c_im, s5_d, w_glu, b_glu,
              g_q_a, w_q_b, g_kv_a, w_kv_b, w_branch, w_out, w_ffn_up, conv_ffn, b_conv_ffn, w_ffn_down,
              g_final):
    xp = x_prompt.astype(jnp.float32)
    xs = x_sample.astype(jnp.float32)
    deltas, s5_res, s5_ims, ckvs, kropes = [], [], [], [], []
    for l in range(DEPTH):
        lp = {
            'g_norm_mix': g_norm_mix[l], 'g_norm_ffn': g_norm_ffn[l], 'w_in': w_in[l],
            'conv_qkv': conv_qkv[l], 'a_log': a_log[l], 'dt_bias': dt_bias[l], 'g_delta_out': g_delta_out[l],
            's5_lam_re': s5_lam_re[l], 's5_lam_im': s5_lam_im[l], 's5_log_dt': s5_log_dt[l],
            's5_b_re': s5_b_re[l], 's5_b_im': s5_b_im[l], 's5_c_re': s5_c_re[l], 's5_c_im': s5_c_im[l],
            's5_d': s5_d[l], 'w_glu': w_glu[l], 'b_glu': b_glu[l],
            'g_q_a': g_q_a[l], 'w_q_b': w_q_b[l], 'g_kv_a': g_kv_a[l], 'w_kv_b': w_kv_b[l],
            'w_branch': w_branch[l], 'w_out': w_out[l],
            'w_ffn_up': w_ffn_up[l], 'conv_ffn': conv_ffn[l], 'b_conv_ffn': b_conv_ffn[l],
            'w_ffn_down': w_ffn_down[l],
        }
        mods_p = modulation(c_ctx[None, :], w_mod[l], b_mod[l])
        xp, ctx_out = trunk_layer(xp, mods_p, lp, None)
        deltas.append(ctx_out[0])
        s5_res.append(ctx_out[1])
        s5_ims.append(ctx_out[2])
        ckvs.append(ctx_out[3])
        kropes.append(ctx_out[4])
        mods_s = modulation(c, w_mod[l], b_mod[l])
        cache_l = (state_delta[:, l], state_s5_re[:, l], state_s5_im[:, l], cache_ckv[:, l], cache_krope[:, l])
        xs, _ = trunk_layer(xs, mods_s, lp, cache_l)
    y_prompt = rmsnorm(xp, g_final)
    y_sample = rmsnorm(xs, g_final)
    new_state_delta = jnp.stack(deltas, axis=1)
    new_state_s5_re = jnp.stack(s5_res, axis=1)
    new_state_s5_im = jnp.stack(s5_ims, axis=1)
    new_cache_ckv = jnp.stack(ckvs, axis=1)
    new_cache_krope = jnp.stack(kropes, axis=1)
    return (y_prompt, y_sample, new_state_delta, new_state_s5_re, new_state_s5_im, new_cache_ckv, new_cache_krope)
```

```python
import functools
import math

import jax
import jax.numpy as jnp
from jax import lax
from jax.experimental import pallas as pl
from jax.experimental.pallas import tpu as pltpu

F32 = jnp.float32
BF16 = jnp.bfloat16

D_MODEL = 1024
DEPTH = 4
GRID_W = 64
H_A = 4
HEAD_DIM_A = 128
A_WIDTH = H_A * HEAD_DIM_A
SHORT_CONV = 5
CHUNK = 64
S5_GROUP = 16
S5_STATE = 64
B_WIDTH = 512
S5_GROUPS = B_WIDTH // S5_GROUP
S5_T = 16
H_C = 8
QK_NOPE = 64
QK_ROPE = 32
V_HEAD = 64
Q_LORA = 384
KV_LORA = 256
ROPE_BASE = 10000.0
N_BRANCH = 3
BRANCH_WIDTH = 512
D_FF = 2816
FFN_CONV = 3
NORM_EPS = 1e-6
HEAD_PAD = 128
SUB = 16

VMEM_LIMIT = 56 * 1024 * 1024
HALO = 16


def _cp(sem):
    return pltpu.CompilerParams(dimension_semantics=sem, vmem_limit_bytes=VMEM_LIMIT)


def _mm(a, b):
    return jnp.dot(a.astype(BF16), b.astype(BF16), preferred_element_type=F32)


def _mm_nt(a, b):
    return lax.dot_general(a.astype(BF16), b.astype(BF16), (((1,), (1,)), ((), ())),
                           preferred_element_type=F32)


def _mm_tn(a, b):
    return lax.dot_general(a.astype(BF16), b.astype(BF16), (((0,), (0,)), ((), ())),
                           preferred_element_type=F32)


def _split3(x):
    x1 = x.astype(BF16)
    r = x - x1.astype(F32)
    x2 = r.astype(BF16)
    x3 = (r - x2.astype(F32)).astype(BF16)
    return x1, x2, x3


def _mm_exact_rhs(a, b_bf16):
    a1, a2, a3 = _split3(a)
    d = lambda t: jnp.dot(t, b_bf16, preferred_element_type=F32)
    return d(a1) + d(a2) + d(a3)


def _mm_exact_lhs(a_bf16, b):
    b1, b2, b3 = _split3(b)
    d = lambda t: jnp.dot(a_bf16, t, preferred_element_type=F32)
    return d(b1) + d(b2) + d(b3)


def _mm3_nt(a, b):
    a1 = a.astype(BF16)
    a2 = (a - a1.astype(F32)).astype(BF16)
    b1 = b.astype(BF16)
    b2 = (b - b1.astype(F32)).astype(BF16)
    d = lambda s, t: lax.dot_general(s, t, (((1,), (1,)), ((), ())), preferred_element_type=F32)
    return d(a1, b1) + d(a1, b2) + d(a2, b1)


def _silu(x):
    return x * jax.nn.sigmoid(x)


def _rms(x):
    return x * lax.rsqrt(jnp.mean(x * x, axis=-1, keepdims=True) + NORM_EPS)


def _mod_kernel(c_ref, w_ref, b_ref, o_ref):
    s = _silu(c_ref[...])
    o_ref[0] = _mm(s, w_ref[0]) + b_ref[0]


def _modulation(cvec, w_mod, b_mod):
    rows = cvec.shape[0]
    nblk = w_mod.shape[-1] // D_MODEL
    return pl.pallas_call(
        _mod_kernel,
        grid=(DEPTH, nblk),
        in_specs=[
            pl.BlockSpec((rows, D_MODEL), lambda l, j: (0, 0)),
            pl.BlockSpec((1, D_MODEL, D_MODEL), lambda l, j: (l, 0, j)),
            pl.BlockSpec((1, 1, D_MODEL), lambda l, j: (l, 0, j)),
        ],
        out_specs=pl.BlockSpec((1, rows, D_MODEL), lambda l, j: (l, 0, j)),
        out_shape=jax.ShapeDtypeStruct((DEPTH, rows, w_mod.shape[-1]), F32),
        compiler_params=_cp(("parallel", "parallel")),
        name="modulation",
    )(cvec, w_mod, b_mod.reshape(DEPTH, 1, -1))


REST_W = 512 + 512 + Q_LORA + KV_LORA + 128
OFF_Z, OFF_U, OFF_QA, OFF_KVA, OFF_SM = 0, 512, 1024, 1024 + Q_LORA, 1024 + Q_LORA + KV_LORA
SM_KR = 16
MLA_W = H_C * HEAD_PAD


def _rope_apply(x, cos, sin):
    lane = lax.broadcasted_iota(jnp.int32, x.shape, 1)
    partner = jnp.where(lane < QK_NOPE + QK_ROPE // 2,
                        pltpu.roll(x, HEAD_PAD - QK_ROPE // 2, 1),
                        pltpu.roll(x, QK_ROPE // 2, 1))
    return x * cos + partner * sin


def _inproj_kernel(*refs, tl, tiles_per_seq, rope, emit_ctx):
    it = iter(refs)
    xp_ref, xc_ref, xn_ref = next(it), next(it), next(it)
    shift_ref, scale_ref, gnorm_ref = next(it), next(it), next(it)
    wqkv_ref, convw_ref, wrest_ref = next(it), next(it), next(it)
    gqa_ref, gkva_ref, alog_ref, dtb_ref = next(it), next(it), next(it), next(it)
    wq_ref, wk_ref, wv_ref = next(it), next(it), next(it)
    cos_ref = sin_ref = None
    if rope:
        cos_ref, sin_ref = next(it), next(it)
    qkv_out, zs_out, u_out, bg_out = next(it), next(it), next(it), next(it)
    ckv_out = kr_out = None
    if emit_ctx:
        ckv_out, kr_out = next(it), next(it)
    qh_out, kh_out, vh_out = next(it), next(it), next(it)
    qkv_scr = next(it)

    i = pl.program_id(0)
    pos = i % tiles_per_seq
    mod_scale = 1.0 + scale_ref[0]
    mod_shift = shift_ref[0]
    gain = gnorm_ref[...]

    def norm_mod(x):
        return (_rms(x) * gain) * mod_scale + mod_shift

    h_cur = norm_mod(xc_ref[...]).astype(BF16)
    h_prev = norm_mod(xp_ref[...]).astype(BF16)
    h_next = norm_mod(xn_ref[...]).astype(BF16)

    w_qkv = wqkv_ref[...]
    prev_ok = (pos > 0).astype(F32)
    next_ok = (pos < tiles_per_seq - 1).astype(F32)
    qkv_scr[0:HALO, :] = _mm(h_prev, w_qkv) * prev_ok
    qkv_scr[HALO:HALO + tl, :] = _mm(h_cur, w_qkv)
    qkv_scr[HALO + tl:HALO + tl + HALO, :] = _mm(h_next, w_qkv) * next_ok
    pad = SHORT_CONV // 2
    for blk in range(3):
        cols = slice(blk * A_WIDTH, (blk + 1) * A_WIDTH)
        acc = None
        for t in range(SHORT_CONV):
            term = qkv_scr[pl.ds(HALO - pad + t, tl), cols] * convw_ref[t:t + 1, cols]
            acc = term if acc is None else acc + term
        acc = _silu(acc)
        if blk < 2:
            parts = []
            for hh in range(H_A):
                a = acc[:, hh * HEAD_DIM_A:(hh + 1) * HEAD_DIM_A]
                parts.append(a * lax.rsqrt(jnp.sum(a * a, axis=-1, keepdims=True) + NORM_EPS))
            acc = jnp.concatenate(parts, axis=1)
        qkv_out[:, cols] = acc.astype(BF16)

    rest = _mm(h_cur, wrest_ref[...])
    zs_out[...] = _silu(rest[:, OFF_Z:OFF_Z + 512])
    u_out[...] = rest[:, OFF_U:OFF_U + 512]

    small = rest[:, OFF_SM:OFF_SM + 128]
    lane = lax.broadcasted_iota(jnp.int32, small.shape, 1)
    beta = jax.nn.sigmoid(small)
    glog = -jnp.exp(alog_ref[...]) * jax.nn.softplus(small + dtb_ref[...])
    bg_out[...] = jnp.where(lane < 2 * H_A, beta, glog)

    qa = _rms(rest[:, OFF_QA:OFF_QA + Q_LORA]) * gqa_ref[...]
    ckv = _rms(rest[:, OFF_KVA:OFF_KVA + KV_LORA]) * gkva_ref[...]
    if emit_ctx:
        ckv_out[...] = ckv
        kr_out[...] = small[:, SM_KR:SM_KR + QK_ROPE]
    ckv_b = ckv.astype(BF16)
    qhat = _mm(qa, wq_ref[...])
    khat = _mm(ckv_b, wk_ref[...])
    vh_out[...] = _mm(ckv_b, wv_ref[...]).astype(BF16)
    kr_al = jnp.where((lane >= QK_NOPE) & (lane < QK_NOPE + QK_ROPE),
                      pltpu.roll(small, QK_NOPE - SM_KR, 1), 0.0)
    if rope:
        cos, sin = cos_ref[...], sin_ref[...]
        kr_al = _rope_apply(kr_al, cos, sin)
    qscale = (QK_NOPE + QK_ROPE) ** -0.5
    for hh in range(H_C):
        cs = slice(hh * HEAD_PAD, (hh + 1) * HEAD_PAD)
        qh = qhat[:, cs]
        if rope:
            qh = _rope_apply(qh, cos, sin)
        qh_out[:, cs] = (qh * qscale).astype(BF16)
        kh_out[:, cs] = (khat[:, cs] + kr_al).astype(BF16)


def _inproj(x, mods_rows, row_of_tile, lp, seq_len, tl, rope_tabs, emit_ctx):
    n = x.shape[0]
    nt = n // tl
    tps = seq_len // tl
    nh8 = n // HALO
    rope = rope_tabs is not None

    def xprev(i):
        return (jnp.maximum(i * (tl // HALO) - 1, 0), 0)

    def xnext(i):
        return (jnp.minimum((i + 1) * (tl // HALO), nh8 - 1), 0)

    const2 = lambda i: (0, 0)
    in_specs = [
        pl.BlockSpec((HALO, D_MODEL), xprev),
        pl.BlockSpec((tl, D_MODEL), lambda i: (i, 0)),
        pl.BlockSpec((HALO, D_MODEL), xnext),
        pl.BlockSpec((1, 1, D_MODEL), lambda i: (row_of_tile(i) * 6 + 0, 0, 0)),
        pl.BlockSpec((1, 1, D_MODEL), lambda i: (row_of_tile(i) * 6 + 1, 0, 0)),
        pl.BlockSpec((1, D_MODEL), const2),
        pl.BlockSpec((D_MODEL, 3 * A_WIDTH), const2),
        pl.BlockSpec((SHORT_CONV, 3 * A_WIDTH), const2),
        pl.BlockSpec((D_MODEL, REST_W), const2),
        pl.BlockSpec((1, Q_LORA), const2),
        pl.BlockSpec((1, KV_LORA), const2),
        pl.BlockSpec((1, 128), const2),
        pl.BlockSpec((1, 128), const2),
        pl.BlockSpec((Q_LORA, MLA_W), const2),
        pl.BlockSpec((KV_LORA, MLA_W), const2),
        pl.BlockSpec((KV_LORA, MLA_W), const2),
    ]
    args = [x, x, x, mods_rows, mods_rows, lp['g_norm_mix'], lp['w_qkv'], lp['conv_qkv'], lp['w_rest'],
            lp['g_q_a'], lp['g_kv_a'], lp['a_log128'], lp['dt_bias128'], lp['w_q_pad'], lp['w_k_pad'], lp['w_v_pad']]
    if rope:
        in_specs += [pl.BlockSpec((tl, HEAD_PAD), lambda i: (i % tps, 0))] * 2
        args += list(rope_tabs)
    tok = lambda w: pl.BlockSpec((tl, w), lambda i: (i, 0))
    out_specs = [tok(3 * A_WIDTH), tok(512), tok(512), tok(128)]
    out_shape = [jax.ShapeDtypeStruct((n, 3 * A_WIDTH), BF16), jax.ShapeDtypeStruct((n, 512), F32),
                 jax.ShapeDtypeStruct((n, 512), F32), jax.ShapeDtypeStruct((n, 128), F32)]
    if emit_ctx:
        out_specs += [tok(KV_LORA), tok(QK_ROPE)]
        out_shape += [jax.ShapeDtypeStruct((n, KV_LORA), F32), jax.ShapeDtypeStruct((n, QK_ROPE), F32)]
    out_specs += [tok(MLA_W)] * 3
    out_shape += [jax.ShapeDtypeStruct((n, MLA_W), BF16)] * 3
    return pl.pallas_call(
        functools.partial(_inproj_kernel, tl=tl, tiles_per_seq=tps, rope=rope, emit_ctx=emit_ctx),
        grid=(nt,),
        in_specs=in_specs,
        out_specs=out_specs,
        out_shape=out_shape,
        scratch_shapes=[pltpu.VMEM((tl + 2 * HALO, 3 * A_WIDTH), F32)],
        compiler_params=_cp(("parallel",)),
        name="inproj",
    )(*args)


def _kvcache_kernel(ckv_ref, kr_ref, wk_ref, wv_ref, kh_out, vh_out):
    ckv_b = ckv_ref[...].astype(BF16)
    khat = _mm(ckv_b, wk_ref[...])
    vh_out[...] = _mm(ckv_b, wv_ref[...]).astype(BF16)
    kr_al = kr_ref[...]
    for hh in range(H_C):
        cs = slice(hh * HEAD_PAD, (hh + 1) * HEAD_PAD)
        kh_out[:, cs] = (khat[:, cs] + kr_al).astype(BF16)


def _kvcache(ckv, kr, lp, tl):
    n = ckv.shape[0]
    const2 = lambda i: (0, 0)
    kr = jnp.pad(kr, ((0, 0), (QK_NOPE, HEAD_PAD - QK_NOPE - QK_ROPE)))
    return pl.pallas_call(
        _kvcache_kernel,
        grid=(n // tl,),
        in_specs=[pl.BlockSpec((tl, KV_LORA), lambda i: (i, 0)), pl.BlockSpec((tl, HEAD_PAD), lambda i: (i, 0)),
                  pl.BlockSpec((KV_LORA, MLA_W), const2), pl.BlockSpec((KV_LORA, MLA_W), const2)],
        out_specs=[pl.BlockSpec((tl, MLA_W), lambda i: (i, 0))] * 2,
        out_shape=[jax.ShapeDtypeStruct((n, MLA_W), BF16)] * 2,
        compiler_params=_cp(("parallel",)),
        name="kvcache",
    )(ckv, kr, lp['w_k_pad'], lp['w_v_pad'])


def _attn_kernel(*refs, nseg):
    q_ref = refs[0]
    k_refs = refs[1:1 + nseg]
    v_refs = refs[1 + nseg:1 + 2 * nseg]
    o_ref = refs[1 + 2 * nseg]
    q = q_ref[...]
    s = [lax.dot_general(q, k[...], (((1,), (1,)), ((), ())), preferred_element_type=F32) for k in k_refs]
    m = s[0].max(axis=-1, keepdims=True)
    for t in s[1:]:
        m = jnp.maximum(m, t.max(axis=-1, keepdims=True))
    p = [jnp.exp(t - m) for t in s]
    den = p[0].sum(axis=-1, keepdims=True)
    for t in p[1:]:
        den = den + t.sum(axis=-1, keepdims=True)
    acc = None
    for t, v in zip(p, v_refs):
        part = jnp.dot(t.astype(BF16), v[...], preferred_element_type=F32)
        acc = part if acc is None else acc + part
    o_ref[...] = (acc / den).astype(BF16)


def _attention(qh, segs, bsz, seq_len, tq):
    nq = seq_len // tq
    in_specs = [pl.BlockSpec((tq, HEAD_PAD), lambda b, h, i: (b * nq + i, h))]
    args = [qh]
    for which in (0, 1):
        for seg in segs:
            in_specs.append(pl.BlockSpec((seg[2], HEAD_PAD), lambda b, h, i: (b, h)))
            args.append(seg[which])
    return pl.pallas_call(
        functools.partial(_attn_kernel, nseg=len(segs)),
        grid=(bsz, H_C, nq),
        in_specs=in_specs,
        out_specs=pl.BlockSpec((tq, HEAD_PAD), lambda b, h, i: (b * nq + i, h)),
        out_shape=jax.ShapeDtypeStruct((bsz * seq_len, MLA_W), BF16),
        compiler_params=_cp(("parallel", "parallel", "parallel")),
        name="attention",
    )(*args)


def _delta_kernel(*refs, seq_len, has_s0, emit_state):
    it = iter(refs)
    qkv_ref, bg_ref, bgt_ref, zs_ref, gout_ref = next(it), next(it), next(it), next(it), next(it)
    s0_ref = next(it) if has_s0 else None
    o_ref = next(it)
    sfin_ref = next(it) if emit_state else None
    s_scr, o_scr = next(it), next(it)

    nc = seq_len // CHUNK
    c = CHUNK
    if has_s0:
        s_scr[...] = s0_ref[0]
    else:
        s_scr[...] = jnp.zeros(s_scr.shape, F32)
    o_scr[...] = jnp.zeros(o_scr.shape, F32)

    ri = lax.broadcasted_iota(jnp.int32, (c, c), 0)
    ci = lax.broadcasted_iota(jnp.int32, (c, c), 1)
    tril = (ri >= ci).astype(BF16)
    triu = (ri <= ci).astype(BF16)
    same_blk = (ri // SUB) == (ci // SUB)
    scale = HEAD_DIM_A ** -0.5

    def one_direction(d, ch):
        rows = pl.ds(pl.multiple_of(ch * c, c), c)
        bgc = bg_ref[rows, :]
        bgr = bgt_ref[0, ch]
        if d == 0:
            incl, strict = ri >= ci, ri > ci
            gc_col = _mm_exact_lhs(tril, bgc)
            gc_row = _mm_exact_rhs(bgr, triu)
            last = c - 1
        else:
            incl, strict = ri <= ci, ri < ci
            gc_col = _mm_exact_lhs(triu, bgc)
            gc_row = _mm_exact_rhs(bgr, tril)
            last = 0
        for hh in range(H_A):
            col = d * H_A + hh
            beta = bgc[:, col:col + 1]
            gcc = gc_col[:, 2 * H_A + col:2 * H_A + col + 1]
            gcr = gc_row[2 * H_A + col:2 * H_A + col + 1, :]
            glast = gcc[last:last + 1, :]
            q = qkv_ref[rows, hh * HEAD_DIM_A:(hh + 1) * HEAD_DIM_A]
            k = qkv_ref[rows, A_WIDTH + hh * HEAD_DIM_A:A_WIDTH + (hh + 1) * HEAD_DIM_A]
            v = qkv_ref[rows, 2 * A_WIDTH + hh * HEAD_DIM_A:2 * A_WIDTH + (hh + 1) * HEAD_DIM_A]
            kf = k.astype(F32)
            kb = kf * beta
            decay = jnp.where(incl, jnp.exp(jnp.where(incl, gcc - gcr, 0.0)), 0.0)
            a = _mm_nt(kb, k) * decay
            qk = _mm_nt(q, k) * (decay * scale)
            dg = jnp.where(same_blk & strict, a, 0.0)
            lo = jnp.where(jnp.logical_not(same_blk) & strict, a, 0.0)
            m1 = _mm(dg, dg)
            p = m1 - dg - _mm(dg, m1)
            pw = m1
            for _ in range(int(math.log2(SUB)) - 2):
                pw = _mm(pw, pw)
                p = p + pw + _mm(p, pw)
            nmat = lo + _mm(p, lo)
            n2 = _mm(nmat, nmat)
            e_col = jnp.exp(gcc)
            rhs = jnp.concatenate([v.astype(F32) * beta, kb * e_col], axis=1)
            y = rhs + _mm(p, rhs)
            y = y + _mm(n2, y)
            x = y - _mm(nmat, y)
            u, w = x[:, :HEAD_DIM_A], x[:, HEAD_DIM_A:]
            s_old = s_scr[d, hh]
            v_new = u - _mm(w, s_old)
            o = _mm(q.astype(F32) * (e_col * scale), s_old) + _mm(qk, v_new)
            k_dec = kf * jnp.exp(glast - gcc)
            s_scr[d, hh] = s_old * jnp.exp(glast) + _mm_tn(k_dec, v_new)
            cs = slice(hh * HEAD_DIM_A, (hh + 1) * HEAD_DIM_A)
            o_scr[rows, cs] = o_scr[rows, cs] + o

    def body(i, carry):
        one_direction(0, i)
        one_direction(1, nc - 1 - i)
        return carry

    lax.fori_loop(0, nc, body, 0)

    gout = gout_ref[...]
    blk = 256
    for r in range(seq_len // blk):
        rows = slice(r * blk, (r + 1) * blk)
        for hh in range(H_A):
            cs = slice(hh * HEAD_DIM_A, (hh + 1) * HEAD_DIM_A)
            o_ref[rows, cs] = (_rms(o_scr[rows, cs]) * gout * zs_ref[rows, cs]).astype(BF16)
    if emit_state:
        sfin_ref[0] = s_scr[...]


def _delta(qkv, bg, bgt, zs, g_out, s0, bsz, seq_len, emit_state):
    nc = seq_len // CHUNK
    has_s0 = s0 is not None
    in_specs = [
        pl.BlockSpec((seq_len, 3 * A_WIDTH), lambda b: (b, 0)),
        pl.BlockSpec((seq_len, 128), lambda b: (b, 0)),
        pl.BlockSpec((1, nc, 16, CHUNK), lambda b: (b, 0, 0, 0)),
        pl.BlockSpec((seq_len, A_WIDTH), lambda b: (b, 0)),
        pl.BlockSpec((1, HEAD_DIM_A), lambda b: (0, 0)),
    ]
    args = [qkv, bg, bgt, zs, g_out]
    st_block = pl.BlockSpec((1, 2, H_A, HEAD_DIM_A, HEAD_DIM_A), lambda b: (b, 0, 0, 0, 0))
    if has_s0:
        in_specs.append(st_block)
        args.append(s0)
    out_specs = [pl.BlockSpec((seq_len, A_WIDTH), lambda b: (b, 0))]
    out_shape = [jax.ShapeDtypeStruct((bsz * seq_len, A_WIDTH), BF16)]
    if emit_state:
        out_specs.append(st_block)
        out_shape.append(jax.ShapeDtypeStruct((bsz, 2, H_A, HEAD_DIM_A, HEAD_DIM_A), F32))
    return pl.pallas_call(
        functools.partial(_delta_kernel, seq_len=seq_len, has_s0=has_s0, emit_state=emit_state),
        grid=(bsz,),
        in_specs=in_specs,
        out_specs=out_specs,
        out_shape=out_shape,
        scratch_shapes=[pltpu.VMEM((2, H_A, HEAD_DIM_A, HEAD_DIM_A), F32), pltpu.VMEM((seq_len, A_WIDTH), F32)],
        compiler_params=_cp(("parallel",)),
        name="delta",
    )(*args)


S5_ROWS = S5_T * S5_GROUP


def _s5prep_kernel(lre_ref, lim_ref, ldt_ref, bre_ref, bim_ref, cre_ref, cim_ref,
                   kt_out, ere_out, eim_out, fre_out, fim_out, lt_out, *, backward):
    lre, lim = lre_ref[0], lim_ref[0]
    dt = jnp.exp(ldt_ref[0])
    zr, zi = lre * dt, lim * dt

    def lam_pow(e):
        mag = jnp.exp(e * zr)
        return mag * jnp.cos(e * zi), mag * jnp.sin(e * zi)

    one = jnp.ones((1, 1), F32)
    l1r, l1i = lam_pow(one)
    den = lre * lre + lim * lim
    nr, ni = l1r - 1.0, l1i
    cfr = (nr * lre + ni * lim) / den
    cfi = (ni * lre - nr * lim) / den
    bre, bim = bre_ref[0], bim_ref[0]
    bbr = cfr * bre - cfi * bim
    bbi = cfr * bim + cfi * bre
    cre, cim = cre_ref[0], cim_ref[0]

    j = (lax.broadcasted_iota(jnp.int32, (S5_ROWS, 1), 0) // S5_GROUP).astype(F32)
    tile = lambda m: jnp.concatenate([m] * S5_T, axis=0)
    c_r, c_i = tile(cre), tile(cim)
    b_r, b_i = tile(bbr), tile(bbi)

    pr, pi = lam_pow(j)
    qr = pr * c_r - pi * c_i
    qi = pr * c_i + pi * c_r
    kt_out[0] = _mm3_nt(qr, bbr) - _mm3_nt(qi, bbi)

    e_exp = j if backward else (S5_T - 1.0) - j
    er, ei = lam_pow(e_exp)
    ere_out[0] = er * b_r - ei * b_i
    eim_out[0] = er * b_i + ei * b_r

    f_exp = (S5_T - j) if backward else j + 1.0
    fr, fi = lam_pow(f_exp)
    fre_out[0] = fr * c_r - fi * c_i
    fim_out[0] = -(fr * c_i + fi * c_r)

    ltr, lti = lam_pow(one * float(S5_T))
    lt_out[0] = jnp.concatenate([ltr, lti], axis=0)


def _s5prep(lam_re, lam_im, log_dt, b_re_t, b_im_t, c_re, c_im, direction):
    g, p, cg = S5_GROUPS, S5_STATE, S5_GROUP
    n = DEPTH * g
    lam_idx = lambda i: ((i // g) * 2 * g + direction * g + i % g, 0, 0)
    par_idx = lambda i: (i, 0, 0)
    outs = pl.pallas_call(
        functools.partial(_s5prep_kernel, backward=bool(direction)),
        grid=(n,),
        in_specs=[pl.BlockSpec((1, 1, p), lam_idx), pl.BlockSpec((1, 1, p), lam_idx), pl.BlockSpec((1, 1, 1), lam_idx),
                  pl.BlockSpec((1, cg, p), par_idx), pl.BlockSpec((1, cg, p), par_idx),
                  pl.BlockSpec((1, cg, p), par_idx), pl.BlockSpec((1, cg, p), par_idx)],
        out_specs=[pl.BlockSpec((1, S5_ROWS, cg), par_idx)] + [pl.BlockSpec((1, S5_ROWS, p), par_idx)] * 4
                  + [pl.BlockSpec((1, 2, p), par_idx)],
        out_shape=[jax.ShapeDtypeStruct((n, S5_ROWS, cg), F32)] + [jax.ShapeDtypeStruct((n, S5_ROWS, p), F32)] * 4
                  + [jax.ShapeDtypeStruct((n, 2, p), F32)],
        compiler_params=_cp(("parallel",)),
        name="s5prep_bwd" if direction else "s5prep_fwd",
    )(lam_re.reshape(DEPTH * 2 * g, 1, p), lam_im.reshape(DEPTH * 2 * g, 1, p), log_dt.reshape(DEPTH * 2 * g, 1, 1),
      b_re_t, b_im_t, c_re.reshape(n, cg, p), c_im.reshape(n, cg, p))
    kt, ere, eim, fre, fim, lt = outs
    kt4 = kt.reshape(n, S5_T, cg, cg)
    tt = jnp.arange(S5_T)[:, None]
    ss = jnp.arange(S5_T)[None, :]
    lag = (ss - tt) if direction else (tt - ss)
    blocks = jnp.where((lag >= 0)[None, :, :, None, None], kt4[:, jnp.clip(lag, 0, S5_T - 1)], 0.0)
    mt = blocks.transpose(0, 1, 3, 2, 4).reshape(n, S5_ROWS, S5_ROWS).astype(BF16)
    emat = jnp.concatenate([ere, eim], axis=-1).astype(BF16)
    fmat = jnp.concatenate([fre, fim], axis=-1).astype(BF16)
    ltrow = lt.reshape(n, 1, 2 * p)
    return mt, emat, fmat, ltrow


def _s5_kernel(*refs, nchunk, bsz, has_h0, emit_state):
    it = iter(refs)
    u_ref = next(it)
    mtf_ref, mtb_ref, ef_ref, eb_ref, ff_ref, fb_ref, ltf_ref, ltb_ref = (next(it) for _ in range(8))
    h0_ref = next(it) if has_h0 else None
    y_ref = next(it)
    hfin_ref = next(it) if emit_state else None
    sf_scr, sb_scr, hf_scr, hb_scr = next(it), next(it), next(it), next(it)

    u = u_ref[0]
    sf_scr[...] = jnp.dot(u, ef_ref[0], preferred_element_type=F32)
    sb_scr[...] = jnp.dot(u, eb_ref[0], preferred_element_type=F32)
    p = S5_STATE
    lane = lax.broadcasted_iota(jnp.int32, (1, 2 * p), 1)

    def coeffs(lt_ref):
        lt = lt_ref[0]
        sw = pltpu.roll(lt, p, 1)
        a = jnp.where(lane < p, lt, sw)
        b = jnp.where(lane < p, -sw, lt)
        return a, b

    af, bf = coeffs(ltf_ref)
    ab, bb = coeffs(ltb_ref)
    if has_h0:
        h0f, h0b = h0_ref[0, 0], h0_ref[0, 1]
    else:
        h0f = h0b = jnp.zeros((bsz, 2 * p), F32)

    def body(k, carry):
        hf, hb = carry
        rf = pl.ds(pl.multiple_of(k * bsz, bsz), bsz)
        rb = pl.ds(pl.multiple_of((nchunk - 1 - k) * bsz, bsz), bsz)
        hf_scr[rf, :] = hf
        hb_scr[rb, :] = hb
        hf = af * hf + bf * pltpu.roll(hf, p, 1) + sf_scr[rf, :]
        hb = ab * hb + bb * pltpu.roll(hb, p, 1) + sb_scr[rb, :]
        return hf, hb

    hf, hb = lax.fori_loop(0, nchunk, body, (h0f, h0b))
    if emit_state:
        hfin_ref[0, 0] = hf
        hfin_ref[0, 1] = hb

    nt = lambda a, b: lax.dot_general(a, b, (((1,), (1,)), ((), ())), preferred_element_type=F32)
    y = nt(u, mtf_ref[0]) + nt(u, mtb_ref[0])
    y = y + nt(hf_scr[...].astype(BF16), ff_ref[0]) + nt(hb_scr[...].astype(BF16), fb_ref[0])
    y_ref[0] = y


def _s5(u_t, mats_l, h0, nchunk, bsz, emit_state):
    g, p = S5_GROUPS, S5_STATE
    r = nchunk * bsz
    has_h0 = h0 is not None
    gi = lambda i: (i, 0, 0)
    in_specs = [pl.BlockSpec((1, r, S5_ROWS), gi),
                pl.BlockSpec((1, S5_ROWS, S5_ROWS), gi), pl.BlockSpec((1, S5_ROWS, S5_ROWS), gi),
                pl.BlockSpec((1, S5_ROWS, 2 * p), gi), pl.BlockSpec((1, S5_ROWS, 2 * p), gi),
                pl.BlockSpec((1, S5_ROWS, 2 * p), gi), pl.BlockSpec((1, S5_ROWS, 2 * p), gi),
                pl.BlockSpec((1, 1, 2 * p), gi), pl.BlockSpec((1, 1, 2 * p), gi)]
    args = [u_t] + list(mats_l)
    st_spec = pl.BlockSpec((1, 2, bsz, 2 * p), lambda i: (i, 0, 0, 0))
    if has_h0:
        in_specs.append(st_spec)
        args.append(h0)
    out_specs = [pl.BlockSpec((1, r, S5_ROWS), gi)]
    out_shape = [jax.ShapeDtypeStruct((g, r, S5_ROWS), F32)]
    if emit_state:
        out_specs.append(st_spec)
        out_shape.append(jax.ShapeDtypeStruct((g, 2, bsz, 2 * p), F32))
    return pl.pallas_call(
        functools.partial(_s5_kernel, nchunk=nchunk, bsz=bsz, has_h0=has_h0, emit_state=emit_state),
        grid=(g,),
        in_specs=in_specs,
        out_specs=out_specs,
        out_shape=out_shape,
        scratch_shapes=[pltpu.VMEM((r, 2 * p), F32)] * 4,
        compiler_params=_cp(("parallel",)),
        name="s5",
    )(*args)


def _merge_kernel(x_ref, shift_ref, scale_ref, gate_ref, gnorm_ref, oa_ref, ys_ref, u_ref, oc_ref,
                  wg_ref, wba_ref, wbb_ref, wbc_ref, wout_ref, wglu_ref, bglu_ref, dskip_ref, xo_ref):
    x = x_ref[...]
    h = ((_rms(x) * gnorm_ref[...]) * (1.0 + scale_ref[0]) + shift_ref[0]).astype(BF16)
    yb = jax.nn.gelu(ys_ref[...] + dskip_ref[...] * u_ref[...])
    ob = yb * jax.nn.sigmoid(_mm(yb, wglu_ref[...]) + bglu_ref[...])
    w_g = wg_ref
    acc = None
    for nbr, (o, w) in enumerate(((oa_ref[...], wba_ref), (ob, wbb_ref), (oc_ref[...], wbc_ref))):
        gate = jax.nn.sigmoid(jnp.dot(h, w_g[:, nbr * D_MODEL:(nbr + 1) * D_MODEL], preferred_element_type=F32))
        term = gate * _mm(o, w[...])
        acc = term if acc is None else acc + term
    out = _mm(acc, wout_ref[...])
    xo_ref[...] = x + gate_ref[0] * out


def _merge(x, mods_rows, row_of_tile, lp, oa, ys, u, oc, tl):
    n = x.shape[0]
    const2 = lambda i: (0, 0)
    tok = lambda w: pl.BlockSpec((tl, w), lambda i: (i, 0))
    mod = lambda j: pl.BlockSpec((1, 1, D_MODEL), lambda i: (row_of_tile(i) * 6 + j, 0, 0))
    return pl.pallas_call(
        _merge_kernel,
        grid=(n // tl,),
        in_specs=[tok(D_MODEL), mod(0), mod(1), mod(2), pl.BlockSpec((1, D_MODEL), const2),
                  tok(A_WIDTH), tok(B_WIDTH), tok(B_WIDTH), tok(MLA_W),
                  pl.BlockSpec((D_MODEL, N_BRANCH * D_MODEL), const2),
                  pl.BlockSpec((BRANCH_WIDTH, D_MODEL), const2), pl.BlockSpec((BRANCH_WIDTH, D_MODEL), const2),
                  pl.BlockSpec((MLA_W, D_MODEL), const2), pl.BlockSpec((D_MODEL, D_MODEL), const2),
                  pl.BlockSpec((B_WIDTH, B_WIDTH), const2), pl.BlockSpec((1, B_WIDTH), const2),
                  pl.BlockSpec((1, B_WIDTH), const2)],
        out_specs=tok(D_MODEL),
        out_shape=jax.ShapeDtypeStruct((n, D_MODEL), F32),
        compiler_params=_cp(("parallel",)),
        name="merge",
    )(x, mods_rows, mods_rows, mods_rows, lp['g_norm_mix'], oa, ys, u, oc, lp['w_gates'], lp['w_br_a'], lp['w_br_b'],
      lp['w_br_c_pad'], lp['w_out'], lp['w_glu'], lp['b_glu'], lp['s5_d'])


FF_BLK = 256


def _ffn_kernel(*refs, tl, tiles_per_seq, final):
    it = iter(refs)
    xp_ref, xc_ref, xn_ref = next(it), next(it), next(it)
    shift_ref, scale_ref, gate_ref, gnorm_ref = next(it), next(it), next(it), next(it)
    wup_ref, convw_ref, convb_ref, wdown_ref = next(it), next(it), next(it), next(it)
    gfin_ref = next(it) if final else None
    xo_ref = next(it)
    yo_ref = next(it) if final else None
    h_scr, acc_scr = next(it), next(it)

    i = pl.program_id(0)
    pos = i % tiles_per_seq
    mod_scale = 1.0 + scale_ref[0]
    mod_shift = shift_ref[0]
    gain = gnorm_ref[...]

    def norm_mod(x):
        return ((_rms(x) * gain) * mod_scale + mod_shift).astype(BF16)

    x = xc_ref[...]
    h_scr[0:HALO, :] = norm_mod(xp_ref[...])
    h_scr[HALO:HALO + tl, :] = norm_mod(x)
    h_scr[HALO + tl:HALO + tl + HALO, :] = norm_mod(xn_ref[...])
    rowi = lax.broadcasted_iota(jnp.int32, (tl + 2 * HALO, 1), 0)
    valid = jnp.logical_and(jnp.logical_or(pos > 0, rowi >= HALO),
                            jnp.logical_or(pos < tiles_per_seq - 1, rowi < HALO + tl)).astype(F32)
    h_ext = h_scr[...]
    pad = FFN_CONV // 2
    n_ext = tl + 2 * HALO

    def conv_act(cols):
        up = jnp.dot(h_ext, wup_ref[:, cols], preferred_element_type=F32) * valid
        acc = None
        for t in range(FFN_CONV):
            sh = (pad - t) % n_ext
            src = up if sh == 0 else pltpu.roll(up, sh, 0)
            term = src[HALO:HALO + tl, :] * convw_ref[t:t + 1, cols]
            acc = term if acc is None else acc + term
        return acc + convb_ref[:, cols]

    for j in range(D_FF // FF_BLK):
        gcols = slice(j * FF_BLK, (j + 1) * FF_BLK)
        vcols = slice(D_FF + j * FF_BLK, D_FF + (j + 1) * FF_BLK)
        act = (_silu(conv_act(gcols)) * conv_act(vcols)).astype(BF16)
        part = jnp.dot(act, wdown_ref[gcols, :], preferred_element_type=F32)
        if j == 0:
            acc_scr[...] = part
        else:
            acc_scr[...] += part
    xo = x + gate_ref[0] * acc_scr[...]
    xo_ref[...] = xo
    if final:
        yo_ref[...] = _rms(xo) * gfin_ref[...]


def _ffn(x, mods_rows, row_of_tile, lp, seq_len, tl, g_final):
    n = x.shape[0]
    tps = seq_len // tl
    nh8 = n // HALO
    final = g_final is not None

    def xprev(i):
        return (jnp.maximum(i * (tl // HALO) - 1, 0), 0)

    def xnext(i):
        return (jnp.minimum((i + 1) * (tl // HALO), nh8 - 1), 0)

    const2 = lambda i: (0, 0)
    mod = lambda j: pl.BlockSpec((1, 1, D_MODEL), lambda i: (row_of_tile(i) * 6 + j, 0, 0))
    tok = pl.BlockSpec((tl, D_MODEL), lambda i: (i, 0))
    in_specs = [pl.BlockSpec((HALO, D_MODEL), xprev), tok, pl.BlockSpec((HALO, D_MODEL), xnext),
                mod(3), mod(4), mod(5), pl.BlockSpec((1, D_MODEL), const2),
                pl.BlockSpec((D_MODEL, 2 * D_FF), const2), pl.BlockSpec((FFN_CONV, 2 * D_FF), const2),
                pl.BlockSpec((1, 2 * D_FF), const2), pl.BlockSpec((D_FF, D_MODEL), const2)]
    args = [x, x, x, mods_rows, mods_rows, mods_rows, lp['g_norm_ffn'], lp['w_ffn_up'], lp['conv_ffn'],
            lp['b_conv_ffn'], lp['w_ffn_down']]
    out_specs = [tok]
    out_shape = [jax.ShapeDtypeStruct((n, D_MODEL), F32)]
    if final:
        in_specs.append(pl.BlockSpec((1, D_MODEL), const2))
        args.append(g_final)
        out_specs.append(tok)
        out_shape.append(jax.ShapeDtypeStruct((n, D_MODEL), F32))
    return pl.pallas_call(
        functools.partial(_ffn_kernel, tl=tl, tiles_per_seq=tps, final=final),
        grid=(n // tl,),
        in_specs=in_specs,
        out_specs=out_specs,
        out_shape=out_shape,
        scratch_shapes=[pltpu.VMEM((tl + 2 * HALO, D_MODEL), BF16), pltpu.VMEM((tl, D_MODEL), F32)],
        compiler_params=_cp(("parallel",)),
        name="ffn",
    )(*args)


def _pad_heads(w, head_w, n_heads):
    k = w.shape[0]
    w = w.reshape(k, n_heads, head_w)
    w = jnp.pad(w, ((0, 0), (0, 0), (0, HEAD_PAD - head_w)))
    return w.reshape(k, n_heads * HEAD_PAD)


def _layer_params(l, p):
    w_in = p['w_in'][l]
    o = 0
    parts = {}
    for name, wd in (('qkv', 3 * A_WIDTH), ('z', A_WIDTH), ('beta', 2 * H_A), ('alpha', 2 * H_A), ('u', B_WIDTH),
                     ('qa', Q_LORA), ('kva', KV_LORA), ('kr', QK_ROPE), ('gates', N_BRANCH * D_MODEL)):
        parts[name] = w_in[:, o:o + wd]
        o += wd
    small = jnp.concatenate([parts['beta'], parts['alpha'], parts['kr'],
                             jnp.zeros((D_MODEL, 128 - 4 * H_A - QK_ROPE), F32)], axis=1)
    w_rest = jnp.concatenate([parts['z'], parts['u'], parts['qa'], parts['kva'], small], axis=1)
    pad8 = lambda v: jnp.pad(v.reshape(1, 2 * H_A), ((0, 0), (2 * H_A, 128 - 4 * H_A)))
    w_kv = p['w_kv_b'][l].reshape(KV_LORA, H_C, QK_NOPE + V_HEAD)
    w_k = w_kv[:, :, :QK_NOPE].reshape(KV_LORA, H_C * QK_NOPE)
    w_v = w_kv[:, :, QK_NOPE:].reshape(KV_LORA, H_C * V_HEAD)
    w_br_c = p['w_branch'][l, 2].reshape(H_C, V_HEAD, D_MODEL)
    w_br_c = jnp.pad(w_br_c, ((0, 0), (0, HEAD_PAD - V_HEAD), (0, 0))).reshape(MLA_W, D_MODEL)
    row = lambda v: v.reshape(1, -1)
    return {
        'g_norm_mix': row(p['g_norm_mix'][l]), 'g_norm_ffn': row(p['g_norm_ffn'][l]),
        'w_qkv': parts['qkv'].astype(BF16), 'w_rest': w_rest.astype(BF16), 'w_gates': parts['gates'].astype(BF16),
        'conv_qkv': p['conv_qkv'][l],
        'a_log128': pad8(p['a_log'][l]), 'dt_bias128': pad8(p['dt_bias'][l]),
        'g_delta_out': row(p['g_delta_out'][l]),
        'g_q_a': row(p['g_q_a'][l]), 'g_kv_a': row(p['g_kv_a'][l]),
        'w_q_pad': _pad_heads(p['w_q_b'][l], QK_NOPE + QK_ROPE, H_C).astype(BF16),
        'w_k_pad': _pad_heads(w_k, QK_NOPE, H_C).astype(BF16),
        'w_v_pad': _pad_heads(w_v, V_HEAD, H_C).astype(BF16),
        'w_br_a': p['w_branch'][l, 0].astype(BF16), 'w_br_b': p['w_branch'][l, 1].astype(BF16),
        'w_br_c_pad': w_br_c.astype(BF16),
        'w_out': p['w_out'][l].astype(BF16),
        'w_glu': p['w_glu'][l].astype(BF16), 'b_glu': row(p['b_glu'][l]), 's5_d': row(p['s5_d'][l]),
        'w_ffn_up': p['w_ffn_up'][l].astype(BF16), 'conv_ffn': p['conv_ffn'][l],
        'b_conv_ffn': row(p['b_conv_ffn'][l]), 'w_ffn_down': p['w_ffn_down'][l].astype(BF16),
    }


def _rope_tables(length):
    rows = length // GRID_W
    row = jnp.repeat(jnp.arange(rows, dtype=F32), GRID_W)
    col = (jnp.arange(length) % GRID_W).astype(F32)
    n_freq = QK_ROPE // 4
    inv_freq = 1.0 / (ROPE_BASE ** (jnp.arange(n_freq, dtype=F32) / n_freq))
    ang = jnp.concatenate([row[:, None] * inv_freq, col[:, None] * inv_freq], axis=-1)
    cos, sin = jnp.cos(ang), jnp.sin(ang)
    ones = jnp.ones((length, QK_NOPE), F32)
    zeros = jnp.zeros((length, QK_NOPE), F32)
    tail = jnp.zeros((length, HEAD_PAD - QK_NOPE - QK_ROPE), F32)
    cos_t = jnp.concatenate([ones, cos, cos, tail], axis=1)
    sin_t = jnp.concatenate([zeros, -sin, sin, tail], axis=1)
    return cos_t, sin_t


def _to_s5_layout(u, bsz, seq_len):
    nchunk = seq_len // S5_T
    t = u.astype(BF16).reshape(bsz, nchunk, S5_T, S5_GROUPS, S5_GROUP)
    return t.transpose(3, 1, 0, 2, 4).reshape(S5_GROUPS, nchunk * bsz, S5_ROWS)


def _from_s5_layout(y, bsz, seq_len):
    nchunk = seq_len // S5_T
    t = y.reshape(S5_GROUPS, nchunk, bsz, S5_T, S5_GROUP)
    return t.transpose(2, 1, 3, 0, 4).reshape(bsz * seq_len, B_WIDTH)


def _layer(x, bsz, seq_len, mods_rows, row_of_tile, lp, s5mats, ctx, rope_tabs, tl, g_final):
    is_ctx = ctx is None
    outs = _inproj(x, mods_rows, row_of_tile, lp, seq_len, tl, rope_tabs, emit_ctx=is_ctx)
    if is_ctx:
        qkv, zs, u, bg, ckv, kr, qh, kh, vh = outs
    else:
        qkv, zs, u, bg, qh, kh, vh = outs

    nc = seq_len // CHUNK
    bgt = bg[:, :16].reshape(bsz, nc, CHUNK, 16).transpose(0, 1, 3, 2)
    d_out = _delta(qkv, bg, bgt, zs, lp['g_delta_out'], None if is_ctx else ctx['s0'], bsz, seq_len, emit_state=is_ctx)
    oa = d_out[0]

    s_out = _s5(_to_s5_layout(u, bsz, seq_len), s5mats, None if is_ctx else ctx['h0'], seq_len // S5_T, bsz,
                emit_state=is_ctx)
    ys = _from_s5_layout(s_out[0], bsz, seq_len)

    segs = [(kh, vh, seq_len)]
    if not is_ctx:
        segs = [(ctx['kh'], ctx['vh'], ctx['past'])] + segs
    oc = _attention(qh, segs, bsz, seq_len, min(seq_len, 512))

    x = _merge(x, mods_rows, row_of_tile, lp, oa, ys, u, oc, tl)
    f_out = _ffn(x, mods_rows, row_of_tile, lp, seq_len, tl, g_final)
    extras = None
    if is_ctx:
        hfin = s_out[1]
        hfin = hfin.transpose(2, 1, 0, 3)
        extras = (d_out[1], hfin[..., :S5_STATE], hfin[..., S5_STATE:], ckv.reshape(bsz, seq_len, KV_LORA),
                  kr.reshape(bsz, seq_len, QK_ROPE))
    return f_out, extras


def kernel(x_prompt, x_sample, state_delta, state_s5_re, state_s5_im, cache_ckv, cache_krope, c, c_ctx, w_mod, b_mod, g_norm_mix, g_norm_ffn, w_in, conv_qkv, a_log, dt_bias, g_delta_out, s5_lam_re, s5_lam_im, s5_log_dt, s5_b_re, s5_b_im, s5_c_re, s5_c_im, s5_d, w_glu, b_glu, g_q_a, w_q_b, g_kv_a, w_kv_b, w_branch, w_out, w_ffn_up, conv_ffn, b_conv_ffn, w_ffn_down, g_final):
    p = dict(g_norm_mix=g_norm_mix, g_norm_ffn=g_norm_ffn, w_in=w_in, conv_qkv=conv_qkv, a_log=a_log,
             dt_bias=dt_bias, g_delta_out=g_delta_out, s5_d=s5_d, w_glu=w_glu, b_glu=b_glu, g_q_a=g_q_a,
             w_q_b=w_q_b, g_kv_a=g_kv_a, w_kv_b=w_kv_b, w_branch=w_branch, w_out=w_out, w_ffn_up=w_ffn_up,
             conv_ffn=conv_ffn, b_conv_ffn=b_conv_ffn, w_ffn_down=w_ffn_down)
    bp, lp_len, _ = x_prompt.shape
    bs, ls_len, _ = x_sample.shape
    past = cache_ckv.shape[2]
    depth = w_in.shape[0]

    mod_rows = 16
    cvec = jnp.concatenate([c_ctx[None, :], c, jnp.zeros((mod_rows - 1 - bs, D_MODEL), F32)], axis=0).astype(F32)
    mods = _modulation(cvec, w_mod, b_mod)

    g = S5_GROUPS
    b_re_t = s5_b_re.reshape(depth * g, S5_STATE, S5_GROUP).transpose(0, 2, 1)
    b_im_t = s5_b_im.reshape(depth * g, S5_STATE, S5_GROUP).transpose(0, 2, 1)
    s5f = _s5prep(s5_lam_re, s5_lam_im, s5_log_dt, b_re_t, b_im_t, s5_c_re, s5_c_im, 0)
    s5b = _s5prep(s5_lam_re, s5_lam_im, s5_log_dt, b_re_t, b_im_t, s5_c_re, s5_c_im, 1)

    rope_tabs = _rope_tables(ls_len)
    tl_p = min(lp_len, 512)
    tl_s = min(ls_len, 512)
    tiles_per_seq_s = ls_len // tl_s

    xp = x_prompt.astype(F32).reshape(bp * lp_len, D_MODEL)
    xs = x_sample.astype(F32).reshape(bs * ls_len, D_MODEL)
    deltas, s5_res, s5_ims, ckvs, kropes = [], [], [], [], []
    yp = ys = None
    for l in range(depth):
        lp = _layer_params(l, p)
        mods_rows = mods[l].reshape(mod_rows * 6, 1, D_MODEL)
        sl = slice(l * g, (l + 1) * g)
        s5mats = (s5f[0][sl], s5b[0][sl], s5f[1][sl], s5b[1][sl], s5f[2][sl], s5b[2][sl], s5f[3][sl], s5b[3][sl])
        last = l == depth - 1
        gfin = g_final.reshape(1, D_MODEL) if last else None

        out_p, extras = _layer(xp, bp, lp_len, mods_rows, lambda i: 0, lp, s5mats, None, None, tl_p, gfin)
        deltas.append(extras[0])
        s5_res.append(extras[1])
        s5_ims.append(extras[2])
        ckvs.append(extras[3])
        kropes.append(extras[4])

        kh_c, vh_c = _kvcache(cache_ckv[:, l].astype(F32).reshape(bs * past, KV_LORA),
                              cache_krope[:, l].astype(F32).reshape(bs * past, QK_ROPE), lp, past)
        h0 = jnp.concatenate([state_s5_re[:, l], state_s5_im[:, l]], axis=-1).astype(F32)
        ctx = dict(s0=state_delta[:, l].astype(F32), h0=h0.transpose(2, 1, 0, 3), kh=kh_c, vh=vh_c, past=past)
        out_s, _ = _layer(xs, bs, ls_len, mods_rows, lambda i: 1 + i // tiles_per_seq_s, lp, s5mats, ctx, rope_tabs,
                          tl_s, gfin)
        xp, xs = out_p[0], out_s[0]
        if last:
            yp, ys = out_p[1], out_s[1]

    y_prompt = yp.reshape(bp, lp_len, D_MODEL)
    y_sample = ys.reshape(bs, ls_len, D_MODEL)
    return (y_prompt, y_sample, jnp.stack(deltas, axis=1), jnp.stack(s5_res, axis=1), jnp.stack(s5_ims, axis=1),
            jnp.stack(ckvs, axis=1), jnp.stack(kropes, axis=1))
```

```python
import functools
import math

import jax
import jax.numpy as jnp
from jax import lax
from jax.experimental import pallas as pl
from jax.experimental.pallas import tpu as pltpu

F32 = jnp.float32
BF16 = jnp.bfloat16

D_MODEL = 1024
DEPTH = 4
GRID_W = 64
H_A = 4
HEAD_DIM_A = 128
A_WIDTH = H_A * HEAD_DIM_A
SHORT_CONV = 5
CHUNK = 64
S5_GROUP = 16
S5_STATE = 64
B_WIDTH = 512
S5_GROUPS = B_WIDTH // S5_GROUP
S5_T = 16
H_C = 8
QK_NOPE = 64
QK_ROPE = 32
V_HEAD = 64
Q_LORA = 384
KV_LORA = 256
ROPE_BASE = 10000.0
N_BRANCH = 3
BRANCH_WIDTH = 512
D_FF = 2816
FFN_CONV = 3
NORM_EPS = 1e-6
HEAD_PAD = 128
SUB = 16

VMEM_LIMIT = 56 * 1024 * 1024
HALO = 16


def _cp(sem):
    return pltpu.CompilerParams(dimension_semantics=sem, vmem_limit_bytes=VMEM_LIMIT)


def _mm(a, b):
    return jnp.dot(a.astype(BF16), b.astype(BF16), preferred_element_type=F32)


def _mm_nt(a, b):
    return lax.dot_general(a.astype(BF16), b.astype(BF16), (((1,), (1,)), ((), ())),
                           preferred_element_type=F32)


def _mm_tn(a, b):
    return lax.dot_general(a.astype(BF16), b.astype(BF16), (((0,), (0,)), ((), ())),
                           preferred_element_type=F32)


def _split3(x):
    x1 = x.astype(BF16)
    r = x - x1.astype(F32)
    x2 = r.astype(BF16)
    x3 = (r - x2.astype(F32)).astype(BF16)
    return x1, x2, x3


def _mm_exact_rhs(a, b_bf16):
    a1, a2, a3 = _split3(a)
    d = lambda t: jnp.dot(t, b_bf16, preferred_element_type=F32)
    return d(a1) + d(a2) + d(a3)


def _mm_exact_lhs(a_bf16, b):
    b1, b2, b3 = _split3(b)
    d = lambda t: jnp.dot(a_bf16, t, preferred_element_type=F32)
    return d(b1) + d(b2) + d(b3)


def _mm3_nt(a, b):
    a1 = a.astype(BF16)
    a2 = (a - a1.astype(F32)).astype(BF16)
    b1 = b.astype(BF16)
    b2 = (b - b1.astype(F32)).astype(BF16)
    d = lambda s, t: lax.dot_general(s, t, (((1,), (1,)), ((), ())), preferred_element_type=F32)
    return d(a1, b1) + d(a1, b2) + d(a2, b1)


def _silu(x):
    return x * jax.nn.sigmoid(x)


def _rms(x):
    return x * lax.rsqrt(jnp.mean(x * x, axis=-1, keepdims=True) + NORM_EPS)


def _mod_kernel(c_ref, w_ref, b_ref, o_ref):
    s = _silu(c_ref[...])
    o_ref[0] = _mm(s, w_ref[0]) + b_ref[0]


def _modulation(cvec, w_mod, b_mod):
    rows = cvec.shape[0]
    nblk = w_mod.shape[-1] // D_MODEL
    return pl.pallas_call(
        _mod_kernel,
        grid=(DEPTH, nblk),
        in_specs=[
            pl.BlockSpec((rows, D_MODEL), lambda l, j: (0, 0)),
            pl.BlockSpec((1, D_MODEL, D_MODEL), lambda l, j: (l, 0, j)),
            pl.BlockSpec((1, 1, D_MODEL), lambda l, j: (l, 0, j)),
        ],
        out_specs=pl.BlockSpec((1, rows, D_MODEL), lambda l, j: (l, 0, j)),
        out_shape=jax.ShapeDtypeStruct((DEPTH, rows, w_mod.shape[-1]), F32),
        compiler_params=_cp(("parallel", "parallel")),
        name="modulation",
    )(cvec, w_mod, b_mod.reshape(DEPTH, 1, -1))


REST_W = 512 + 512 + Q_LORA + KV_LORA + 128
OFF_Z, OFF_U, OFF_QA, OFF_KVA, OFF_SM = 0, 512, 1024, 1024 + Q_LORA, 1024 + Q_LORA + KV_LORA
SM_KR = 16
MLA_W = H_C * HEAD_PAD


def _rope_apply(x, cos, sin):
    lane = lax.broadcasted_iota(jnp.int32, x.shape, 1)
    partner = jnp.where(lane < QK_NOPE + QK_ROPE // 2,
                        pltpu.roll(x, HEAD_PAD - QK_ROPE // 2, 1),
                        pltpu.roll(x, QK_ROPE // 2, 1))
    return x * cos + partner * sin


def _inproj_kernel(*refs, tl, tiles_per_seq, rope, emit_ctx):
    it = iter(refs)
    xp_ref, xc_ref, xn_ref = next(it), next(it), next(it)
    shift_ref, scale_ref, gnorm_ref = next(it), next(it), next(it)
    wqkv_ref, convw_ref, wrest_ref = next(it), next(it), next(it)
    gqa_ref, gkva_ref, alog_ref, dtb_ref = next(it), next(it), next(it), next(it)
    wq_ref, wk_ref, wv_ref = next(it), next(it), next(it)
    cos_ref = sin_ref = None
    if rope:
        cos_ref, sin_ref = next(it), next(it)
    qkv_out, zs_out, u_out, bg_out = next(it), next(it), next(it), next(it)
    ckv_out = kr_out = None
    if emit_ctx:
        ckv_out, kr_out = next(it), next(it)
    qh_out, kh_out, vh_out = next(it), next(it), next(it)
    qkv_scr = next(it)

    i = pl.program_id(0)
    pos = i % tiles_per_seq
    mod_scale = 1.0 + scale_ref[0]
    mod_shift = shift_ref[0]
    gain = gnorm_ref[...]

    def norm_mod(x):
        return (_rms(x) * gain) * mod_scale + mod_shift

    h_cur = norm_mod(xc_ref[...]).astype(BF16)
    h_prev = norm_mod(xp_ref[...]).astype(BF16)
    h_next = norm_mod(xn_ref[...]).astype(BF16)

    w_qkv = wqkv_ref[...]
    prev_ok = (pos > 0).astype(F32)
    next_ok = (pos < tiles_per_seq - 1).astype(F32)
    qkv_scr[0:HALO, :] = _mm(h_prev, w_qkv) * prev_ok
    qkv_scr[HALO:HALO + tl, :] = _mm(h_cur, w_qkv)
    qkv_scr[HALO + tl:HALO + tl + HALO, :] = _mm(h_next, w_qkv) * next_ok
    pad = SHORT_CONV // 2
    for blk in range(3):
        cols = slice(blk * A_WIDTH, (blk + 1) * A_WIDTH)
        acc = None
        for t in range(SHORT_CONV):
            term = qkv_scr[pl.ds(HALO - pad + t, tl), cols] * convw_ref[t:t + 1, cols]
            acc = term if acc is None else acc + term
        acc = _silu(acc)
        if blk < 2:
            parts = []
            for hh in range(H_A):
                a = acc[:, hh * HEAD_DIM_A:(hh + 1) * HEAD_DIM_A]
                parts.append(a * lax.rsqrt(jnp.sum(a * a, axis=-1, keepdims=True) + NORM_EPS))
            acc = jnp.concatenate(parts, axis=1)
        qkv_out[:, cols] = acc.astype(BF16)

    rest = _mm(h_cur, wrest_ref[...])
    zs_out[...] = _silu(rest[:, OFF_Z:OFF_Z + 512])
    u_out[...] = rest[:, OFF_U:OFF_U + 512]

    small = rest[:, OFF_SM:OFF_SM + 128]
    lane = lax.broadcasted_iota(jnp.int32, small.shape, 1)
    beta = jax.nn.sigmoid(small)
    glog = -jnp.exp(alog_ref[...]) * jax.nn.softplus(small + dtb_ref[...])
    bg_out[...] = jnp.where(lane < 2 * H_A, beta, glog)

    qa = _rms(rest[:, OFF_QA:OFF_QA + Q_LORA]) * gqa_ref[...]
    ckv = _rms(rest[:, OFF_KVA:OFF_KVA + KV_LORA]) * gkva_ref[...]
    if emit_ctx:
        ckv_out[...] = ckv
        kr_out[...] = small[:, SM_KR:SM_KR + QK_ROPE]
    ckv_b = ckv.astype(BF16)
    qhat = _mm(qa, wq_ref[...])
    khat = _mm(ckv_b, wk_ref[...])
    vh_out[...] = _mm(ckv_b, wv_ref[...]).astype(BF16)
    kr_al = jnp.where((lane >= QK_NOPE) & (lane < QK_NOPE + QK_ROPE),
                      pltpu.roll(small, QK_NOPE - SM_KR, 1), 0.0)
    if rope:
        cos, sin = cos_ref[...], sin_ref[...]
        kr_al = _rope_apply(kr_al, cos, sin)
    qscale = (QK_NOPE + QK_ROPE) ** -0.5
    for hh in range(H_C):
        cs = slice(hh * HEAD_PAD, (hh + 1) * HEAD_PAD)
        qh = qhat[:, cs]
        if rope:
            qh = _rope_apply(qh, cos, sin)
        qh_out[:, cs] = (qh * qscale).astype(BF16)
        kh_out[:, cs] = (khat[:, cs] + kr_al).astype(BF16)


def _inproj(x, mods_rows, row_of_tile, lp, seq_len, tl, rope_tabs, emit_ctx):
    n = x.shape[0]
    nt = n // tl
    tps = seq_len // tl
    nh8 = n // HALO
    rope = rope_tabs is not None

    def xprev(i):
        return (jnp.maximum(i * (tl // HALO) - 1, 0), 0)

    def xnext(i):
        return (jnp.minimum((i + 1) * (tl // HALO), nh8 - 1), 0)

    const2 = lambda i: (0, 0)
    in_specs = [
        pl.BlockSpec((HALO, D_MODEL), xprev),
        pl.BlockSpec((tl, D_MODEL), lambda i: (i, 0)),
        pl.BlockSpec((HALO, D_MODEL), xnext),
        pl.BlockSpec((1, 1, D_MODEL), lambda i: (row_of_tile(i) * 6 + 0, 0, 0)),
        pl.BlockSpec((1, 1, D_MODEL), lambda i: (row_of_tile(i) * 6 + 1, 0, 0)),
        pl.BlockSpec((1, D_MODEL), const2),
        pl.BlockSpec((D_MODEL, 3 * A_WIDTH), const2),
        pl.BlockSpec((SHORT_CONV, 3 * A_WIDTH), const2),
        pl.BlockSpec((D_MODEL, REST_W), const2),
        pl.BlockSpec((1, Q_LORA), const2),
        pl.BlockSpec((1, KV_LORA), const2),
        pl.BlockSpec((1, 128), const2),
        pl.BlockSpec((1, 128), const2),
        pl.BlockSpec((Q_LORA, MLA_W), const2),
        pl.BlockSpec((KV_LORA, MLA_W), const2),
        pl.BlockSpec((KV_LORA, MLA_W), const2),
    ]
    args = [x, x, x, mods_rows, mods_rows, lp['g_norm_mix'], lp['w_qkv'], lp['conv_qkv'], lp['w_rest'],
            lp['g_q_a'], lp['g_kv_a'], lp['a_log128'], lp['dt_bias128'], lp['w_q_pad'], lp['w_k_pad'], lp['w_v_pad']]
    if rope:
        in_specs += [pl.BlockSpec((tl, HEAD_PAD), lambda i: (i % tps, 0))] * 2
        args += list(rope_tabs)
    tok = lambda w: pl.BlockSpec((tl, w), lambda i: (i, 0))
    out_specs = [tok(3 * A_WIDTH), tok(512), tok(512), tok(128)]
    out_shape = [jax.ShapeDtypeStruct((n, 3 * A_WIDTH), BF16), jax.ShapeDtypeStruct((n, 512), F32),
                 jax.ShapeDtypeStruct((n, 512), F32), jax.ShapeDtypeStruct((n, 128), F32)]
    if emit_ctx:
        out_specs += [tok(KV_LORA), tok(QK_ROPE)]
        out_shape += [jax.ShapeDtypeStruct((n, KV_LORA), F32), jax.ShapeDtypeStruct((n, QK_ROPE), F32)]
    out_specs += [tok(MLA_W)] * 3
    out_shape += [jax.ShapeDtypeStruct((n, MLA_W), BF16)] * 3
    return pl.pallas_call(
        functools.partial(_inproj_kernel, tl=tl, tiles_per_seq=tps, rope=rope, emit_ctx=emit_ctx),
        grid=(nt,),
        in_specs=in_specs,
        out_specs=out_specs,
        out_shape=out_shape,
        scratch_shapes=[pltpu.VMEM((tl + 2 * HALO, 3 * A_WIDTH), F32)],
        compiler_params=_cp(("parallel",)),
        name="inproj",
    )(*args)


def _kvcache_kernel(ckv_ref, kr_ref, wk_ref, wv_ref, kh_out, vh_out):
    ckv_b = ckv_ref[...].astype(BF16)
    khat = _mm(ckv_b, wk_ref[...])
    vh_out[...] = _mm(ckv_b, wv_ref[...]).astype(BF16)
    kr_al = kr_ref[...]
    for hh in range(H_C):
        cs = slice(hh * HEAD_PAD, (hh + 1) * HEAD_PAD)
        kh_out[:, cs] = (khat[:, cs] + kr_al).astype(BF16)


def _kvcache(ckv, kr, lp, tl):
    n = ckv.shape[0]
    const2 = lambda i: (0, 0)
    kr = jnp.pad(kr, ((0, 0), (QK_NOPE, HEAD_PAD - QK_NOPE - QK_ROPE)))
    return pl.pallas_call(
        _kvcache_kernel,
        grid=(n // tl,),
        in_specs=[pl.BlockSpec((tl, KV_LORA), lambda i: (i, 0)), pl.BlockSpec((tl, HEAD_PAD), lambda i: (i, 0)),
                  pl.BlockSpec((KV_LORA, MLA_W), const2), pl.BlockSpec((KV_LORA, MLA_W), const2)],
        out_specs=[pl.BlockSpec((tl, MLA_W), lambda i: (i, 0))] * 2,
        out_shape=[jax.ShapeDtypeStruct((n, MLA_W), BF16)] * 2,
        compiler_params=_cp(("parallel",)),
        name="kvcache",
    )(ckv, kr, lp['w_k_pad'], lp['w_v_pad'])


def _attn_kernel(*refs, nseg):
    q_ref = refs[0]
    k_refs = refs[1:1 + nseg]
    v_refs = refs[1 + nseg:1 + 2 * nseg]
    o_ref = refs[1 + 2 * nseg]
    q = q_ref[...]
    s = [lax.dot_general(q, k[...], (((1,), (1,)), ((), ())), preferred_element_type=F32) for k in k_refs]
    m = s[0].max(axis=-1, keepdims=True)
    for t in s[1:]:
        m = jnp.maximum(m, t.max(axis=-1, keepdims=True))
    p = [jnp.exp(t - m) for t in s]
    den = p[0].sum(axis=-1, keepdims=True)
    for t in p[1:]:
        den = den + t.sum(axis=-1, keepdims=True)
    acc = None
    for t, v in zip(p, v_refs):
        part = jnp.dot(t.astype(BF16), v[...], preferred_element_type=F32)
        acc = part if acc is None else acc + part
    o_ref[...] = (acc / den).astype(BF16)


def _attention(qh, segs, bsz, seq_len, tq):
    nq = seq_len // tq
    in_specs = [pl.BlockSpec((tq, HEAD_PAD), lambda b, h, i: (b * nq + i, h))]
    args = [qh]
    for which in (0, 1):
        for seg in segs:
            in_specs.append(pl.BlockSpec((seg[2], HEAD_PAD), lambda b, h, i: (b, h)))
            args.append(seg[which])
    return pl.pallas_call(
        functools.partial(_attn_kernel, nseg=len(segs)),
        grid=(bsz, H_C, nq),
        in_specs=in_specs,
        out_specs=pl.BlockSpec((tq, HEAD_PAD), lambda b, h, i: (b * nq + i, h)),
        out_shape=jax.ShapeDtypeStruct((bsz * seq_len, MLA_W), BF16),
        compiler_params=_cp(("parallel", "parallel", "parallel")),
        name="attention",
    )(*args)


def _delta_kernel(*refs, seq_len, has_s0, emit_state):
    it = iter(refs)
    qkv_ref, bg_ref, bgt_ref, zs_ref, gout_ref = next(it), next(it), next(it), next(it), next(it)
    s0_ref = next(it) if has_s0 else None
    o_ref = next(it)
    sfin_ref = next(it) if emit_state else None
    s_scr, of_scr, ob_scr = next(it), next(it), next(it)
    o_scrs = (of_scr, ob_scr)

    nc = seq_len // CHUNK
    c = CHUNK
    if has_s0:
        s_scr[...] = s0_ref[0]
    else:
        s_scr[...] = jnp.zeros(s_scr.shape, F32)

    ri = lax.broadcasted_iota(jnp.int32, (c, c), 0)
    ci = lax.broadcasted_iota(jnp.int32, (c, c), 1)
    tril = (ri >= ci).astype(BF16)
    triu = (ri <= ci).astype(BF16)
    same_blk = (ri // SUB) == (ci // SUB)
    scale = HEAD_DIM_A ** -0.5

    def one_direction(d, rows, qkv_blk, bgc, bgr, s_olds):
        results = []
        if d == 0:
            incl, strict = ri >= ci, ri > ci
            gc_col = _mm_exact_lhs(tril, bgc)
            gc_row = _mm_exact_rhs(bgr, triu)
            last = c - 1
        else:
            incl, strict = ri <= ci, ri < ci
            gc_col = _mm_exact_lhs(triu, bgc)
            gc_row = _mm_exact_rhs(bgr, tril)
            last = 0
        for hh in range(H_A):
            col = d * H_A + hh
            beta = bgc[:, col:col + 1]
            gcc = gc_col[:, 2 * H_A + col:2 * H_A + col + 1]
            gcr = gc_row[2 * H_A + col:2 * H_A + col + 1, :]
            glast = gcc[last:last + 1, :]
            q = qkv_blk[:, hh * HEAD_DIM_A:(hh + 1) * HEAD_DIM_A]
            k = qkv_blk[:, A_WIDTH + hh * HEAD_DIM_A:A_WIDTH + (hh + 1) * HEAD_DIM_A]
            v = qkv_blk[:, 2 * A_WIDTH + hh * HEAD_DIM_A:2 * A_WIDTH + (hh + 1) * HEAD_DIM_A]
            kf = k.astype(F32)
            kb = kf * beta
            decay = jnp.where(incl, jnp.exp(jnp.where(incl, gcc - gcr, 0.0)), 0.0)
            a = _mm_nt(kb, k) * decay
            qk = _mm_nt(q, k) * (decay * scale)
            dg = jnp.where(same_blk & strict, a, 0.0)
            lo = jnp.where(jnp.logical_not(same_blk) & strict, a, 0.0)
            m1 = _mm(dg, dg)
            p = m1 - dg - _mm(dg, m1)
            pw = m1
            for _ in range(int(math.log2(SUB)) - 2):
                pw = _mm(pw, pw)
                p = p + pw + _mm(p, pw)
            nmat = lo + _mm(p, lo)
            n2 = _mm(nmat, nmat)
            e_col = jnp.exp(gcc)
            rhs = jnp.concatenate([v.astype(F32) * beta, kb * e_col], axis=1)
            y = rhs + _mm(p, rhs)
            y = y + _mm(n2, y)
            x = y - _mm(nmat, y)
            u, w = x[:, :HEAD_DIM_A], x[:, HEAD_DIM_A:]
            s_old = s_olds[hh]
            v_new = u - _mm(w, s_old)
            o = _mm(q.astype(F32) * (e_col * scale), s_old) + _mm(qk, v_new)
            k_dec = kf * jnp.exp(glast - gcc)
            s_new = s_old * jnp.exp(glast) + _mm_tn(k_dec, v_new)
            results.append((o, s_new))
        return results

    def body(i, carry):
        chs = (i, nc - 1 - i)
        rows = [pl.ds(pl.multiple_of(ch * c, c), c) for ch in chs]
        loaded = []
        for d in range(2):
            loaded.append((qkv_ref[rows[d], :], bg_ref[rows[d], :], bgt_ref[0, chs[d]],
                           [s_scr[d, hh] for hh in range(H_A)]))
        results = [one_direction(d, rows[d], *loaded[d]) for d in range(2)]
        for d in range(2):
            for hh in range(H_A):
                o, s_new = results[d][hh]
                s_scr[d, hh] = s_new
                o_scrs[d][rows[d], hh * HEAD_DIM_A:(hh + 1) * HEAD_DIM_A] = o
        return carry

    lax.fori_loop(0, nc, body, 0)

    gout = gout_ref[...]
    blk = 256
    for r in range(seq_len // blk):
        rows = slice(r * blk, (r + 1) * blk)
        for hh in range(H_A):
            cs = slice(hh * HEAD_DIM_A, (hh + 1) * HEAD_DIM_A)
            o = of_scr[rows, cs] + ob_scr[rows, cs]
            o_ref[rows, cs] = (_rms(o) * gout * zs_ref[rows, cs]).astype(BF16)
    if emit_state:
        sfin_ref[0] = s_scr[...]


def _delta(qkv, bg, bgt, zs, g_out, s0, bsz, seq_len, emit_state):
    nc = seq_len // CHUNK
    has_s0 = s0 is not None
    in_specs = [
        pl.BlockSpec((seq_len, 3 * A_WIDTH), lambda b: (b, 0)),
        pl.BlockSpec((seq_len, 128), lambda b: (b, 0)),
        pl.BlockSpec((1, nc, 16, CHUNK), lambda b: (b, 0, 0, 0)),
        pl.BlockSpec((seq_len, A_WIDTH), lambda b: (b, 0)),
        pl.BlockSpec((1, HEAD_DIM_A), lambda b: (0, 0)),
    ]
    args = [qkv, bg, bgt, zs, g_out]
    st_block = pl.BlockSpec((1, 2, H_A, HEAD_DIM_A, HEAD_DIM_A), lambda b: (b, 0, 0, 0, 0))
    if has_s0:
        in_specs.append(st_block)
        args.append(s0)
    out_specs = [pl.BlockSpec((seq_len, A_WIDTH), lambda b: (b, 0))]
    out_shape = [jax.ShapeDtypeStruct((bsz * seq_len, A_WIDTH), BF16)]
    if emit_state:
        out_specs.append(st_block)
        out_shape.append(jax.ShapeDtypeStruct((bsz, 2, H_A, HEAD_DIM_A, HEAD_DIM_A), F32))
    return pl.pallas_call(
        functools.partial(_delta_kernel, seq_len=seq_len, has_s0=has_s0, emit_state=emit_state),
        grid=(bsz,),
        in_specs=in_specs,
        out_specs=out_specs,
        out_shape=out_shape,
        scratch_shapes=[pltpu.VMEM((2, H_A, HEAD_DIM_A, HEAD_DIM_A), F32), pltpu.VMEM((seq_len, A_WIDTH), F32),
                        pltpu.VMEM((seq_len, A_WIDTH), F32)],
        compiler_params=_cp(("parallel",)),
        name="delta",
    )(*args)


DELTA_TB = 256
DELTA_G = DELTA_TB // CHUNK


def _bmm(a, b):
    return lax.dot_general(a.astype(BF16), b.astype(BF16), (((2,), (1,)), ((0,), (0,))),
                           preferred_element_type=F32)


def _bmm_nt(a, b):
    return lax.dot_general(a.astype(BF16), b.astype(BF16), (((2,), (2,)), ((0,), (0,))),
                           preferred_element_type=F32)


def _bmm_tn(a, b):
    return lax.dot_general(a.astype(BF16), b.astype(BF16), (((1,), (1,)), ((0,), (0,))),
                           preferred_element_type=F32)


def _delta_prep_kernel(qkv_ref, bg_ref, bgt_ref, uf_ref, wf_ref, qef_ref, kef_ref, qkf_ref,
                       ub_ref, wb_ref, qeb_ref, keb_ref, qkb_ref, dec_ref):
    c = CHUNK
    outs = ((uf_ref, wf_ref, qef_ref, kef_ref, qkf_ref), (ub_ref, wb_ref, qeb_ref, keb_ref, qkb_ref))
    ri = lax.broadcasted_iota(jnp.int32, (c, c), 0)
    ci = lax.broadcasted_iota(jnp.int32, (c, c), 1)
    tril = (ri >= ci).astype(BF16)
    triu = (ri <= ci).astype(BF16)
    same_blk = (ri // SUB) == (ci // SUB)
    scale = HEAD_DIM_A ** -0.5

    insts = [(g, d, hh) for g in range(DELTA_G) for d in range(2) for hh in range(H_A)]
    n = len(insts)
    qkv = qkv_ref[...]
    bg = bg_ref[...]

    def head_stack(base):
        return jnp.stack([qkv[g * c:(g + 1) * c, base + hh * HEAD_DIM_A:base + (hh + 1) * HEAD_DIM_A]
                          for g, d, hh in insts])

    q, k, v = head_stack(0), head_stack(A_WIDTH), head_stack(2 * A_WIDTH)

    cols, rws = {}, {}
    for g in range(DELTA_G):
        bgc = bg[g * c:(g + 1) * c, :]
        bgr = bgt_ref[0, g]
        cols[g] = (bgc, _mm_exact_lhs(tril, bgc), _mm_exact_lhs(triu, bgc))
        rws[g] = (_mm_exact_rhs(bgr, triu), _mm_exact_rhs(bgr, tril))
    beta = jnp.stack([cols[g][0][:, d * H_A + hh:d * H_A + hh + 1] for g, d, hh in insts])
    gcc = jnp.stack([cols[g][1 + d][:, 2 * H_A + d * H_A + hh:2 * H_A + d * H_A + hh + 1]
                     for g, d, hh in insts])
    gcr = jnp.stack([rws[g][d][2 * H_A + d * H_A + hh:2 * H_A + d * H_A + hh + 1, :] for g, d, hh in insts])
    glast = jnp.stack([gcc[i, (c - 1 if insts[i][1] == 0 else 0):(c if insts[i][1] == 0 else 1), :]
                       for i in range(n)])

    sign = jnp.stack([jnp.full((1, 1), 1 - 2 * d, jnp.int32) for g, d, hh in insts])
    tri = (ri - ci)[None] * sign
    incl = tri >= 0
    strict = tri > 0
    blk = jnp.broadcast_to(same_blk[None], incl.shape)

    kf = k.astype(F32)
    kb = kf * beta
    decay = jnp.where(incl, jnp.exp(jnp.where(incl, gcc - gcr, 0.0)), 0.0)
    a = _bmm_nt(kb, k) * decay
    qk = _bmm_nt(q, k) * (decay * scale)
    dg = jnp.where(blk & strict, a, 0.0)
    lo = jnp.where(jnp.logical_not(blk) & strict, a, 0.0)
    m1 = _bmm(dg, dg)
    p = m1 - dg - _bmm(dg, m1)
    pw = m1
    for _ in range(int(math.log2(SUB)) - 2):
        pw = _bmm(pw, pw)
        p = p + pw + _bmm(p, pw)
    nm = lo + _bmm(p, lo)
    n2 = _bmm(nm, nm)
    e_col = jnp.exp(gcc)
    rhs = jnp.concatenate([v.astype(F32) * beta, kb * e_col], axis=2)
    y = rhs + _bmm(p, rhs)
    y = y + _bmm(n2, y)
    x = y - _bmm(nm, y)
    u, w = x[:, :, :HEAD_DIM_A], x[:, :, HEAD_DIM_A:]
    qe = q.astype(F32) * (e_col * scale)
    ke = kf * jnp.exp(glast - gcc)
    dec = jnp.exp(glast)
    zpad = jnp.zeros((c, HEAD_DIM_A - c), BF16)
    for i, (g, d, hh) in enumerate(insts):
        rows = slice(g * c, (g + 1) * c)
        cs = slice(hh * HEAD_DIM_A, (hh + 1) * HEAD_DIM_A)
        u_ref, w_ref, qe_ref, ke_ref, qk_ref = outs[d]
        u_ref[0, rows, cs] = u[i]
        w_ref[0, rows, cs] = w[i].astype(BF16)
        qe_ref[0, rows, cs] = qe[i].astype(BF16)
        ke_ref[0, rows, cs] = ke[i].astype(BF16)
        qk_ref[0, rows, cs] = jnp.concatenate([qk[i].astype(BF16), zpad], axis=1)
        dec_ref[0, g, d * H_A + hh:d * H_A + hh + 1, :] = jnp.broadcast_to(dec[i], (1, HEAD_DIM_A))


def _delta_prep(qkv, bg, bgt, bsz, seq_len):
    nblk = seq_len // DELTA_TB
    nc = seq_len // CHUNK
    tokf = jax.ShapeDtypeStruct((bsz, seq_len, A_WIDTH), F32)
    tokb = jax.ShapeDtypeStruct((bsz, seq_len, A_WIDTH), BF16)
    tspec = pl.BlockSpec((1, DELTA_TB, A_WIDTH), lambda b, j: (b, j, 0))
    return pl.pallas_call(
        _delta_prep_kernel,
        grid=(bsz, nblk),
        in_specs=[pl.BlockSpec((DELTA_TB, 3 * A_WIDTH), lambda b, j: (b * nblk + j, 0)),
                  pl.BlockSpec((DELTA_TB, 128), lambda b, j: (b * nblk + j, 0)),
                  pl.BlockSpec((1, DELTA_G, 16, CHUNK), lambda b, j: (b, j, 0, 0))],
        out_specs=[tspec] * 10 + [pl.BlockSpec((1, DELTA_G, 2 * H_A, HEAD_DIM_A), lambda b, j: (b, j, 0, 0))],
        out_shape=[tokf, tokb, tokb, tokb, tokb] * 2 + [jax.ShapeDtypeStruct((bsz, nc, 2 * H_A, HEAD_DIM_A), F32)],
        compiler_params=_cp(("parallel", "parallel")),
        name="delta_prep",
    )(qkv, bg, bgt)


def _delta_recur_kernel(*refs, nb, nblk, has_s0, emit_state):
    it = iter(refs)
    fwd = [next(it) for _ in range(5)]
    bwd = [next(it) for _ in range(5)]
    decf_ref, decb_ref = next(it), next(it)
    s0_ref = next(it) if has_s0 else None
    of_ref, ob_ref = next(it), next(it)
    sfin_ref = next(it) if emit_state else None
    s_scr = next(it)
    j = pl.program_id(1)
    c = CHUNK

    @pl.when(j == 0)
    def _():
        if has_s0:
            s_scr[...] = s0_ref[...]
        else:
            s_scr[...] = jnp.zeros(s_scr.shape, F32)

    insts = [(bi, d, hh) for bi in range(nb) for d in range(2) for hh in range(H_A)]
    for step in range(DELTA_G):
        gsel = (step, DELTA_G - 1 - step)

        def stk(idx, width=HEAD_DIM_A):
            return jnp.stack([(fwd, bwd)[d][idx][bi, gsel[d] * c:(gsel[d] + 1) * c,
                                                 hh * HEAD_DIM_A:hh * HEAD_DIM_A + width]
                              for bi, d, hh in insts])

        u, w, qe, ke, qk = stk(0), stk(1), stk(2), stk(3), stk(4, c)
        dec = jnp.stack([(decf_ref, decb_ref)[d][bi, gsel[d], d * H_A + hh:d * H_A + hh + 1, :]
                         for bi, d, hh in insts])
        s_old = jnp.stack([s_scr[bi, d, hh] for bi, d, hh in insts])
        r1 = _bmm(jnp.concatenate([w, qe], axis=1), s_old)
        v_new = (u - r1[:, :c, :]).astype(BF16)
        o = r1[:, c:, :] + _bmm(qk, v_new)
        s_new = s_old * dec + _bmm_tn(ke, v_new)
        for i, (bi, d, hh) in enumerate(insts):
            s_scr[bi, d, hh] = s_new[i]
            (of_ref, ob_ref)[d][bi, gsel[d] * c:(gsel[d] + 1) * c, hh * HEAD_DIM_A:(hh + 1) * HEAD_DIM_A] = o[i]

    if emit_state:
        @pl.when(j == nblk - 1)
        def _():
            sfin_ref[...] = s_scr[...]


def _delta_recur(prep, s0, bsz, seq_len, nb, emit_state):
    nblk = seq_len // DELTA_TB
    has_s0 = s0 is not None
    fspec = pl.BlockSpec((nb, DELTA_TB, A_WIDTH), lambda b, j: (b, j, 0))
    bspec = pl.BlockSpec((nb, DELTA_TB, A_WIDTH), lambda b, j: (b, nblk - 1 - j, 0))
    dspec_f = pl.BlockSpec((nb, DELTA_G, 2 * H_A, HEAD_DIM_A), lambda b, j: (b, j, 0, 0))
    dspec_b = pl.BlockSpec((nb, DELTA_G, 2 * H_A, HEAD_DIM_A), lambda b, j: (b, nblk - 1 - j, 0, 0))
    st_spec = pl.BlockSpec((nb, 2, H_A, HEAD_DIM_A, HEAD_DIM_A), lambda b, j: (b, 0, 0, 0, 0))
    in_specs = [fspec] * 5 + [bspec] * 5 + [dspec_f, dspec_b]
    args = list(prep[:10]) + [prep[10], prep[10]]
    if has_s0:
        in_specs.append(st_spec)
        args.append(s0)
    out_specs = [fspec, bspec]
    out_shape = [jax.ShapeDtypeStruct((bsz, seq_len, A_WIDTH), F32)] * 2
    if emit_state:
        out_specs.append(st_spec)
        out_shape.append(jax.ShapeDtypeStruct((bsz, 2, H_A, HEAD_DIM_A, HEAD_DIM_A), F32))
    return pl.pallas_call(
        functools.partial(_delta_recur_kernel, nb=nb, nblk=nblk, has_s0=has_s0, emit_state=emit_state),
        grid=(bsz // nb, nblk),
        in_specs=in_specs,
        out_specs=out_specs,
        out_shape=out_shape,
        scratch_shapes=[pltpu.VMEM((nb, 2, H_A, HEAD_DIM_A, HEAD_DIM_A), F32)],
        compiler_params=_cp(("parallel", "arbitrary")),
        name="delta_recur",
    )(*args)


S5_ROWS = S5_T * S5_GROUP


def _s5prep_kernel(lre_ref, lim_ref, ldt_ref, bre_ref, bim_ref, cre_ref, cim_ref,
                   kt_out, ere_out, eim_out, fre_out, fim_out, lt_out, *, backward):
    lre, lim = lre_ref[0], lim_ref[0]
    dt = jnp.exp(ldt_ref[0])
    zr, zi = lre * dt, lim * dt

    def lam_pow(e):
        mag = jnp.exp(e * zr)
        return mag * jnp.cos(e * zi), mag * jnp.sin(e * zi)

    one = jnp.ones((1, 1), F32)
    l1r, l1i = lam_pow(one)
    den = lre * lre + lim * lim
    nr, ni = l1r - 1.0, l1i
    cfr = (nr * lre + ni * lim) / den
    cfi = (ni * lre - nr * lim) / den
    bre, bim = bre_ref[0], bim_ref[0]
    bbr = cfr * bre - cfi * bim
    bbi = cfr * bim + cfi * bre
    cre, cim = cre_ref[0], cim_ref[0]

    j = (lax.broadcasted_iota(jnp.int32, (S5_ROWS, 1), 0) // S5_GROUP).astype(F32)
    tile = lambda m: jnp.concatenate([m] * S5_T, axis=0)
    c_r, c_i = tile(cre), tile(cim)
    b_r, b_i = tile(bbr), tile(bbi)

    pr, pi = lam_pow(j)
    qr = pr * c_r - pi * c_i
    qi = pr * c_i + pi * c_r
    kt_out[0] = _mm3_nt(qr, bbr) - _mm3_nt(qi, bbi)

    e_exp = j if backward else (S5_T - 1.0) - j
    er, ei = lam_pow(e_exp)
    ere_out[0] = er * b_r - ei * b_i
    eim_out[0] = er * b_i + ei * b_r

    f_exp = (S5_T - j) if backward else j + 1.0
    fr, fi = lam_pow(f_exp)
    fre_out[0] = fr * c_r - fi * c_i
    fim_out[0] = -(fr * c_i + fi * c_r)

    ltr, lti = lam_pow(one * float(S5_T))
    lt_out[0] = jnp.concatenate([ltr, lti], axis=0)


def _s5prep(lam_re, lam_im, log_dt, b_re_t, b_im_t, c_re, c_im, direction):
    g, p, cg = S5_GROUPS, S5_STATE, S5_GROUP
    n = DEPTH * g
    lam_idx = lambda i: ((i // g) * 2 * g + direction * g + i % g, 0, 0)
    par_idx = lambda i: (i, 0, 0)
    outs = pl.pallas_call(
        functools.partial(_s5prep_kernel, backward=bool(direction)),
        grid=(n,),
        in_specs=[pl.BlockSpec((1, 1, p), lam_idx), pl.BlockSpec((1, 1, p), lam_idx), pl.BlockSpec((1, 1, 1), lam_idx),
                  pl.BlockSpec((1, cg, p), par_idx), pl.BlockSpec((1, cg, p), par_idx),
                  pl.BlockSpec((1, cg, p), par_idx), pl.BlockSpec((1, cg, p), par_idx)],
        out_specs=[pl.BlockSpec((1, S5_ROWS, cg), par_idx)] + [pl.BlockSpec((1, S5_ROWS, p), par_idx)] * 4
                  + [pl.BlockSpec((1, 2, p), par_idx)],
        out_shape=[jax.ShapeDtypeStruct((n, S5_ROWS, cg), F32)] + [jax.ShapeDtypeStruct((n, S5_ROWS, p), F32)] * 4
                  + [jax.ShapeDtypeStruct((n, 2, p), F32)],
        compiler_params=_cp(("parallel",)),
        name="s5prep_bwd" if direction else "s5prep_fwd",
    )(lam_re.reshape(DEPTH * 2 * g, 1, p), lam_im.reshape(DEPTH * 2 * g, 1, p), log_dt.reshape(DEPTH * 2 * g, 1, 1),
      b_re_t, b_im_t, c_re.reshape(n, cg, p), c_im.reshape(n, cg, p))
    kt, ere, eim, fre, fim, lt = outs
    kt4 = kt.reshape(n, S5_T, cg, cg)
    tt = jnp.arange(S5_T)[:, None]
    ss = jnp.arange(S5_T)[None, :]
    lag = (ss - tt) if direction else (tt - ss)
    blocks = jnp.where((lag >= 0)[None, :, :, None, None], kt4[:, jnp.clip(lag, 0, S5_T - 1)], 0.0)
    mt = blocks.transpose(0, 1, 3, 2, 4).reshape(n, S5_ROWS, S5_ROWS).astype(BF16)
    emat = jnp.concatenate([ere, eim], axis=-1).astype(BF16)
    fmat = jnp.concatenate([fre, fim], axis=-1).astype(BF16)
    ltrow = lt.reshape(n, 1, 2 * p)
    return mt, emat, fmat, ltrow


def _s5_kernel(*refs, nchunk, bsz, has_h0, emit_state):
    it = iter(refs)
    u_ref = next(it)
    mtf_ref, mtb_ref, ef_ref, eb_ref, ff_ref, fb_ref, ltf_ref, ltb_ref = (next(it) for _ in range(8))
    h0_ref = next(it) if has_h0 else None
    y_ref = next(it)
    hfin_ref = next(it) if emit_state else None
    sf_scr, sb_scr, hf_scr, hb_scr = next(it), next(it), next(it), next(it)

    u = u_ref[0]
    sf_scr[...] = jnp.dot(u, ef_ref[0], preferred_element_type=F32)
    sb_scr[...] = jnp.dot(u, eb_ref[0], preferred_element_type=F32)
    p = S5_STATE
    lane = lax.broadcasted_iota(jnp.int32, (1, 2 * p), 1)

    def coeffs(lt_ref):
        lt = lt_ref[0]
        sw = pltpu.roll(lt, p, 1)
        a = jnp.where(lane < p, lt, sw)
        b = jnp.where(lane < p, -sw, lt)
        return a, b

    af, bf = coeffs(ltf_ref)
    ab, bb = coeffs(ltb_ref)
    if has_h0:
        h0f, h0b = h0_ref[0, 0], h0_ref[0, 1]
    else:
        h0f = h0b = jnp.zeros((bsz, 2 * p), F32)

    def body(k, carry):
        hf, hb = carry
        rf = pl.ds(pl.multiple_of(k * bsz, bsz), bsz)
        rb = pl.ds(pl.multiple_of((nchunk - 1 - k) * bsz, bsz), bsz)
        hf_scr[rf, :] = hf
        hb_scr[rb, :] = hb
        hf = af * hf + bf * pltpu.roll(hf, p, 1) + sf_scr[rf, :]
        hb = ab * hb + bb * pltpu.roll(hb, p, 1) + sb_scr[rb, :]
        return hf, hb

    hf, hb = lax.fori_loop(0, nchunk, body, (h0f, h0b))
    if emit_state:
        hfin_ref[0, 0] = hf
        hfin_ref[0, 1] = hb

    nt = lambda a, b: lax.dot_general(a, b, (((1,), (1,)), ((), ())), preferred_element_type=F32)
    y = nt(u, mtf_ref[0]) + nt(u, mtb_ref[0])
    y = y + nt(hf_scr[...].astype(BF16), ff_ref[0]) + nt(hb_scr[...].astype(BF16), fb_ref[0])
    y_ref[0] = y


def _s5(u_t, mats_l, h0, nchunk, bsz, emit_state):
    g, p = S5_GROUPS, S5_STATE
    r = nchunk * bsz
    has_h0 = h0 is not None
    gi = lambda i: (i, 0, 0)
    in_specs = [pl.BlockSpec((1, r, S5_ROWS), gi),
                pl.BlockSpec((1, S5_ROWS, S5_ROWS), gi), pl.BlockSpec((1, S5_ROWS, S5_ROWS), gi),
                pl.BlockSpec((1, S5_ROWS, 2 * p), gi), pl.BlockSpec((1, S5_ROWS, 2 * p), gi),
                pl.BlockSpec((1, S5_ROWS, 2 * p), gi), pl.BlockSpec((1, S5_ROWS, 2 * p), gi),
                pl.BlockSpec((1, 1, 2 * p), gi), pl.BlockSpec((1, 1, 2 * p), gi)]
    args = [u_t] + list(mats_l)
    st_spec = pl.BlockSpec((1, 2, bsz, 2 * p), lambda i: (i, 0, 0, 0))
    if has_h0:
        in_specs.append(st_spec)
        args.append(h0)
    out_specs = [pl.BlockSpec((1, r, S5_ROWS), gi)]
    out_shape = [jax.ShapeDtypeStruct((g, r, S5_ROWS), F32)]
    if emit_state:
        out_specs.append(st_spec)
        out_shape.append(jax.ShapeDtypeStruct((g, 2, bsz, 2 * p), F32))
    return pl.pallas_call(
        functools.partial(_s5_kernel, nchunk=nchunk, bsz=bsz, has_h0=has_h0, emit_state=emit_state),
        grid=(g,),
        in_specs=in_specs,
        out_specs=out_specs,
        out_shape=out_shape,
        scratch_shapes=[pltpu.VMEM((r, 2 * p), F32)] * 4,
        compiler_params=_cp(("parallel",)),
        name="s5",
    )(*args)


def _merge_kernel(x_ref, shift_ref, scale_ref, gate_ref, gnorm_ref, of_ref, ob_ref, zs_ref, gout_ref, ys_ref, u_ref,
                  oc_ref, wg_ref, wba_ref, wbb_ref, wbc_ref, wout_ref, wglu_ref, bglu_ref, dskip_ref, xo_ref):
    x = x_ref[...]
    h = ((_rms(x) * gnorm_ref[...]) * (1.0 + scale_ref[0]) + shift_ref[0]).astype(BF16)
    od = of_ref[...] + ob_ref[...]
    gout = gout_ref[...]
    oa = jnp.concatenate([_rms(od[:, hh * HEAD_DIM_A:(hh + 1) * HEAD_DIM_A]) * gout for hh in range(H_A)], axis=1)
    oa = oa * zs_ref[...]
    yb = jax.nn.gelu(ys_ref[...] + dskip_ref[...] * u_ref[...])
    ob = yb * jax.nn.sigmoid(_mm(yb, wglu_ref[...]) + bglu_ref[...])
    w_g = wg_ref
    acc = None
    for nbr, (o, w) in enumerate(((oa, wba_ref), (ob, wbb_ref), (oc_ref[...], wbc_ref))):
        gate = jax.nn.sigmoid(jnp.dot(h, w_g[:, nbr * D_MODEL:(nbr + 1) * D_MODEL], preferred_element_type=F32))
        term = gate * _mm(o, w[...])
        acc = term if acc is None else acc + term
    out = _mm(acc, wout_ref[...])
    xo_ref[...] = x + gate_ref[0] * out


def _merge(x, mods_rows, row_of_tile, lp, o_f, o_b, zs, ys, u, oc, tl):
    n = x.shape[0]
    const2 = lambda i: (0, 0)
    tok = lambda w: pl.BlockSpec((tl, w), lambda i: (i, 0))
    mod = lambda j: pl.BlockSpec((1, 1, D_MODEL), lambda i: (row_of_tile(i) * 6 + j, 0, 0))
    return pl.pallas_call(
        _merge_kernel,
        grid=(n // tl,),
        in_specs=[tok(D_MODEL), mod(0), mod(1), mod(2), pl.BlockSpec((1, D_MODEL), const2),
                  tok(A_WIDTH), tok(A_WIDTH), tok(A_WIDTH), pl.BlockSpec((1, HEAD_DIM_A), const2),
                  tok(B_WIDTH), tok(B_WIDTH), tok(MLA_W),
                  pl.BlockSpec((D_MODEL, N_BRANCH * D_MODEL), const2),
                  pl.BlockSpec((BRANCH_WIDTH, D_MODEL), const2), pl.BlockSpec((BRANCH_WIDTH, D_MODEL), const2),
                  pl.BlockSpec((MLA_W, D_MODEL), const2), pl.BlockSpec((D_MODEL, D_MODEL), const2),
                  pl.BlockSpec((B_WIDTH, B_WIDTH), const2), pl.BlockSpec((1, B_WIDTH), const2),
                  pl.BlockSpec((1, B_WIDTH), const2)],
        out_specs=tok(D_MODEL),
        out_shape=jax.ShapeDtypeStruct((n, D_MODEL), F32),
        compiler_params=_cp(("parallel",)),
        name="merge",
    )(x, mods_rows, mods_rows, mods_rows, lp['g_norm_mix'], o_f, o_b, zs, lp['g_delta_out'], ys, u, oc,
      lp['w_gates'], lp['w_br_a'], lp['w_br_b'],
      lp['w_br_c_pad'], lp['w_out'], lp['w_glu'], lp['b_glu'], lp['s5_d'])


FF_BLK = 256


def _ffn_kernel(*refs, tl, tiles_per_seq, final):
    it = iter(refs)
    xp_ref, xc_ref, xn_ref = next(it), next(it), next(it)
    shift_ref, scale_ref, gate_ref, gnorm_ref = next(it), next(it), next(it), next(it)
    wup_ref, convw_ref, convb_ref, wdown_ref = next(it), next(it), next(it), next(it)
    gfin_ref = next(it) if final else None
    xo_ref = next(it)
    yo_ref = next(it) if final else None
    h_scr, acc_scr = next(it), next(it)

    i = pl.program_id(0)
    pos = i % tiles_per_seq
    mod_scale = 1.0 + scale_ref[0]
    mod_shift = shift_ref[0]
    gain = gnorm_ref[...]

    def norm_mod(x):
        return ((_rms(x) * gain) * mod_scale + mod_shift).astype(BF16)

    x = xc_ref[...]
    h_scr[0:HALO, :] = norm_mod(xp_ref[...])
    h_scr[HALO:HALO + tl, :] = norm_mod(x)
    h_scr[HALO + tl:HALO + tl + HALO, :] = norm_mod(xn_ref[...])
    rowi = lax.broadcasted_iota(jnp.int32, (tl + 2 * HALO, 1), 0)
    valid = jnp.logical_and(jnp.logical_or(pos > 0, rowi >= HALO),
                            jnp.logical_or(pos < tiles_per_seq - 1, rowi < HALO + tl)).astype(F32)
    h_ext = h_scr[...]
    pad = FFN_CONV // 2
    n_ext = tl + 2 * HALO

    def conv_act(cols):
        up = jnp.dot(h_ext, wup_ref[:, cols], preferred_element_type=F32) * valid
        acc = None
        for t in range(FFN_CONV):
            sh = (pad - t) % n_ext
            src = up if sh == 0 else pltpu.roll(up, sh, 0)
            term = src[HALO:HALO + tl, :] * convw_ref[t:t + 1, cols]
            acc = term if acc is None else acc + term
        return acc + convb_ref[:, cols]

    for j in range(D_FF // FF_BLK):
        gcols = slice(j * FF_BLK, (j + 1) * FF_BLK)
        vcols = slice(D_FF + j * FF_BLK, D_FF + (j + 1) * FF_BLK)
        act = (_silu(conv_act(gcols)) * conv_act(vcols)).astype(BF16)
        part = jnp.dot(act, wdown_ref[gcols, :], preferred_element_type=F32)
        if j == 0:
            acc_scr[...] = part
        else:
            acc_scr[...] += part
    xo = x + gate_ref[0] * acc_scr[...]
    xo_ref[...] = xo
    if final:
        yo_ref[...] = _rms(xo) * gfin_ref[...]


def _ffn(x, mods_rows, row_of_tile, lp, seq_len, tl, g_final):
    n = x.shape[0]
    tps = seq_len // tl
    nh8 = n // HALO
    final = g_final is not None

    def xprev(i):
        return (jnp.maximum(i * (tl // HALO) - 1, 0), 0)

    def xnext(i):
        return (jnp.minimum((i + 1) * (tl // HALO), nh8 - 1), 0)

    const2 = lambda i: (0, 0)
    mod = lambda j: pl.BlockSpec((1, 1, D_MODEL), lambda i: (row_of_tile(i) * 6 + j, 0, 0))
    tok = pl.BlockSpec((tl, D_MODEL), lambda i: (i, 0))
    in_specs = [pl.BlockSpec((HALO, D_MODEL), xprev), tok, pl.BlockSpec((HALO, D_MODEL), xnext),
                mod(3), mod(4), mod(5), pl.BlockSpec((1, D_MODEL), const2),
                pl.BlockSpec((D_MODEL, 2 * D_FF), const2), pl.BlockSpec((FFN_CONV, 2 * D_FF), const2),
                pl.BlockSpec((1, 2 * D_FF), const2), pl.BlockSpec((D_FF, D_MODEL), const2)]
    args = [x, x, x, mods_rows, mods_rows, mods_rows, lp['g_norm_ffn'], lp['w_ffn_up'], lp['conv_ffn'],
            lp['b_conv_ffn'], lp['w_ffn_down']]
    out_specs = [tok]
    out_shape = [jax.ShapeDtypeStruct((n, D_MODEL), F32)]
    if final:
        in_specs.append(pl.BlockSpec((1, D_MODEL), const2))
        args.append(g_final)
        out_specs.append(tok)
        out_shape.append(jax.ShapeDtypeStruct((n, D_MODEL), F32))
    return pl.pallas_call(
        functools.partial(_ffn_kernel, tl=tl, tiles_per_seq=tps, final=final),
        grid=(n // tl,),
        in_specs=in_specs,
        out_specs=out_specs,
        out_shape=out_shape,
        scratch_shapes=[pltpu.VMEM((tl + 2 * HALO, D_MODEL), BF16), pltpu.VMEM((tl, D_MODEL), F32)],
        compiler_params=_cp(("parallel",)),
        name="ffn",
    )(*args)


def _pad_heads(w, head_w, n_heads):
    k = w.shape[0]
    w = w.reshape(k, n_heads, head_w)
    w = jnp.pad(w, ((0, 0), (0, 0), (0, HEAD_PAD - head_w)))
    return w.reshape(k, n_heads * HEAD_PAD)


def _layer_params(l, p):
    w_in = p['w_in'][l]
    o = 0
    parts = {}
    for name, wd in (('qkv', 3 * A_WIDTH), ('z', A_WIDTH), ('beta', 2 * H_A), ('alpha', 2 * H_A), ('u', B_WIDTH),
                     ('qa', Q_LORA), ('kva', KV_LORA), ('kr', QK_ROPE), ('gates', N_BRANCH * D_MODEL)):
        parts[name] = w_in[:, o:o + wd]
        o += wd
    small = jnp.concatenate([parts['beta'], parts['alpha'], parts['kr'],
                             jnp.zeros((D_MODEL, 128 - 4 * H_A - QK_ROPE), F32)], axis=1)
    w_rest = jnp.concatenate([parts['z'], parts['u'], parts['qa'], parts['kva'], small], axis=1)
    pad8 = lambda v: jnp.pad(v.reshape(1, 2 * H_A), ((0, 0), (2 * H_A, 128 - 4 * H_A)))
    w_kv = p['w_kv_b'][l].reshape(KV_LORA, H_C, QK_NOPE + V_HEAD)
    w_k = w_kv[:, :, :QK_NOPE].reshape(KV_LORA, H_C * QK_NOPE)
    w_v = w_kv[:, :, QK_NOPE:].reshape(KV_LORA, H_C * V_HEAD)
    w_br_c = p['w_branch'][l, 2].reshape(H_C, V_HEAD, D_MODEL)
    w_br_c = jnp.pad(w_br_c, ((0, 0), (0, HEAD_PAD - V_HEAD), (0, 0))).reshape(MLA_W, D_MODEL)
    row = lambda v: v.reshape(1, -1)
    return {
        'g_norm_mix': row(p['g_norm_mix'][l]), 'g_norm_ffn': row(p['g_norm_ffn'][l]),
        'w_qkv': parts['qkv'].astype(BF16), 'w_rest': w_rest.astype(BF16), 'w_gates': parts['gates'].astype(BF16),
        'conv_qkv': p['conv_qkv'][l],
        'a_log128': pad8(p['a_log'][l]), 'dt_bias128': pad8(p['dt_bias'][l]),
        'g_delta_out': row(p['g_delta_out'][l]),
        'g_q_a': row(p['g_q_a'][l]), 'g_kv_a': row(p['g_kv_a'][l]),
        'w_q_pad': _pad_heads(p['w_q_b'][l], QK_NOPE + QK_ROPE, H_C).astype(BF16),
        'w_k_pad': _pad_heads(w_k, QK_NOPE, H_C).astype(BF16),
        'w_v_pad': _pad_heads(w_v, V_HEAD, H_C).astype(BF16),
        'w_br_a': p['w_branch'][l, 0].astype(BF16), 'w_br_b': p['w_branch'][l, 1].astype(BF16),
        'w_br_c_pad': w_br_c.astype(BF16),
        'w_out': p['w_out'][l].astype(BF16),
        'w_glu': p['w_glu'][l].astype(BF16), 'b_glu': row(p['b_glu'][l]), 's5_d': row(p['s5_d'][l]),
        'w_ffn_up': p['w_ffn_up'][l].astype(BF16), 'conv_ffn': p['conv_ffn'][l],
        'b_conv_ffn': row(p['b_conv_ffn'][l]), 'w_ffn_down': p['w_ffn_down'][l].astype(BF16),
    }


def _rope_tables(length):
    rows = length // GRID_W
    row = jnp.repeat(jnp.arange(rows, dtype=F32), GRID_W)
    col = (jnp.arange(length) % GRID_W).astype(F32)
    n_freq = QK_ROPE // 4
    inv_freq = 1.0 / (ROPE_BASE ** (jnp.arange(n_freq, dtype=F32) / n_freq))
    ang = jnp.concatenate([row[:, None] * inv_freq, col[:, None] * inv_freq], axis=-1)
    cos, sin = jnp.cos(ang), jnp.sin(ang)
    ones = jnp.ones((length, QK_NOPE), F32)
    zeros = jnp.zeros((length, QK_NOPE), F32)
    tail = jnp.zeros((length, HEAD_PAD - QK_NOPE - QK_ROPE), F32)
    cos_t = jnp.concatenate([ones, cos, cos, tail], axis=1)
    sin_t = jnp.concatenate([zeros, -sin, sin, tail], axis=1)
    return cos_t, sin_t


def _to_s5_layout(u, bsz, seq_len):
    nchunk = seq_len // S5_T
    t = u.astype(BF16).reshape(bsz, nchunk, S5_T, S5_GROUPS, S5_GROUP)
    return t.transpose(3, 1, 0, 2, 4).reshape(S5_GROUPS, nchunk * bsz, S5_ROWS)


def _from_s5_layout(y, bsz, seq_len):
    nchunk = seq_len // S5_T
    t = y.reshape(S5_GROUPS, nchunk, bsz, S5_T, S5_GROUP)
    return t.transpose(2, 1, 3, 0, 4).reshape(bsz * seq_len, B_WIDTH)


def _layer(x, bsz, seq_len, mods_rows, row_of_tile, lp, s5mats, ctx, rope_tabs, tl, g_final):
    is_ctx = ctx is None
    outs = _inproj(x, mods_rows, row_of_tile, lp, seq_len, tl, rope_tabs, emit_ctx=is_ctx)
    if is_ctx:
        qkv, zs, u, bg, ckv, kr, qh, kh, vh = outs
    else:
        qkv, zs, u, bg, qh, kh, vh = outs

    nc = seq_len // CHUNK
    bgt = bg[:, :16].reshape(bsz, nc, CHUNK, 16).transpose(0, 1, 3, 2)
    prep = _delta_prep(qkv, bg, bgt, bsz, seq_len)
    d_out = _delta_recur(prep, None if is_ctx else ctx['s0'], bsz, seq_len, min(bsz, 4), emit_state=is_ctx)
    o_f = d_out[0].reshape(bsz * seq_len, A_WIDTH)
    o_b = d_out[1].reshape(bsz * seq_len, A_WIDTH)

    s_out = _s5(_to_s5_layout(u, bsz, seq_len), s5mats, None if is_ctx else ctx['h0'], seq_len // S5_T, bsz,
                emit_state=is_ctx)
    ys = _from_s5_layout(s_out[0], bsz, seq_len)

    segs = [(kh, vh, seq_len)]
    if not is_ctx:
        segs = [(ctx['kh'], ctx['vh'], ctx['past'])] + segs
    oc = _attention(qh, segs, bsz, seq_len, min(seq_len, 512))

    x = _merge(x, mods_rows, row_of_tile, lp, o_f, o_b, zs, ys, u, oc, tl)
    f_out = _ffn(x, mods_rows, row_of_tile, lp, seq_len, tl, g_final)
    extras = None
    if is_ctx:
        hfin = s_out[1]
        hfin = hfin.transpose(2, 1, 0, 3)
        extras = (d_out[2], hfin[..., :S5_STATE], hfin[..., S5_STATE:], ckv.reshape(bsz, seq_len, KV_LORA),
                  kr.reshape(bsz, seq_len, QK_ROPE))
    return f_out, extras


def kernel(x_prompt, x_sample, state_delta, state_s5_re, state_s5_im, cache_ckv, cache_krope, c, c_ctx, w_mod, b_mod, g_norm_mix, g_norm_ffn, w_in, conv_qkv, a_log, dt_bias, g_delta_out, s5_lam_re, s5_lam_im, s5_log_dt, s5_b_re, s5_b_im, s5_c_re, s5_c_im, s5_d, w_glu, b_glu, g_q_a, w_q_b, g_kv_a, w_kv_b, w_branch, w_out, w_ffn_up, conv_ffn, b_conv_ffn, w_ffn_down, g_final):
    p = dict(g_norm_mix=g_norm_mix, g_norm_ffn=g_norm_ffn, w_in=w_in, conv_qkv=conv_qkv, a_log=a_log,
             dt_bias=dt_bias, g_delta_out=g_delta_out, s5_d=s5_d, w_glu=w_glu, b_glu=b_glu, g_q_a=g_q_a,
             w_q_b=w_q_b, g_kv_a=g_kv_a, w_kv_b=w_kv_b, w_branch=w_branch, w_out=w_out, w_ffn_up=w_ffn_up,
             conv_ffn=conv_ffn, b_conv_ffn=b_conv_ffn, w_ffn_down=w_ffn_down)
    bp, lp_len, _ = x_prompt.shape
    bs, ls_len, _ = x_sample.shape
    past = cache_ckv.shape[2]
    depth = w_in.shape[0]

    mod_rows = 16
    cvec = jnp.concatenate([c_ctx[None, :], c, jnp.zeros((mod_rows - 1 - bs, D_MODEL), F32)], axis=0).astype(F32)
    mods = _modulation(cvec, w_mod, b_mod)

    g = S5_GROUPS
    b_re_t = s5_b_re.reshape(depth * g, S5_STATE, S5_GROUP).transpose(0, 2, 1)
    b_im_t = s5_b_im.reshape(depth * g, S5_STATE, S5_GROUP).transpose(0, 2, 1)
    s5f = _s5prep(s5_lam_re, s5_lam_im, s5_log_dt, b_re_t, b_im_t, s5_c_re, s5_c_im, 0)
    s5b = _s5prep(s5_lam_re, s5_lam_im, s5_log_dt, b_re_t, b_im_t, s5_c_re, s5_c_im, 1)

    rope_tabs = _rope_tables(ls_len)
    tl_p = min(lp_len, 512)
    tl_s = min(ls_len, 512)
    tiles_per_seq_s = ls_len // tl_s

    xp = x_prompt.astype(F32).reshape(bp * lp_len, D_MODEL)
    xs = x_sample.astype(F32).reshape(bs * ls_len, D_MODEL)
    deltas, s5_res, s5_ims, ckvs, kropes = [], [], [], [], []
    yp = ys = None
    for l in range(depth):
        lp = _layer_params(l, p)
        mods_rows = mods[l].reshape(mod_rows * 6, 1, D_MODEL)
        sl = slice(l * g, (l + 1) * g)
        s5mats = (s5f[0][sl], s5b[0][sl], s5f[1][sl], s5b[1][sl], s5f[2][sl], s5b[2][sl], s5f[3][sl], s5b[3][sl])
        last = l == depth - 1
        gfin = g_final.reshape(1, D_MODEL) if last else None

        out_p, extras = _layer(xp, bp, lp_len, mods_rows, lambda i: 0, lp, s5mats, None, None, tl_p, gfin)
        deltas.append(extras[0])
        s5_res.append(extras[1])
        s5_ims.append(extras[2])
        ckvs.append(extras[3])
        kropes.append(extras[4])

        kh_c, vh_c = _kvcache(cache_ckv[:, l].astype(F32).reshape(bs * past, KV_LORA),
                              cache_krope[:, l].astype(F32).reshape(bs * past, QK_ROPE), lp, past)
        h0 = jnp.concatenate([state_s5_re[:, l], state_s5_im[:, l]], axis=-1).astype(F32)
        ctx = dict(s0=state_delta[:, l].astype(F32), h0=h0.transpose(2, 1, 0, 3), kh=kh_c, vh=vh_c, past=past)
        out_s, _ = _layer(xs, bs, ls_len, mods_rows, lambda i: 1 + i // tiles_per_seq_s, lp, s5mats, ctx, rope_tabs,
                          tl_s, gfin)
        xp, xs = out_p[0], out_s[0]
        if last:
            yp, ys = out_p[1], out_s[1]

    y_prompt = yp.reshape(bp, lp_len, D_MODEL)
    y_sample = ys.reshape(bs, ls_len, D_MODEL)
    return (y_prompt, y_sample, jnp.stack(deltas, axis=1), jnp.stack(s5_res, axis=1), jnp.stack(s5_ims, axis=1),
            jnp.stack(ckvs, axis=1), jnp.stack(kropes, axis=1))
```

```python
import functools
import math

import jax
import jax.numpy as jnp
from jax import lax
from jax.experimental import pallas as pl
from jax.experimental.pallas import tpu as pltpu

F32 = jnp.float32
BF16 = jnp.bfloat16

D_MODEL = 1024
DEPTH = 4
GRID_W = 64
H_A = 4
HEAD_DIM_A = 128
A_WIDTH = H_A * HEAD_DIM_A
SHORT_CONV = 5
CHUNK = 64
S5_GROUP = 16
S5_STATE = 64
B_WIDTH = 512
S5_GROUPS = B_WIDTH // S5_GROUP
H_C = 8
QK_NOPE = 64
QK_ROPE = 32
V_HEAD = 64
Q_LORA = 384
KV_LORA = 256
ROPE_BASE = 10000.0
N_BRANCH = 3
BRANCH_WIDTH = 512
D_FF = 2816
FFN_CONV = 3
NORM_EPS = 1e-6

LANES = 128
HEAD_PAD = LANES
SUB = 16
S5_T = 8
S5_ROWS = S5_T * S5_GROUP
S5_SG_GROUPS = LANES // S5_GROUP
S5_SG = S5_GROUPS // S5_SG_GROUPS
S5_K = S5_T * LANES

VMEM_LIMIT = 56 * 1024 * 1024
HALO = 16


def _cp(sem):
    return pltpu.CompilerParams(dimension_semantics=sem, vmem_limit_bytes=VMEM_LIMIT)


def _mm(a, b):
    return jnp.dot(a.astype(BF16), b.astype(BF16), preferred_element_type=F32)


def _bmm(a, b):
    return lax.dot_general(a.astype(BF16), b.astype(BF16), (((2,), (1,)), ((0,), (0,))),
                           preferred_element_type=F32)


def _bmm_nt(a, b):
    return lax.dot_general(a.astype(BF16), b.astype(BF16), (((2,), (2,)), ((0,), (0,))),
                           preferred_element_type=F32)


def _bmm_tn(a, b):
    return lax.dot_general(a.astype(BF16), b.astype(BF16), (((1,), (1,)), ((0,), (0,))),
                           preferred_element_type=F32)


def _split3(x):
    x1 = x.astype(BF16)
    r = x - x1.astype(F32)
    x2 = r.astype(BF16)
    x3 = (r - x2.astype(F32)).astype(BF16)
    return x1, x2, x3


def _mm_exact_rhs(a, b_bf16):
    a1, a2, a3 = _split3(a)
    d = lambda t: jnp.dot(t, b_bf16, preferred_element_type=F32)
    return d(a1) + d(a2) + d(a3)


def _mm_exact_lhs(a_bf16, b):
    b1, b2, b3 = _split3(b)
    d = lambda t: jnp.dot(a_bf16, t, preferred_element_type=F32)
    return d(b1) + d(b2) + d(b3)


def _mm3_nt(a, b):
    a1 = a.astype(BF16)
    a2 = (a - a1.astype(F32)).astype(BF16)
    b1 = b.astype(BF16)
    b2 = (b - b1.astype(F32)).astype(BF16)
    d = lambda s, t: lax.dot_general(s, t, (((1,), (1,)), ((), ())), preferred_element_type=F32)
    return d(a1, b1) + d(a1, b2) + d(a2, b1)


def _silu(x):
    return x * jax.nn.sigmoid(x)


def _rms(x):
    return x * lax.rsqrt(jnp.mean(x * x, axis=-1, keepdims=True) + NORM_EPS)


def _mod_kernel(c_ref, w_ref, b_ref, o_ref):
    s = _silu(c_ref[...])
    o_ref[0] = _mm(s, w_ref[0]) + b_ref[0]


def _modulation(cvec, w_mod, b_mod):
    rows = cvec.shape[0]
    nblk = w_mod.shape[-1] // D_MODEL
    return pl.pallas_call(
        _mod_kernel,
        grid=(DEPTH, nblk),
        in_specs=[
            pl.BlockSpec((rows, D_MODEL), lambda l, j: (0, 0)),
            pl.BlockSpec((1, D_MODEL, D_MODEL), lambda l, j: (l, 0, j)),
            pl.BlockSpec((1, 1, D_MODEL), lambda l, j: (l, 0, j)),
        ],
        out_specs=pl.BlockSpec((1, rows, D_MODEL), lambda l, j: (l, 0, j)),
        out_shape=jax.ShapeDtypeStruct((DEPTH, rows, w_mod.shape[-1]), F32),
        compiler_params=_cp(("parallel", "parallel")),
        name="modulation",
    )(cvec, w_mod, b_mod.reshape(DEPTH, 1, -1))


REST_W = 512 + 512 + Q_LORA + KV_LORA + LANES
OFF_Z, OFF_U, OFF_QA, OFF_KVA, OFF_SM = 0, 512, 1024, 1024 + Q_LORA, 1024 + Q_LORA + KV_LORA
SM_KR = 16
MLA_W = H_C * HEAD_PAD


def _rope_apply(x, cos, sin):
    lane = lax.broadcasted_iota(jnp.int32, x.shape, 1)
    partner = jnp.where(lane < QK_NOPE + QK_ROPE // 2,
                        pltpu.roll(x, HEAD_PAD - QK_ROPE // 2, 1),
                        pltpu.roll(x, QK_ROPE // 2, 1))
    return x * cos + partner * sin


def _inproj_kernel(*refs, tl, tiles_per_seq, rope, emit_ctx):
    it = iter(refs)
    xp_ref, xc_ref, xn_ref = next(it), next(it), next(it)
    shift_ref, scale_ref, gnorm_ref = next(it), next(it), next(it)
    wqkv_ref, convw_ref, wrest_ref = next(it), next(it), next(it)
    gqa_ref, gkva_ref, alog_ref, dtb_ref = next(it), next(it), next(it), next(it)
    wq_ref, wk_ref, wv_ref = next(it), next(it), next(it)
    cos_ref = sin_ref = None
    if rope:
        cos_ref, sin_ref = next(it), next(it)
    qkv_out, zs_out, u_out, bg_out = next(it), next(it), next(it), next(it)
    ckv_out = kr_out = None
    if emit_ctx:
        ckv_out, kr_out = next(it), next(it)
    qh_out, kh_out, vh_out = next(it), next(it), next(it)
    qkv_scr = next(it)

    i = pl.program_id(0)
    pos = i % tiles_per_seq
    mod_scale = 1.0 + scale_ref[0]
    mod_shift = shift_ref[0]
    gain = gnorm_ref[...]

    def norm_mod(x):
        return ((_rms(x) * gain) * mod_scale + mod_shift).astype(BF16)

    h_cur = norm_mod(xc_ref[...])
    h_ext = jnp.concatenate([norm_mod(xp_ref[...]), h_cur, norm_mod(xn_ref[...])], axis=0)

    n_ext = tl + 2 * HALO
    rowi = lax.broadcasted_iota(jnp.int32, (n_ext, 1), 0)
    valid = jnp.logical_and(jnp.logical_or(pos > 0, rowi >= HALO),
                            jnp.logical_or(pos < tiles_per_seq - 1, rowi < HALO + tl)).astype(F32)
    qkv_scr[...] = _mm(h_ext, wqkv_ref[...]) * valid
    pad = SHORT_CONV // 2
    for blk in range(3):
        cols = slice(blk * A_WIDTH, (blk + 1) * A_WIDTH)
        acc = None
        for t in range(SHORT_CONV):
            term = qkv_scr[HALO - pad + t:HALO - pad + t + tl, cols] * convw_ref[t:t + 1, cols]
            acc = term if acc is None else acc + term
        acc = _silu(acc)
        if blk < 2:
            parts = []
            for hh in range(H_A):
                a = acc[:, hh * HEAD_DIM_A:(hh + 1) * HEAD_DIM_A]
                parts.append(a * lax.rsqrt(jnp.sum(a * a, axis=-1, keepdims=True) + NORM_EPS))
            acc = jnp.concatenate(parts, axis=1)
        qkv_out[:, cols] = acc.astype(BF16)

    rest = _mm(h_cur, wrest_ref[...])
    zs_out[...] = _silu(rest[:, OFF_Z:OFF_Z + 512])
    u_out[...] = rest[:, OFF_U:OFF_U + 512]

    small = rest[:, OFF_SM:OFF_SM + LANES]
    lane = lax.broadcasted_iota(jnp.int32, small.shape, 1)
    beta = jax.nn.sigmoid(small)
    glog = -jnp.exp(alog_ref[...]) * jax.nn.softplus(small + dtb_ref[...])
    bg_out[...] = jnp.where(lane < 2 * H_A, beta, glog)

    qa = _rms(rest[:, OFF_QA:OFF_QA + Q_LORA]) * gqa_ref[...]
    ckv = _rms(rest[:, OFF_KVA:OFF_KVA + KV_LORA]) * gkva_ref[...]
    if emit_ctx:
        ckv_out[...] = ckv
        kr_out[...] = small[:, SM_KR:SM_KR + QK_ROPE]
    ckv_b = ckv.astype(BF16)
    qhat = _mm(qa, wq_ref[...])
    khat = _mm(ckv_b, wk_ref[...])
    vh_out[...] = _mm(ckv_b, wv_ref[...]).astype(BF16)
    kr_al = jnp.where((lane >= QK_NOPE) & (lane < QK_NOPE + QK_ROPE),
                      pltpu.roll(small, QK_NOPE - SM_KR, 1), 0.0)
    if rope:
        cos, sin = cos_ref[...], sin_ref[...]
        kr_al = _rope_apply(kr_al, cos, sin)
    qscale = (QK_NOPE + QK_ROPE) ** -0.5
    for hh in range(H_C):
        cs = slice(hh * HEAD_PAD, (hh + 1) * HEAD_PAD)
        qh = qhat[:, cs]
        if rope:
            qh = _rope_apply(qh, cos, sin)
        qh_out[:, cs] = (qh * qscale).astype(BF16)
        kh_out[:, cs] = (khat[:, cs] + kr_al).astype(BF16)


def _halo_maps(n, tl):
    nh = n // HALO

    def xprev(i):
        return (jnp.maximum(i * (tl // HALO) - 1, 0), 0)

    def xnext(i):
        return (jnp.minimum((i + 1) * (tl // HALO), nh - 1), 0)

    return xprev, xnext


def _inproj(x, mods_rows, row_of_tile, lp, seq_len, tl, rope_tabs, emit_ctx):
    n = x.shape[0]
    nt = n // tl
    tps = seq_len // tl
    rope = rope_tabs is not None
    xprev, xnext = _halo_maps(n, tl)
    const2 = lambda i: (0, 0)
    in_specs = [
        pl.BlockSpec((HALO, D_MODEL), xprev),
        pl.BlockSpec((tl, D_MODEL), lambda i: (i, 0)),
        pl.BlockSpec((HALO, D_MODEL), xnext),
        pl.BlockSpec((1, 1, D_MODEL), lambda i: (row_of_tile(i) * 6 + 0, 0, 0)),
        pl.BlockSpec((1, 1, D_MODEL), lambda i: (row_of_tile(i) * 6 + 1, 0, 0)),
        pl.BlockSpec((1, D_MODEL), const2),
        pl.BlockSpec((D_MODEL, 3 * A_WIDTH), const2),
        pl.BlockSpec((SHORT_CONV, 3 * A_WIDTH), const2),
        pl.BlockSpec((D_MODEL, REST_W), const2),
        pl.BlockSpec((1, Q_LORA), const2),
        pl.BlockSpec((1, KV_LORA), const2),
        pl.BlockSpec((1, LANES), const2),
        pl.BlockSpec((1, LANES), const2),
        pl.BlockSpec((Q_LORA, MLA_W), const2),
        pl.BlockSpec((KV_LORA, MLA_W), const2),
        pl.BlockSpec((KV_LORA, MLA_W), const2),
    ]
    args = [x, x, x, mods_rows, mods_rows, lp['g_norm_mix'], lp['w_qkv'], lp['conv_qkv'], lp['w_rest'],
            lp['g_q_a'], lp['g_kv_a'], lp['a_log128'], lp['dt_bias128'], lp['w_q_pad'], lp['w_k_pad'], lp['w_v_pad']]
    if rope:
        in_specs += [pl.BlockSpec((tl, HEAD_PAD), lambda i: (i % tps, 0))] * 2
        args += list(rope_tabs)
    tok = lambda w: pl.BlockSpec((tl, w), lambda i: (i, 0))
    out_specs = [tok(3 * A_WIDTH), tok(512), tok(512), tok(LANES)]
    out_shape = [jax.ShapeDtypeStruct((n, 3 * A_WIDTH), BF16), jax.ShapeDtypeStruct((n, 512), F32),
                 jax.ShapeDtypeStruct((n, 512), F32), jax.ShapeDtypeStruct((n, LANES), F32)]
    if emit_ctx:
        out_specs += [tok(KV_LORA), tok(QK_ROPE)]
        out_shape += [jax.ShapeDtypeStruct((n, KV_LORA), F32), jax.ShapeDtypeStruct((n, QK_ROPE), F32)]
    out_specs += [tok(MLA_W)] * 3
    out_shape += [jax.ShapeDtypeStruct((n, MLA_W), BF16)] * 3
    return pl.pallas_call(
        functools.partial(_inproj_kernel, tl=tl, tiles_per_seq=tps, rope=rope, emit_ctx=emit_ctx),
        grid=(nt,),
        in_specs=in_specs,
        out_specs=out_specs,
        out_shape=out_shape,
        scratch_shapes=[pltpu.VMEM((tl + 2 * HALO, 3 * A_WIDTH), F32)],
        compiler_params=_cp(("parallel",)),
        name="inproj",
    )(*args)


def _kvcache_kernel(ckv_ref, kr_ref, wk_ref, wv_ref, kh_out, vh_out):
    ckv_b = ckv_ref[...].astype(BF16)
    khat = _mm(ckv_b, wk_ref[...])
    vh_out[...] = _mm(ckv_b, wv_ref[...]).astype(BF16)
    kr_al = kr_ref[...]
    for hh in range(H_C):
        cs = slice(hh * HEAD_PAD, (hh + 1) * HEAD_PAD)
        kh_out[:, cs] = (khat[:, cs] + kr_al).astype(BF16)


def _kvcache(ckv, kr, lp, tl):
    n = ckv.shape[0]
    const2 = lambda i: (0, 0)
    kr = jnp.pad(kr, ((0, 0), (QK_NOPE, HEAD_PAD - QK_NOPE - QK_ROPE)))
    return pl.pallas_call(
        _kvcache_kernel,
        grid=(n // tl,),
        in_specs=[pl.BlockSpec((tl, KV_LORA), lambda i: (i, 0)), pl.BlockSpec((tl, HEAD_PAD), lambda i: (i, 0)),
                  pl.BlockSpec((KV_LORA, MLA_W), const2), pl.BlockSpec((KV_LORA, MLA_W), const2)],
        out_specs=[pl.BlockSpec((tl, MLA_W), lambda i: (i, 0))] * 2,
        out_shape=[jax.ShapeDtypeStruct((n, MLA_W), BF16)] * 2,
        compiler_params=_cp(("parallel",)),
        name="kvcache",
    )(ckv, kr, lp['w_k_pad'], lp['w_v_pad'])


ATT_SUB = 256


def _attn_kernel(*refs, nseg, tq):
    q_ref = refs[0]
    k_refs = refs[1:1 + nseg]
    v_refs = refs[1 + nseg:1 + 2 * nseg]
    o_ref = refs[1 + 2 * nseg]
    for sb in range(tq // ATT_SUB):
        rows = slice(sb * ATT_SUB, (sb + 1) * ATT_SUB)
        q = q_ref[rows, :]
        s = [lax.dot_general(q, k[...], (((1,), (1,)), ((), ())), preferred_element_type=F32) for k in k_refs]
        m = s[0].max(axis=-1, keepdims=True)
        for t in s[1:]:
            m = jnp.maximum(m, t.max(axis=-1, keepdims=True))
        p = [jnp.exp(t - m) for t in s]
        den = p[0].sum(axis=-1, keepdims=True)
        for t in p[1:]:
            den = den + t.sum(axis=-1, keepdims=True)
        acc = None
        for t, v in zip(p, v_refs):
            part = jnp.dot(t.astype(BF16), v[...], preferred_element_type=F32)
            acc = part if acc is None else acc + part
        o_ref[rows, :] = (acc / den).astype(BF16)


def _attention(qh, segs, bsz, seq_len, tq):
    nq = seq_len // tq
    in_specs = [pl.BlockSpec((tq, HEAD_PAD), lambda b, h, i: (b * nq + i, h))]
    args = [qh]
    for which in (0, 1):
        for seg in segs:
            in_specs.append(pl.BlockSpec((seg[2], HEAD_PAD), lambda b, h, i: (b, h)))
            args.append(seg[which])
    return pl.pallas_call(
        functools.partial(_attn_kernel, nseg=len(segs), tq=tq),
        grid=(bsz, H_C, nq),
        in_specs=in_specs,
        out_specs=pl.BlockSpec((tq, HEAD_PAD), lambda b, h, i: (b * nq + i, h)),
        out_shape=jax.ShapeDtypeStruct((bsz * seq_len, MLA_W), BF16),
        compiler_params=_cp(("parallel", "parallel", "parallel")),
        name="attention",
    )(*args)


DELTA_TB = 256
DELTA_G = DELTA_TB // CHUNK


def _delta_prep_kernel(qkv_ref, bg_ref, bgt_ref, uf_ref, wf_ref, qef_ref, kef_ref, qkf_ref,
                       ub_ref, wb_ref, qeb_ref, keb_ref, qkb_ref, dec_ref):
    c = CHUNK
    outs = ((uf_ref, wf_ref, qef_ref, kef_ref, qkf_ref), (ub_ref, wb_ref, qeb_ref, keb_ref, qkb_ref))
    ri = lax.broadcasted_iota(jnp.int32, (c, c), 0)
    ci = lax.broadcasted_iota(jnp.int32, (c, c), 1)
    tril = (ri >= ci).astype(BF16)
    triu = (ri <= ci).astype(BF16)
    same_blk = (ri // SUB) == (ci // SUB)
    scale = HEAD_DIM_A ** -0.5

    insts = [(g, d, hh) for g in range(DELTA_G) for d in range(2) for hh in range(H_A)]
    n = len(insts)
    qkv = qkv_ref[...]
    bg = bg_ref[...]

    def head_stack(base):
        return jnp.stack([qkv[g * c:(g + 1) * c, base + hh * HEAD_DIM_A:base + (hh + 1) * HEAD_DIM_A]
                          for g, d, hh in insts])

    q, k, v = head_stack(0), head_stack(A_WIDTH), head_stack(2 * A_WIDTH)

    cols, rws = {}, {}
    for g in range(DELTA_G):
        bgc = bg[g * c:(g + 1) * c, :]
        bgr = bgt_ref[0, g]
        cols[g] = (bgc, _mm_exact_lhs(tril, bgc), _mm_exact_lhs(triu, bgc))
        rws[g] = (_mm_exact_rhs(bgr, triu), _mm_exact_rhs(bgr, tril))
    beta = jnp.stack([cols[g][0][:, d * H_A + hh:d * H_A + hh + 1] for g, d, hh in insts])
    gcc = jnp.stack([cols[g][1 + d][:, 2 * H_A + d * H_A + hh:2 * H_A + d * H_A + hh + 1]
                     for g, d, hh in insts])
    gcr = jnp.stack([rws[g][d][2 * H_A + d * H_A + hh:2 * H_A + d * H_A + hh + 1, :] for g, d, hh in insts])
    glast = jnp.stack([gcc[i, (c - 1 if insts[i][1] == 0 else 0):(c if insts[i][1] == 0 else 1), :]
                       for i in range(n)])

    sign = jnp.stack([jnp.full((1, 1), 1 - 2 * d, jnp.int32) for g, d, hh in insts])
    tri = (ri - ci)[None] * sign
    incl = tri >= 0
    strict = tri > 0
    blk = jnp.broadcast_to(same_blk[None], incl.shape)

    kf = k.astype(F32)
    kb = kf * beta
    decay = jnp.where(incl, jnp.exp(jnp.where(incl, gcc - gcr, 0.0)), 0.0)
    a = _bmm_nt(kb, k) * decay
    qk = _bmm_nt(q, k) * (decay * scale)
    dg = jnp.where(blk & strict, a, 0.0)
    lo = jnp.where(jnp.logical_not(blk) & strict, a, 0.0)
    m1 = _bmm(dg, dg)
    p = m1 - dg - _bmm(dg, m1)
    pw = m1
    for _ in range(int(math.log2(SUB)) - 2):
        pw = _bmm(pw, pw)
        p = p + pw + _bmm(p, pw)
    nm = lo + _bmm(p, lo)
    n2 = _bmm(nm, nm)
    e_col = jnp.exp(gcc)
    rhs = jnp.concatenate([v.astype(F32) * beta, kb * e_col], axis=2)
    y = rhs + _bmm(p, rhs)
    y = y + _bmm(n2, y)
    x = y - _bmm(nm, y)
    u, w = x[:, :, :HEAD_DIM_A], x[:, :, HEAD_DIM_A:]
    qe = q.astype(F32) * (e_col * scale)
    ke = kf * jnp.exp(glast - gcc)
    dec = jnp.exp(glast)
    zpad = jnp.zeros((c, HEAD_DIM_A - c), BF16)
    for i, (g, d, hh) in enumerate(insts):
        rows = slice(g * c, (g + 1) * c)
        cs = slice(hh * HEAD_DIM_A, (hh + 1) * HEAD_DIM_A)
        u_ref, w_ref, qe_ref, ke_ref, qk_ref = outs[d]
        u_ref[0, rows, cs] = u[i]
        w_ref[0, rows, cs] = w[i].astype(BF16)
        qe_ref[0, rows, cs] = qe[i].astype(BF16)
        ke_ref[0, rows, cs] = ke[i].astype(BF16)
        qk_ref[0, rows, cs] = jnp.concatenate([qk[i].astype(BF16), zpad], axis=1)
        dec_ref[0, g, d * H_A + hh:d * H_A + hh + 1, :] = jnp.broadcast_to(dec[i], (1, HEAD_DIM_A))


def _delta_prep(qkv, bg, bgt, bsz, seq_len):
    nblk = seq_len // DELTA_TB
    nc = seq_len // CHUNK
    tokf = jax.ShapeDtypeStruct((bsz, seq_len, A_WIDTH), F32)
    tokb = jax.ShapeDtypeStruct((bsz, seq_len, A_WIDTH), BF16)
    tspec = pl.BlockSpec((1, DELTA_TB, A_WIDTH), lambda b, j: (b, j, 0))
    return pl.pallas_call(
        _delta_prep_kernel,
        grid=(bsz, nblk),
        in_specs=[pl.BlockSpec((DELTA_TB, 3 * A_WIDTH), lambda b, j: (b * nblk + j, 0)),
                  pl.BlockSpec((DELTA_TB, LANES), lambda b, j: (b * nblk + j, 0)),
                  pl.BlockSpec((1, DELTA_G, 16, CHUNK), lambda b, j: (b, j, 0, 0))],
        out_specs=[tspec] * 10 + [pl.BlockSpec((1, DELTA_G, 2 * H_A, HEAD_DIM_A), lambda b, j: (b, j, 0, 0))],
        out_shape=[tokf, tokb, tokb, tokb, tokb] * 2 + [jax.ShapeDtypeStruct((bsz, nc, 2 * H_A, HEAD_DIM_A), F32)],
        compiler_params=_cp(("parallel", "parallel")),
        name="delta_prep",
    )(qkv, bg, bgt)


def _delta_recur_kernel(*refs, nb, nblk, has_s0, emit_state):
    it = iter(refs)
    fwd = [next(it) for _ in range(5)]
    bwd = [next(it) for _ in range(5)]
    decf_ref, decb_ref = next(it), next(it)
    s0_ref = next(it) if has_s0 else None
    of_ref, ob_ref = next(it), next(it)
    sfin_ref = next(it) if emit_state else None
    s_scr = next(it)
    j = pl.program_id(1)
    c = CHUNK

    @pl.when(j == 0)
    def _():
        if has_s0:
            s_scr[...] = s0_ref[...]
        else:
            s_scr[...] = jnp.zeros(s_scr.shape, F32)

    insts = [(bi, d, hh) for bi in range(nb) for d in range(2) for hh in range(H_A)]
    for step in range(DELTA_G):
        gsel = (step, DELTA_G - 1 - step)

        def stk(idx, width=HEAD_DIM_A):
            return jnp.stack([(fwd, bwd)[d][idx][bi, gsel[d] * c:(gsel[d] + 1) * c,
                                                 hh * HEAD_DIM_A:hh * HEAD_DIM_A + width]
                              for bi, d, hh in insts])

        u, w, qe, ke, qk = stk(0), stk(1), stk(2), stk(3), stk(4, c)
        dec = jnp.stack([(decf_ref, decb_ref)[d][bi, gsel[d], d * H_A + hh:d * H_A + hh + 1, :]
                         for bi, d, hh in insts])
        s_old = jnp.stack([s_scr[bi, d, hh] for bi, d, hh in insts])
        r1 = _bmm(jnp.concatenate([w, qe], axis=1), s_old)
        v_new = (u - r1[:, :c, :]).astype(BF16)
        o = r1[:, c:, :] + _bmm(qk, v_new)
        s_new = s_old * dec + _bmm_tn(ke, v_new)
        for i, (bi, d, hh) in enumerate(insts):
            s_scr[bi, d, hh] = s_new[i]
            (of_ref, ob_ref)[d][bi, gsel[d] * c:(gsel[d] + 1) * c, hh * HEAD_DIM_A:(hh + 1) * HEAD_DIM_A] = o[i]

    if emit_state:
        @pl.when(j == nblk - 1)
        def _():
            sfin_ref[...] = s_scr[...]


def _delta_recur(prep, s0, bsz, seq_len, nb, emit_state):
    nblk = seq_len // DELTA_TB
    has_s0 = s0 is not None
    fspec = pl.BlockSpec((nb, DELTA_TB, A_WIDTH), lambda b, j: (b, j, 0))
    bspec = pl.BlockSpec((nb, DELTA_TB, A_WIDTH), lambda b, j: (b, nblk - 1 - j, 0))
    dspec_f = pl.BlockSpec((nb, DELTA_G, 2 * H_A, HEAD_DIM_A), lambda b, j: (b, j, 0, 0))
    dspec_b = pl.BlockSpec((nb, DELTA_G, 2 * H_A, HEAD_DIM_A), lambda b, j: (b, nblk - 1 - j, 0, 0))
    st_spec = pl.BlockSpec((nb, 2, H_A, HEAD_DIM_A, HEAD_DIM_A), lambda b, j: (b, 0, 0, 0, 0))
    in_specs = [fspec] * 5 + [bspec] * 5 + [dspec_f, dspec_b]
    args = list(prep[:10]) + [prep[10], prep[10]]
    if has_s0:
        in_specs.append(st_spec)
        args.append(s0)
    out_specs = [fspec, bspec]
    out_shape = [jax.ShapeDtypeStruct((bsz, seq_len, A_WIDTH), F32)] * 2
    if emit_state:
        out_specs.append(st_spec)
        out_shape.append(jax.ShapeDtypeStruct((bsz, 2, H_A, HEAD_DIM_A, HEAD_DIM_A), F32))
    return pl.pallas_call(
        functools.partial(_delta_recur_kernel, nb=nb, nblk=nblk, has_s0=has_s0, emit_state=emit_state),
        grid=(bsz // nb, nblk),
        in_specs=in_specs,
        out_specs=out_specs,
        out_shape=out_shape,
        scratch_shapes=[pltpu.VMEM((nb, 2, H_A, HEAD_DIM_A, HEAD_DIM_A), F32)],
        compiler_params=_cp(("parallel", "arbitrary")),
        name="delta_recur",
    )(*args)


def _s5prep_kernel(lre_ref, lim_ref, ldt_ref, bre_ref, bim_ref, cre_ref, cim_ref,
                   kt_out, ere_out, eim_out, fre_out, fim_out, lt_out, *, backward):
    lre, lim = lre_ref[0], lim_ref[0]
    dt = jnp.exp(ldt_ref[0])
    zr, zi = lre * dt, lim * dt

    def lam_pow(e):
        mag = jnp.exp(e * zr)
        return mag * jnp.cos(e * zi), mag * jnp.sin(e * zi)

    one = jnp.ones((1, 1), F32)
    l1r, l1i = lam_pow(one)
    den = lre * lre + lim * lim
    nr, ni = l1r - 1.0, l1i
    cfr = (nr * lre + ni * lim) / den
    cfi = (ni * lre - nr * lim) / den
    bre, bim = bre_ref[0], bim_ref[0]
    bbr = cfr * bre - cfi * bim
    bbi = cfr * bim + cfi * bre
    cre, cim = cre_ref[0], cim_ref[0]

    j = (lax.broadcasted_iota(jnp.int32, (S5_ROWS, 1), 0) // S5_GROUP).astype(F32)
    tile = lambda m: jnp.concatenate([m] * S5_T, axis=0)
    c_r, c_i = tile(cre), tile(cim)
    b_r, b_i = tile(bbr), tile(bbi)

    pr, pi = lam_pow(j)
    qr = pr * c_r - pi * c_i
    qi = pr * c_i + pi * c_r
    kt_out[0] = _mm3_nt(qr, bbr) - _mm3_nt(qi, bbi)

    e_exp = j if backward else (S5_T - 1.0) - j
    er, ei = lam_pow(e_exp)
    ere_out[0] = er * b_r - ei * b_i
    eim_out[0] = er * b_i + ei * b_r

    f_exp = (S5_T - j) if backward else j + 1.0
    fr, fi = lam_pow(f_exp)
    fre_out[0] = fr * c_r - fi * c_i
    fim_out[0] = -(fr * c_i + fi * c_r)

    ltr, lti = lam_pow(one * float(S5_T))
    lt_out[0] = jnp.concatenate([ltr, lti], axis=0)


def _s5prep(lam_re, lam_im, log_dt, b_re_t, b_im_t, c_re, c_im, direction):
    g, p, cg = S5_GROUPS, S5_STATE, S5_GROUP
    n = DEPTH * g
    lam_idx = lambda i: ((i // g) * 2 * g + direction * g + i % g, 0, 0)
    par_idx = lambda i: (i, 0, 0)
    outs = pl.pallas_call(
        functools.partial(_s5prep_kernel, backward=bool(direction)),
        grid=(n,),
        in_specs=[pl.BlockSpec((1, 1, p), lam_idx), pl.BlockSpec((1, 1, p), lam_idx), pl.BlockSpec((1, 1, 1), lam_idx),
                  pl.BlockSpec((1, cg, p), par_idx), pl.BlockSpec((1, cg, p), par_idx),
                  pl.BlockSpec((1, cg, p), par_idx), pl.BlockSpec((1, cg, p), par_idx)],
        out_specs=[pl.BlockSpec((1, S5_ROWS, cg), par_idx)] + [pl.BlockSpec((1, S5_ROWS, p), par_idx)] * 4
                  + [pl.BlockSpec((1, 2, p), par_idx)],
        out_shape=[jax.ShapeDtypeStruct((n, S5_ROWS, cg), F32)] + [jax.ShapeDtypeStruct((n, S5_ROWS, p), F32)] * 4
                  + [jax.ShapeDtypeStruct((n, 2, p), F32)],
        compiler_params=_cp(("parallel",)),
        name="s5prep_bwd" if direction else "s5prep_fwd",
    )(lam_re.reshape(DEPTH * 2 * g, 1, p), lam_im.reshape(DEPTH * 2 * g, 1, p), log_dt.reshape(DEPTH * 2 * g, 1, 1),
      b_re_t, b_im_t, c_re.reshape(n, cg, p), c_im.reshape(n, cg, p))
    kt, ere, eim, fre, fim, lt = outs
    nq, gs, t8 = DEPTH * S5_SG, S5_SG_GROUPS, S5_T
    eye = jnp.eye(gs, dtype=F32)
    kt4 = kt.reshape(n, t8, cg, cg)
    tt = jnp.arange(t8)[None, :]
    ss = jnp.arange(t8)[:, None]
    lag = (ss - tt) if direction else (tt - ss)
    blocks = jnp.where((lag >= 0)[None, :, :, None, None], kt4[:, jnp.clip(lag, 0, t8 - 1)], 0.0)
    blocks = blocks.reshape(nq, gs, t8, t8, cg, cg)
    mmat = jnp.einsum('qgstca,gh->qsgathc', blocks, eye).reshape(nq, S5_K, S5_K)
    e_ri = jnp.stack([ere, eim], axis=1).reshape(nq, gs, 2, t8, cg, p)
    emat = jnp.einsum('qgrsap,gh->qsgarhp', e_ri, eye).reshape(nq, S5_K, S5_K).astype(BF16)
    f_ri = jnp.stack([fre, fim], axis=1).reshape(nq, gs, 2, t8, cg, p)
    fmat = jnp.einsum('qgrtcp,gh->qrhptgc', f_ri, eye).reshape(nq, S5_K, S5_K).astype(BF16)
    ltrow = lt.reshape(nq, gs, 2, p).transpose(0, 2, 1, 3).reshape(nq, 1, S5_K)
    return mmat, emat, fmat, ltrow


S5_RT = 256


def _s5_kernel(*refs, nchunk, nb, seq_len, has_h0, emit_state):
    it = iter(refs)
    u_ref = next(it)
    m_ref, ef_ref, eb_ref, ff_ref, fb_ref, ltf_ref, ltb_ref = (next(it) for _ in range(7))
    h0_ref = next(it) if has_h0 else None
    y_ref = next(it)
    hfin_ref = next(it) if emit_state else None
    up_scr, sf_scr, sb_scr = next(it), next(it), next(it)

    r = nchunk * nb
    rtile = min(S5_RT, r)
    half = S5_K // 2
    for t in range(S5_T):
        for b in range(nb):
            up_scr[t, pl.ds(b, nchunk, stride=nb), :] = u_ref[pl.ds(b * seq_len + t, nchunk, stride=S5_T), :]

    def u_tile(rt):
        rows = slice(rt * rtile, (rt + 1) * rtile)
        return jnp.concatenate([up_scr[t, rows, :] for t in range(S5_T)], axis=1).astype(BF16)

    for rt in range(r // rtile):
        rows = slice(rt * rtile, (rt + 1) * rtile)
        ub = u_tile(rt)
        sf_scr[rows, :] = jnp.dot(ub, ef_ref[0], preferred_element_type=F32)
        sb_scr[rows, :] = jnp.dot(ub, eb_ref[0], preferred_element_type=F32)

    lane = lax.broadcasted_iota(jnp.int32, (1, S5_K), 1)

    def coeffs(lt_ref):
        lt = lt_ref[0]
        sw = pltpu.roll(lt, half, 1)
        return jnp.where(lane < half, lt, sw), jnp.where(lane < half, -sw, lt)

    af, bf = coeffs(ltf_ref)
    ab, bb = coeffs(ltb_ref)
    if has_h0:
        h0f, h0b = h0_ref[0, 0, 0], h0_ref[0, 1, 0]
    else:
        h0f = h0b = jnp.zeros((nb, S5_K), F32)

    tile_rows = max(nb, 8)
    cps = tile_rows // nb
    n_iter = nchunk // cps

    def step(h, a, b, seg):
        return a * h + b * pltpu.roll(h, half, 1) + seg

    def body(i, carry):
        hf, hb = carry
        rf = pl.ds(pl.multiple_of(i * tile_rows, tile_rows), tile_rows)
        rb = pl.ds(pl.multiple_of((n_iter - 1 - i) * tile_rows, tile_rows), tile_rows)
        xf, xb = sf_scr[rf, :], sb_scr[rb, :]
        of, ob = [], [None] * cps
        for jj in range(cps):
            of.append(hf)
            hf = step(hf, af, bf, xf[jj * nb:(jj + 1) * nb, :])
        for jj in reversed(range(cps)):
            ob[jj] = hb
            hb = step(hb, ab, bb, xb[jj * nb:(jj + 1) * nb, :])
        sf_scr[rf, :] = of[0] if cps == 1 else jnp.concatenate(of, axis=0)
        sb_scr[rb, :] = ob[0] if cps == 1 else jnp.concatenate(ob, axis=0)
        return hf, hb

    hf, hb = lax.fori_loop(0, n_iter, body, (h0f, h0b))
    if emit_state:
        hfin_ref[0, 0, 0] = hf
        hfin_ref[0, 1, 0] = hb

    for rt in range(r // rtile):
        rows = slice(rt * rtile, (rt + 1) * rtile)
        y = jnp.dot(u_tile(rt), m_ref[0], preferred_element_type=F32)
        y = y + jnp.dot(sf_scr[rows, :].astype(BF16), ff_ref[0], preferred_element_type=F32)
        y = y + jnp.dot(sb_scr[rows, :].astype(BF16), fb_ref[0], preferred_element_type=F32)
        for t in range(S5_T):
            up_scr[t, rows, :] = y[:, t * LANES:(t + 1) * LANES]
    for t in range(S5_T):
        for b in range(nb):
            y_ref[pl.ds(b * seq_len + t, nchunk, stride=S5_T), :] = up_scr[t, pl.ds(b, nchunk, stride=nb), :]


def _s5(u, mats_l, h0, bsz, seq_len, nb, emit_state):
    nchunk = seq_len // S5_T
    r = nchunk * nb
    has_h0 = h0 is not None
    qi = lambda q, j: (q, 0, 0)
    wspec = pl.BlockSpec((1, S5_K, S5_K), qi)
    in_specs = [pl.BlockSpec((nb * seq_len, LANES), lambda q, j: (j, q))] + [wspec] * 5 \
        + [pl.BlockSpec((1, 1, S5_K), qi)] * 2
    args = [u] + list(mats_l)
    st_spec = pl.BlockSpec((1, 2, 1, nb, S5_K), lambda q, j: (q, 0, j, 0, 0))
    st_shape = (S5_SG, 2, bsz // nb, nb, S5_K)
    if has_h0:
        in_specs.append(st_spec)
        args.append(h0.reshape(st_shape))
    out_specs = [pl.BlockSpec((nb * seq_len, LANES), lambda q, j: (j, q))]
    out_shape = [jax.ShapeDtypeStruct((bsz * seq_len, B_WIDTH), F32)]
    if emit_state:
        out_specs.append(st_spec)
        out_shape.append(jax.ShapeDtypeStruct(st_shape, F32))
    outs = pl.pallas_call(
        functools.partial(_s5_kernel, nchunk=nchunk, nb=nb, seq_len=seq_len, has_h0=has_h0, emit_state=emit_state),
        grid=(S5_SG, bsz // nb),
        in_specs=in_specs,
        out_specs=out_specs,
        out_shape=out_shape,
        scratch_shapes=[pltpu.VMEM((S5_T, r, LANES), F32), pltpu.VMEM((r, S5_K), F32), pltpu.VMEM((r, S5_K), F32)],
        compiler_params=_cp(("parallel", "parallel")),
        name="s5",
    )(*args)
    if emit_state:
        return outs[0], outs[1].reshape(S5_SG, 2, bsz, S5_K)
    return (outs[0],)


def _merge_kernel(x_ref, shift_ref, scale_ref, gate_ref, gnorm_ref, of_ref, ob_ref, zs_ref, gout_ref, ys_ref, u_ref,
                  oc_ref, wg_ref, wba_ref, wbb_ref, wbc_ref, wout_ref, wglu_ref, bglu_ref, dskip_ref, xo_ref):
    x = x_ref[...]
    h = ((_rms(x) * gnorm_ref[...]) * (1.0 + scale_ref[0]) + shift_ref[0]).astype(BF16)
    od = of_ref[...] + ob_ref[...]
    gout = gout_ref[...]
    oa = jnp.concatenate([_rms(od[:, hh * HEAD_DIM_A:(hh + 1) * HEAD_DIM_A]) * gout for hh in range(H_A)], axis=1)
    oa = oa * zs_ref[...]
    yb = jax.nn.gelu(ys_ref[...] + dskip_ref[...] * u_ref[...])
    ob = yb * jax.nn.sigmoid(_mm(yb, wglu_ref[...]) + bglu_ref[...])
    acc = None
    for nbr, (o, w) in enumerate(((oa, wba_ref), (ob, wbb_ref), (oc_ref[...], wbc_ref))):
        gate = jax.nn.sigmoid(jnp.dot(h, wg_ref[:, nbr * D_MODEL:(nbr + 1) * D_MODEL], preferred_element_type=F32))
        term = gate * _mm(o, w[...])
        acc = term if acc is None else acc + term
    out = _mm(acc, wout_ref[...])
    xo_ref[...] = x + gate_ref[0] * out


def _merge(x, mods_rows, row_of_tile, lp, o_f, o_b, zs, ys, u, oc, tl):
    n = x.shape[0]
    const2 = lambda i: (0, 0)
    tok = lambda w: pl.BlockSpec((tl, w), lambda i: (i, 0))
    mod = lambda j: pl.BlockSpec((1, 1, D_MODEL), lambda i: (row_of_tile(i) * 6 + j, 0, 0))
    return pl.pallas_call(
        _merge_kernel,
        grid=(n // tl,),
        in_specs=[tok(D_MODEL), mod(0), mod(1), mod(2), pl.BlockSpec((1, D_MODEL), const2),
                  tok(A_WIDTH), tok(A_WIDTH), tok(A_WIDTH), pl.BlockSpec((1, HEAD_DIM_A), const2),
                  tok(B_WIDTH), tok(B_WIDTH), tok(MLA_W),
                  pl.BlockSpec((D_MODEL, N_BRANCH * D_MODEL), const2),
                  pl.BlockSpec((BRANCH_WIDTH, D_MODEL), const2), pl.BlockSpec((BRANCH_WIDTH, D_MODEL), const2),
                  pl.BlockSpec((MLA_W, D_MODEL), const2), pl.BlockSpec((D_MODEL, D_MODEL), const2),
                  pl.BlockSpec((B_WIDTH, B_WIDTH), const2), pl.BlockSpec((1, B_WIDTH), const2),
                  pl.BlockSpec((1, B_WIDTH), const2)],
        out_specs=tok(D_MODEL),
        out_shape=jax.ShapeDtypeStruct((n, D_MODEL), F32),
        compiler_params=_cp(("parallel",)),
        name="merge",
    )(x, mods_rows, mods_rows, mods_rows, lp['g_norm_mix'], o_f, o_b, zs, lp['g_delta_out'], ys, u, oc,
      lp['w_gates'], lp['w_br_a'], lp['w_br_b'],
      lp['w_br_c_pad'], lp['w_out'], lp['w_glu'], lp['b_glu'], lp['s5_d'])


FF_BLK = 256


def _ffn_kernel(*refs, tl, tiles_per_seq, final):
    it = iter(refs)
    xp_ref, xc_ref, xn_ref = next(it), next(it), next(it)
    shift_ref, scale_ref, gate_ref, gnorm_ref = next(it), next(it), next(it), next(it)
    wup_ref, convw_ref, convb_ref, wdown_ref = next(it), next(it), next(it), next(it)
    gfin_ref = next(it) if final else None
    xo_ref = next(it)
    yo_ref = next(it) if final else None
    act_scr = next(it)

    i = pl.program_id(0)
    pos = i % tiles_per_seq
    mod_scale = 1.0 + scale_ref[0]
    mod_shift = shift_ref[0]
    gain = gnorm_ref[...]

    def norm_mod(x):
        return ((_rms(x) * gain) * mod_scale + mod_shift).astype(BF16)

    x = xc_ref[...]
    h_ext = jnp.concatenate([norm_mod(xp_ref[...]), norm_mod(x), norm_mod(xn_ref[...])], axis=0)
    n_ext = tl + 2 * HALO
    rowi = lax.broadcasted_iota(jnp.int32, (n_ext, 1), 0)
    valid = jnp.logical_and(jnp.logical_or(pos > 0, rowi >= HALO),
                            jnp.logical_or(pos < tiles_per_seq - 1, rowi < HALO + tl)).astype(F32)
    pad = FFN_CONV // 2

    def conv_act(cols):
        up = jnp.dot(h_ext, wup_ref[:, cols], preferred_element_type=F32) * valid
        acc = None
        for t in range(FFN_CONV):
            sh = (pad - t) % n_ext
            src = up if sh == 0 else pltpu.roll(up, sh, 0)
            term = src[HALO:HALO + tl, :] * convw_ref[t:t + 1, cols]
            acc = term if acc is None else acc + term
        return acc + convb_ref[:, cols]

    for j in range(D_FF // FF_BLK):
        gcols = slice(j * FF_BLK, (j + 1) * FF_BLK)
        vcols = slice(D_FF + j * FF_BLK, D_FF + (j + 1) * FF_BLK)
        act_scr[:, gcols] = (_silu(conv_act(gcols)) * conv_act(vcols)).astype(BF16)
    out = jnp.dot(act_scr[...], wdown_ref[...], preferred_element_type=F32)
    xo = x + gate_ref[0] * out
    xo_ref[...] = xo
    if final:
        yo_ref[...] = _rms(xo) * gfin_ref[...]


def _ffn(x, mods_rows, row_of_tile, lp, seq_len, tl, g_final):
    n = x.shape[0]
    tps = seq_len // tl
    final = g_final is not None
    xprev, xnext = _halo_maps(n, tl)
    const2 = lambda i: (0, 0)
    mod = lambda j: pl.BlockSpec((1, 1, D_MODEL), lambda i: (row_of_tile(i) * 6 + j, 0, 0))
    tok = pl.BlockSpec((tl, D_MODEL), lambda i: (i, 0))
    in_specs = [pl.BlockSpec((HALO, D_MODEL), xprev), tok, pl.BlockSpec((HALO, D_MODEL), xnext),
                mod(3), mod(4), mod(5), pl.BlockSpec((1, D_MODEL), const2),
                pl.BlockSpec((D_MODEL, 2 * D_FF), const2), pl.BlockSpec((FFN_CONV, 2 * D_FF), const2),
                pl.BlockSpec((1, 2 * D_FF), const2), pl.BlockSpec((D_FF, D_MODEL), const2)]
    args = [x, x, x, mods_rows, mods_rows, mods_rows, lp['g_norm_ffn'], lp['w_ffn_up'], lp['conv_ffn'],
            lp['b_conv_ffn'], lp['w_ffn_down']]
    out_specs = [tok]
    out_shape = [jax.ShapeDtypeStruct((n, D_MODEL), F32)]
    if final:
        in_specs.append(pl.BlockSpec((1, D_MODEL), const2))
        args.append(g_final)
        out_specs.append(tok)
        out_shape.append(jax.ShapeDtypeStruct((n, D_MODEL), F32))
    return pl.pallas_call(
        functools.partial(_ffn_kernel, tl=tl, tiles_per_seq=tps, final=final),
        grid=(n // tl,),
        in_specs=in_specs,
        out_specs=out_specs,
        out_shape=out_shape,
        scratch_shapes=[pltpu.VMEM((tl, D_FF), BF16)],
        compiler_params=_cp(("parallel",)),
        name="ffn",
    )(*args)


def _pad_heads(w, head_w, n_heads):
    k = w.shape[0]
    w = w.reshape(k, n_heads, head_w)
    w = jnp.pad(w, ((0, 0), (0, 0), (0, HEAD_PAD - head_w)))
    return w.reshape(k, n_heads * HEAD_PAD)


def _layer_params(l, p):
    w_in = p['w_in'][l]
    o = 0
    parts = {}
    for name, wd in (('qkv', 3 * A_WIDTH), ('z', A_WIDTH), ('beta', 2 * H_A), ('alpha', 2 * H_A), ('u', B_WIDTH),
                     ('qa', Q_LORA), ('kva', KV_LORA), ('kr', QK_ROPE), ('gates', N_BRANCH * D_MODEL)):
        parts[name] = w_in[:, o:o + wd]
        o += wd
    small = jnp.concatenate([parts['beta'], parts['alpha'], parts['kr'],
                             jnp.zeros((D_MODEL, LANES - 4 * H_A - QK_ROPE), F32)], axis=1)
    w_rest = jnp.concatenate([parts['z'], parts['u'], parts['qa'], parts['kva'], small], axis=1)
    pad8 = lambda v: jnp.pad(v.reshape(1, 2 * H_A), ((0, 0), (2 * H_A, LANES - 4 * H_A)))
    w_kv = p['w_kv_b'][l].reshape(KV_LORA, H_C, QK_NOPE + V_HEAD)
    w_k = w_kv[:, :, :QK_NOPE].reshape(KV_LORA, H_C * QK_NOPE)
    w_v = w_kv[:, :, QK_NOPE:].reshape(KV_LORA, H_C * V_HEAD)
    w_br_c = p['w_branch'][l, 2].reshape(H_C, V_HEAD, D_MODEL)
    w_br_c = jnp.pad(w_br_c, ((0, 0), (0, HEAD_PAD - V_HEAD), (0, 0))).reshape(MLA_W, D_MODEL)
    row = lambda v: v.reshape(1, -1)
    return {
        'g_norm_mix': row(p['g_norm_mix'][l]), 'g_norm_ffn': row(p['g_norm_ffn'][l]),
        'w_qkv': parts['qkv'].astype(BF16), 'w_rest': w_rest.astype(BF16), 'w_gates': parts['gates'].astype(BF16),
        'conv_qkv': p['conv_qkv'][l],
        'a_log128': pad8(p['a_log'][l]), 'dt_bias128': pad8(p['dt_bias'][l]),
        'g_delta_out': row(p['g_delta_out'][l]),
        'g_q_a': row(p['g_q_a'][l]), 'g_kv_a': row(p['g_kv_a'][l]),
        'w_q_pad': _pad_heads(p['w_q_b'][l], QK_NOPE + QK_ROPE, H_C).astype(BF16),
        'w_k_pad': _pad_heads(w_k, QK_NOPE, H_C).astype(BF16),
        'w_v_pad': _pad_heads(w_v, V_HEAD, H_C).astype(BF16),
        'w_br_a': p['w_branch'][l, 0].astype(BF16), 'w_br_b': p['w_branch'][l, 1].astype(BF16),
        'w_br_c_pad': w_br_c.astype(BF16),
        'w_out': p['w_out'][l].astype(BF16),
        'w_glu': p['w_glu'][l].astype(BF16), 'b_glu': row(p['b_glu'][l]), 's5_d': row(p['s5_d'][l]),
        'w_ffn_up': p['w_ffn_up'][l].astype(BF16), 'conv_ffn': p['conv_ffn'][l],
        'b_conv_ffn': row(p['b_conv_ffn'][l]), 'w_ffn_down': p['w_ffn_down'][l].astype(BF16),
    }


def _rope_tables(length):
    rows = length // GRID_W
    row = jnp.repeat(jnp.arange(rows, dtype=F32), GRID_W)
    col = (jnp.arange(length) % GRID_W).astype(F32)
    n_freq = QK_ROPE // 4
    inv_freq = 1.0 / (ROPE_BASE ** (jnp.arange(n_freq, dtype=F32) / n_freq))
    ang = jnp.concatenate([row[:, None] * inv_freq, col[:, None] * inv_freq], axis=-1)
    cos, sin = jnp.cos(ang), jnp.sin(ang)
    ones = jnp.ones((length, QK_NOPE), F32)
    zeros = jnp.zeros((length, QK_NOPE), F32)
    tail = jnp.zeros((length, HEAD_PAD - QK_NOPE - QK_ROPE), F32)
    cos_t = jnp.concatenate([ones, cos, cos, tail], axis=1)
    sin_t = jnp.concatenate([zeros, -sin, sin, tail], axis=1)
    return cos_t, sin_t


def _layer(x, bsz, seq_len, mods_rows, row_of_tile, lp, s5mats, ctx, rope_tabs, tl, g_final):
    is_ctx = ctx is None
    outs = _inproj(x, mods_rows, row_of_tile, lp, seq_len, tl, rope_tabs, emit_ctx=is_ctx)
    if is_ctx:
        qkv, zs, u, bg, ckv, kr, qh, kh, vh = outs
    else:
        qkv, zs, u, bg, qh, kh, vh = outs

    nc = seq_len // CHUNK
    bgt = bg[:, :16].reshape(bsz, nc, CHUNK, 16).transpose(0, 1, 3, 2)
    prep = _delta_prep(qkv, bg, bgt, bsz, seq_len)
    d_out = _delta_recur(prep, None if is_ctx else ctx['s0'], bsz, seq_len, min(bsz, 4), emit_state=is_ctx)
    o_f = d_out[0].reshape(bsz * seq_len, A_WIDTH)
    o_b = d_out[1].reshape(bsz * seq_len, A_WIDTH)

    s5_nb = max(1, min(bsz, (1024 * S5_T) // seq_len))
    s_out = _s5(u, s5mats, None if is_ctx else ctx['h0'], bsz, seq_len, s5_nb, emit_state=is_ctx)
    ys = s_out[0]

    segs = [(kh, vh, seq_len)]
    if not is_ctx:
        segs = [(ctx['kh'], ctx['vh'], ctx['past'])] + segs
    oc = _attention(qh, segs, bsz, seq_len, min(seq_len, 512))

    x = _merge(x, mods_rows, row_of_tile, lp, o_f, o_b, zs, ys, u, oc, tl)
    f_out = _ffn(x, mods_rows, row_of_tile, lp, seq_len, tl, g_final)
    extras = None
    if is_ctx:
        hfin = s_out[1]
        hfin = hfin.reshape(S5_SG, 2, bsz, 2, S5_SG_GROUPS, S5_STATE).transpose(3, 2, 1, 0, 4, 5)
        hfin = hfin.reshape(2, bsz, 2, S5_GROUPS, S5_STATE)
        extras = (d_out[2], hfin[0], hfin[1], ckv.reshape(bsz, seq_len, KV_LORA), kr.reshape(bsz, seq_len, QK_ROPE))
    return f_out, extras


def kernel(x_prompt, x_sample, state_delta, state_s5_re, state_s5_im, cache_ckv, cache_krope, c, c_ctx, w_mod, b_mod, g_norm_mix, g_norm_ffn, w_in, conv_qkv, a_log, dt_bias, g_delta_out, s5_lam_re, s5_lam_im, s5_log_dt, s5_b_re, s5_b_im, s5_c_re, s5_c_im, s5_d, w_glu, b_glu, g_q_a, w_q_b, g_kv_a, w_kv_b, w_branch, w_out, w_ffn_up, conv_ffn, b_conv_ffn, w_ffn_down, g_final):
    p = dict(g_norm_mix=g_norm_mix, g_norm_ffn=g_norm_ffn, w_in=w_in, conv_qkv=conv_qkv, a_log=a_log,
             dt_bias=dt_bias, g_delta_out=g_delta_out, s5_d=s5_d, w_glu=w_glu, b_glu=b_glu, g_q_a=g_q_a,
             w_q_b=w_q_b, g_kv_a=g_kv_a, w_kv_b=w_kv_b, w_branch=w_branch, w_out=w_out, w_ffn_up=w_ffn_up,
             conv_ffn=conv_ffn, b_conv_ffn=b_conv_ffn, w_ffn_down=w_ffn_down)
    bp, lp_len, _ = x_prompt.shape
    bs, ls_len, _ = x_sample.shape
    past = cache_ckv.shape[2]
    depth = w_in.shape[0]

    mod_rows = 16
    cvec = jnp.concatenate([c_ctx[None, :], c, jnp.zeros((mod_rows - 1 - bs, D_MODEL), F32)], axis=0).astype(F32)
    mods = _modulation(cvec, w_mod, b_mod)

    g = S5_GROUPS
    b_re_t = s5_b_re.reshape(depth * g, S5_STATE, S5_GROUP).transpose(0, 2, 1)
    b_im_t = s5_b_im.reshape(depth * g, S5_STATE, S5_GROUP).transpose(0, 2, 1)
    s5f = _s5prep(s5_lam_re, s5_lam_im, s5_log_dt, b_re_t, b_im_t, s5_c_re, s5_c_im, 0)
    s5b = _s5prep(s5_lam_re, s5_lam_im, s5_log_dt, b_re_t, b_im_t, s5_c_re, s5_c_im, 1)
    s5_m = (s5f[0] + s5b[0]).astype(BF16)

    rope_tabs = _rope_tables(ls_len)
    tl_p = min(lp_len, 512)
    tl_s = min(ls_len, 512)
    tiles_per_seq_s = ls_len // tl_s

    def sg_state(re, im):
        t = jnp.stack([re, im], axis=0).astype(F32).reshape(2, bs, 2, S5_SG, S5_SG_GROUPS, S5_STATE)
        return t.transpose(3, 2, 1, 0, 4, 5).reshape(S5_SG, 2, bs, S5_K)

    xp = x_prompt.astype(F32).reshape(bp * lp_len, D_MODEL)
    xs = x_sample.astype(F32).reshape(bs * ls_len, D_MODEL)
    deltas, s5_res, s5_ims, ckvs, kropes = [], [], [], [], []
    yp = ys = None
    for l in range(depth):
        lp = _layer_params(l, p)
        mods_rows = mods[l].reshape(mod_rows * 6, 1, D_MODEL)
        sl = slice(l * S5_SG, (l + 1) * S5_SG)
        s5mats = (s5_m[sl], s5f[1][sl], s5b[1][sl], s5f[2][sl], s5b[2][sl], s5f[3][sl], s5b[3][sl])
        last = l == depth - 1
        gfin = g_final.reshape(1, D_MODEL) if last else None

        out_p, extras = _layer(xp, bp, lp_len, mods_rows, lambda i: 0, lp, s5mats, None, None, tl_p, gfin)
        deltas.append(extras[0])
        s5_res.append(extras[1])
        s5_ims.append(extras[2])
        ckvs.append(extras[3])
        kropes.append(extras[4])

        kh_c, vh_c = _kvcache(cache_ckv[:, l].astype(F32).reshape(bs * past, KV_LORA),
                              cache_krope[:, l].astype(F32).reshape(bs * past, QK_ROPE), lp, past)
        ctx = dict(s0=state_delta[:, l].astype(F32), h0=sg_state(state_s5_re[:, l], state_s5_im[:, l]),
                   kh=kh_c, vh=vh_c, past=past)
        out_s, _ = _layer(xs, bs, ls_len, mods_rows, lambda i: 1 + i // tiles_per_seq_s, lp, s5mats, ctx, rope_tabs,
                          tl_s, gfin)
        xp, xs = out_p[0], out_s[0]
        if last:
            yp, ys = out_p[1], out_s[1]

    y_prompt = yp.reshape(bp, lp_len, D_MODEL)
    y_sample = ys.reshape(bs, ls_len, D_MODEL)
    return (y_prompt, y_sample, jnp.stack(deltas, axis=1), jnp.stack(s5_res, axis=1), jnp.stack(s5_ims, axis=1),
            jnp.stack(ckvs, axis=1), jnp.stack(kropes, axis=1))
```

```python
import functools
import math

import jax
import jax.numpy as jnp
from jax import lax
from jax.experimental import pallas as pl
from jax.experimental.pallas import tpu as pltpu

F32 = jnp.float32
BF16 = jnp.bfloat16

D_MODEL = 1024
DEPTH = 4
GRID_W = 64
H_A = 4
HEAD_DIM_A = 128
A_WIDTH = H_A * HEAD_DIM_A
SHORT_CONV = 5
CHUNK = 64
S5_GROUP = 16
S5_STATE = 64
B_WIDTH = 512
S5_GROUPS = B_WIDTH // S5_GROUP
H_C = 8
QK_NOPE = 64
QK_ROPE = 32
V_HEAD = 64
Q_LORA = 384
KV_LORA = 256
ROPE_BASE = 10000.0
N_BRANCH = 3
BRANCH_WIDTH = 512
D_FF = 2816
FFN_CONV = 3
NORM_EPS = 1e-6

LANES = 128
HEAD_PAD = LANES
SUB = 16
S5_T = 8
S5_ROWS = S5_T * S5_GROUP
S5_SG_GROUPS = LANES // S5_GROUP
S5_SG = S5_GROUPS // S5_SG_GROUPS
S5_K = S5_T * LANES

VMEM_LIMIT = 56 * 1024 * 1024
HALO = 16


def _cp(sem):
    return pltpu.CompilerParams(dimension_semantics=sem, vmem_limit_bytes=VMEM_LIMIT)


def _mm(a, b):
    return jnp.dot(a.astype(BF16), b.astype(BF16), preferred_element_type=F32)


def _bmm(a, b):
    return lax.dot_general(a.astype(BF16), b.astype(BF16), (((2,), (1,)), ((0,), (0,))),
                           preferred_element_type=F32)


def _bmm_nt(a, b):
    return lax.dot_general(a.astype(BF16), b.astype(BF16), (((2,), (2,)), ((0,), (0,))),
                           preferred_element_type=F32)


def _bmm_tn(a, b):
    return lax.dot_general(a.astype(BF16), b.astype(BF16), (((1,), (1,)), ((0,), (0,))),
                           preferred_element_type=F32)


def _split3(x):
    x1 = x.astype(BF16)
    r = x - x1.astype(F32)
    x2 = r.astype(BF16)
    x3 = (r - x2.astype(F32)).astype(BF16)
    return x1, x2, x3


def _mm_exact_rhs(a, b_bf16):
    a1, a2, a3 = _split3(a)
    d = lambda t: jnp.dot(t, b_bf16, preferred_element_type=F32)
    return d(a1) + d(a2) + d(a3)


def _mm_exact_lhs(a_bf16, b):
    b1, b2, b3 = _split3(b)
    d = lambda t: jnp.dot(a_bf16, t, preferred_element_type=F32)
    return d(b1) + d(b2) + d(b3)


def _mm3_nt(a, b):
    a1 = a.astype(BF16)
    a2 = (a - a1.astype(F32)).astype(BF16)
    b1 = b.astype(BF16)
    b2 = (b - b1.astype(F32)).astype(BF16)
    d = lambda s, t: lax.dot_general(s, t, (((1,), (1,)), ((), ())), preferred_element_type=F32)
    return d(a1, b1) + d(a1, b2) + d(a2, b1)


def _silu(x):
    return x * jax.nn.sigmoid(x)


def _rms(x):
    return x * lax.rsqrt(jnp.mean(x * x, axis=-1, keepdims=True) + NORM_EPS)


def _mod_kernel(c_ref, w_ref, b_ref, o_ref):
    s = _silu(c_ref[...])
    o_ref[0] = _mm(s, w_ref[0]) + b_ref[0]


def _modulation(cvec, w_mod, b_mod):
    rows = cvec.shape[0]
    nblk = w_mod.shape[-1] // D_MODEL
    return pl.pallas_call(
        _mod_kernel,
        grid=(DEPTH, nblk),
        in_specs=[
            pl.BlockSpec((rows, D_MODEL), lambda l, j: (0, 0)),
            pl.BlockSpec((1, D_MODEL, D_MODEL), lambda l, j: (l, 0, j)),
            pl.BlockSpec((1, 1, D_MODEL), lambda l, j: (l, 0, j)),
        ],
        out_specs=pl.BlockSpec((1, rows, D_MODEL), lambda l, j: (l, 0, j)),
        out_shape=jax.ShapeDtypeStruct((DEPTH, rows, w_mod.shape[-1]), F32),
        compiler_params=_cp(("parallel", "parallel")),
        name="modulation",
    )(cvec, w_mod, b_mod.reshape(DEPTH, 1, -1))


REST_W = 512 + 512 + Q_LORA + KV_LORA + LANES
OFF_Z, OFF_U, OFF_QA, OFF_KVA, OFF_SM = 0, 512, 1024, 1024 + Q_LORA, 1024 + Q_LORA + KV_LORA
SM_KR = 16
MLA_W = H_C * HEAD_PAD


def _rope_apply(x, cos, sin):
    lane = lax.broadcasted_iota(jnp.int32, x.shape, 1)
    partner = jnp.where(lane < QK_NOPE + QK_ROPE // 2,
                        pltpu.roll(x, HEAD_PAD - QK_ROPE // 2, 1),
                        pltpu.roll(x, QK_ROPE // 2, 1))
    return x * cos + partner * sin


def _inproj_kernel(*refs, tl, tiles_per_seq, rope, emit_ctx):
    it = iter(refs)
    xp_ref, xc_ref, xn_ref = next(it), next(it), next(it)
    shift_ref, scale_ref, gnorm_ref = next(it), next(it), next(it)
    wqkv_ref, convw_ref, wrest_ref = next(it), next(it), next(it)
    gqa_ref, gkva_ref, alog_ref, dtb_ref = next(it), next(it), next(it), next(it)
    wq_ref, wk_ref, wv_ref = next(it), next(it), next(it)
    cos_ref = sin_ref = None
    if rope:
        cos_ref, sin_ref = next(it), next(it)
    qkv_out, zs_out, u_out, bg_out = next(it), next(it), next(it), next(it)
    ckv_out = kr_out = None
    if emit_ctx:
        ckv_out, kr_out = next(it), next(it)
    qh_out, kh_out, vh_out = next(it), next(it), next(it)
    qkv_scr = next(it)

    i = pl.program_id(0)
    pos = i % tiles_per_seq
    mod_scale = 1.0 + scale_ref[0]
    mod_shift = shift_ref[0]
    gain = gnorm_ref[...]

    def norm_mod(x):
        return ((_rms(x) * gain) * mod_scale + mod_shift).astype(BF16)

    h_cur = norm_mod(xc_ref[...])
    h_ext = jnp.concatenate([norm_mod(xp_ref[...]), h_cur, norm_mod(xn_ref[...])], axis=0)

    n_ext = tl + 2 * HALO
    rowi = lax.broadcasted_iota(jnp.int32, (n_ext, 1), 0)
    valid = jnp.logical_and(jnp.logical_or(pos > 0, rowi >= HALO),
                            jnp.logical_or(pos < tiles_per_seq - 1, rowi < HALO + tl)).astype(F32)
    qkv_scr[...] = _mm(h_ext, wqkv_ref[...]) * valid
    pad = SHORT_CONV // 2
    for blk in range(3):
        cols = slice(blk * A_WIDTH, (blk + 1) * A_WIDTH)
        acc = None
        for t in range(SHORT_CONV):
            term = qkv_scr[HALO - pad + t:HALO - pad + t + tl, cols] * convw_ref[t:t + 1, cols]
            acc = term if acc is None else acc + term
        acc = _silu(acc)
        if blk < 2:
            parts = []
            for hh in range(H_A):
                a = acc[:, hh * HEAD_DIM_A:(hh + 1) * HEAD_DIM_A]
                parts.append(a * lax.rsqrt(jnp.sum(a * a, axis=-1, keepdims=True) + NORM_EPS))
            acc = jnp.concatenate(parts, axis=1)
        qkv_out[:, cols] = acc.astype(BF16)

    rest = _mm(h_cur, wrest_ref[...])
    zs_out[...] = _silu(rest[:, OFF_Z:OFF_Z + 512])
    u_out[...] = rest[:, OFF_U:OFF_U + 512]

    small = rest[:, OFF_SM:OFF_SM + LANES]
    lane = lax.broadcasted_iota(jnp.int32, small.shape, 1)
    beta = jax.nn.sigmoid(small)
    glog = -jnp.exp(alog_ref[...]) * jax.nn.softplus(small + dtb_ref[...])
    bg_out[...] = jnp.where(lane < 2 * H_A, beta, glog)

    qa = _rms(rest[:, OFF_QA:OFF_QA + Q_LORA]) * gqa_ref[...]
    ckv = _rms(rest[:, OFF_KVA:OFF_KVA + KV_LORA]) * gkva_ref[...]
    if emit_ctx:
        ckv_out[...] = ckv
        kr_out[...] = small[:, SM_KR:SM_KR + QK_ROPE]
    ckv_b = ckv.astype(BF16)
    qhat = _mm(qa, wq_ref[...])
    khat = _mm(ckv_b, wk_ref[...])
    vh_out[...] = _mm(ckv_b, wv_ref[...]).astype(BF16)
    kr_al = jnp.where((lane >= QK_NOPE) & (lane < QK_NOPE + QK_ROPE),
                      pltpu.roll(small, QK_NOPE - SM_KR, 1), 0.0)
    if rope:
        cos, sin = cos_ref[...], sin_ref[...]
        kr_al = _rope_apply(kr_al, cos, sin)
    qscale = (QK_NOPE + QK_ROPE) ** -0.5
    for hh in range(H_C):
        cs = slice(hh * HEAD_PAD, (hh + 1) * HEAD_PAD)
        qh = qhat[:, cs]
        if rope:
            qh = _rope_apply(qh, cos, sin)
        qh_out[:, cs] = (qh * qscale).astype(BF16)
        kh_out[:, cs] = (khat[:, cs] + kr_al).astype(BF16)


def _halo_maps(n, tl):
    nh = n // HALO

    def xprev(i):
        return (jnp.maximum(i * (tl // HALO) - 1, 0), 0)

    def xnext(i):
        return (jnp.minimum((i + 1) * (tl // HALO), nh - 1), 0)

    return xprev, xnext


def _inproj(x, mods_rows, row_of_tile, lp, seq_len, tl, rope_tabs, emit_ctx):
    n = x.shape[0]
    nt = n // tl
    tps = seq_len // tl
    rope = rope_tabs is not None
    xprev, xnext = _halo_maps(n, tl)
    const2 = lambda i: (0, 0)
    in_specs = [
        pl.BlockSpec((HALO, D_MODEL), xprev),
        pl.BlockSpec((tl, D_MODEL), lambda i: (i, 0)),
        pl.BlockSpec((HALO, D_MODEL), xnext),
        pl.BlockSpec((1, 1, D_MODEL), lambda i: (row_of_tile(i) * 6 + 0, 0, 0)),
        pl.BlockSpec((1, 1, D_MODEL), lambda i: (row_of_tile(i) * 6 + 1, 0, 0)),
        pl.BlockSpec((1, D_MODEL), const2),
        pl.BlockSpec((D_MODEL, 3 * A_WIDTH), const2),
        pl.BlockSpec((SHORT_CONV, 3 * A_WIDTH), const2),
        pl.BlockSpec((D_MODEL, REST_W), const2),
        pl.BlockSpec((1, Q_LORA), const2),
        pl.BlockSpec((1, KV_LORA), const2),
        pl.BlockSpec((1, LANES), const2),
        pl.BlockSpec((1, LANES), const2),
        pl.BlockSpec((Q_LORA, MLA_W), const2),
        pl.BlockSpec((KV_LORA, MLA_W), const2),
        pl.BlockSpec((KV_LORA, MLA_W), const2),
    ]
    args = [x, x, x, mods_rows, mods_rows, lp['g_norm_mix'], lp['w_qkv'], lp['conv_qkv'], lp['w_rest'],
            lp['g_q_a'], lp['g_kv_a'], lp['a_log128'], lp['dt_bias128'], lp['w_q_pad'], lp['w_k_pad'], lp['w_v_pad']]
    if rope:
        in_specs += [pl.BlockSpec((tl, HEAD_PAD), lambda i: (i % tps, 0))] * 2
        args += list(rope_tabs)
    tok = lambda w: pl.BlockSpec((tl, w), lambda i: (i, 0))
    out_specs = [tok(3 * A_WIDTH), tok(512), tok(512), tok(LANES)]
    out_shape = [jax.ShapeDtypeStruct((n, 3 * A_WIDTH), BF16), jax.ShapeDtypeStruct((n, 512), F32),
                 jax.ShapeDtypeStruct((n, 512), F32), jax.ShapeDtypeStruct((n, LANES), F32)]
    if emit_ctx:
        out_specs += [tok(KV_LORA), tok(QK_ROPE)]
        out_shape += [jax.ShapeDtypeStruct((n, KV_LORA), F32), jax.ShapeDtypeStruct((n, QK_ROPE), F32)]
    out_specs += [tok(MLA_W)] * 3
    out_shape += [jax.ShapeDtypeStruct((n, MLA_W), BF16)] * 3
    return pl.pallas_call(
        functools.partial(_inproj_kernel, tl=tl, tiles_per_seq=tps, rope=rope, emit_ctx=emit_ctx),
        grid=(nt,),
        in_specs=in_specs,
        out_specs=out_specs,
        out_shape=out_shape,
        scratch_shapes=[pltpu.VMEM((tl + 2 * HALO, 3 * A_WIDTH), F32)],
        compiler_params=_cp(("parallel",)),
        name="inproj",
    )(*args)


def _kvcache_kernel(ckv_ref, kr_ref, wk_ref, wv_ref, kh_out, vh_out):
    ckv_b = ckv_ref[...].astype(BF16)
    khat = _mm(ckv_b, wk_ref[...])
    vh_out[...] = _mm(ckv_b, wv_ref[...]).astype(BF16)
    kr_al = kr_ref[...]
    for hh in range(H_C):
        cs = slice(hh * HEAD_PAD, (hh + 1) * HEAD_PAD)
        kh_out[:, cs] = (khat[:, cs] + kr_al).astype(BF16)


def _kvcache(ckv, kr, lp, tl):
    n = ckv.shape[0]
    const2 = lambda i: (0, 0)
    kr = jnp.pad(kr, ((0, 0), (QK_NOPE, HEAD_PAD - QK_NOPE - QK_ROPE)))
    return pl.pallas_call(
        _kvcache_kernel,
        grid=(n // tl,),
        in_specs=[pl.BlockSpec((tl, KV_LORA), lambda i: (i, 0)), pl.BlockSpec((tl, HEAD_PAD), lambda i: (i, 0)),
                  pl.BlockSpec((KV_LORA, MLA_W), const2), pl.BlockSpec((KV_LORA, MLA_W), const2)],
        out_specs=[pl.BlockSpec((tl, MLA_W), lambda i: (i, 0))] * 2,
        out_shape=[jax.ShapeDtypeStruct((n, MLA_W), BF16)] * 2,
        compiler_params=_cp(("parallel",)),
        name="kvcache",
    )(ckv, kr, lp['w_k_pad'], lp['w_v_pad'])


ATT_SUB = 256


def _attn_kernel(*refs, nseg, tq, hps):
    q_ref = refs[0]
    k_refs = refs[1:1 + nseg]
    v_refs = refs[1 + nseg:1 + 2 * nseg]
    o_ref = refs[1 + 2 * nseg]
    for hh in range(hps):
        cs = slice(hh * HEAD_PAD, (hh + 1) * HEAD_PAD)
        for sb in range(tq // ATT_SUB):
            rows = slice(sb * ATT_SUB, (sb + 1) * ATT_SUB)
            q = q_ref[rows, cs]
            s = [lax.dot_general(q, k[:, cs], (((1,), (1,)), ((), ())), preferred_element_type=F32) for k in k_refs]
            m = s[0].max(axis=-1, keepdims=True)
            for t in s[1:]:
                m = jnp.maximum(m, t.max(axis=-1, keepdims=True))
            p = [jnp.exp(t - m) for t in s]
            den = p[0].sum(axis=-1, keepdims=True)
            for t in p[1:]:
                den = den + t.sum(axis=-1, keepdims=True)
            acc = None
            for t, v in zip(p, v_refs):
                part = jnp.dot(t.astype(BF16), v[:, cs], preferred_element_type=F32)
                acc = part if acc is None else acc + part
            o_ref[rows, cs] = (acc / den).astype(BF16)


def _attention(qh, segs, bsz, seq_len, tq, hps):
    nq = seq_len // tq
    wblk = hps * HEAD_PAD
    in_specs = [pl.BlockSpec((tq, wblk), lambda b, h, i: (b * nq + i, h))]
    args = [qh]
    for which in (0, 1):
        for seg in segs:
            in_specs.append(pl.BlockSpec((seg[2], wblk), lambda b, h, i: (b, h)))
            args.append(seg[which])
    return pl.pallas_call(
        functools.partial(_attn_kernel, nseg=len(segs), tq=tq, hps=hps),
        grid=(bsz, H_C // hps, nq),
        in_specs=in_specs,
        out_specs=pl.BlockSpec((tq, wblk), lambda b, h, i: (b * nq + i, h)),
        out_shape=jax.ShapeDtypeStruct((bsz * seq_len, MLA_W), BF16),
        compiler_params=_cp(("parallel", "parallel", "parallel")),
        name="attention",
    )(*args)


DELTA_TB = 256
DELTA_G = DELTA_TB // CHUNK


def _delta_prep_kernel(qkv_ref, bg_ref, uf_ref, wf_ref, qef_ref, kef_ref, qkf_ref,
                       ub_ref, wb_ref, qeb_ref, keb_ref, qkb_ref, dec_ref):
    c = CHUNK
    outs = ((uf_ref, wf_ref, qef_ref, kef_ref, qkf_ref), (ub_ref, wb_ref, qeb_ref, keb_ref, qkb_ref))
    ri = lax.broadcasted_iota(jnp.int32, (c, c), 0)
    ci = lax.broadcasted_iota(jnp.int32, (c, c), 1)
    tril = (ri >= ci).astype(BF16)
    triu = (ri <= ci).astype(BF16)
    same_blk = (ri // SUB) == (ci // SUB)
    scale = HEAD_DIM_A ** -0.5

    insts = [(g, d, hh) for g in range(DELTA_G) for d in range(2) for hh in range(H_A)]
    n = len(insts)
    qkv = qkv_ref[...]
    bg = bg_ref[...]

    def head_stack(base):
        return jnp.stack([qkv[g * c:(g + 1) * c, base + hh * HEAD_DIM_A:base + (hh + 1) * HEAD_DIM_A]
                          for g, d, hh in insts])

    q, k, v = head_stack(0), head_stack(A_WIDTH), head_stack(2 * A_WIDTH)

    per_slab = LANES // c
    bg_t = [bg[s * LANES:(s + 1) * LANES, :].T for s in range(DELTA_TB // LANES)]
    cols, rws = {}, {}
    for g in range(DELTA_G):
        bgc = bg[g * c:(g + 1) * c, :]
        bgr = bg_t[g // per_slab][0:16, (g % per_slab) * c:(g % per_slab + 1) * c]
        cols[g] = (bgc, _mm_exact_lhs(tril, bgc), _mm_exact_lhs(triu, bgc))
        rws[g] = (_mm_exact_rhs(bgr, triu), _mm_exact_rhs(bgr, tril))
    beta = jnp.stack([cols[g][0][:, d * H_A + hh:d * H_A + hh + 1] for g, d, hh in insts])
    gcc = jnp.stack([cols[g][1 + d][:, 2 * H_A + d * H_A + hh:2 * H_A + d * H_A + hh + 1]
                     for g, d, hh in insts])
    gcr = jnp.stack([rws[g][d][2 * H_A + d * H_A + hh:2 * H_A + d * H_A + hh + 1, :] for g, d, hh in insts])
    glast = jnp.stack([gcc[i, (c - 1 if insts[i][1] == 0 else 0):(c if insts[i][1] == 0 else 1), :]
                       for i in range(n)])

    sign = jnp.stack([jnp.full((1, 1), 1 - 2 * d, jnp.int32) for g, d, hh in insts])
    tri = (ri - ci)[None] * sign
    incl = tri >= 0
    strict = tri > 0
    blk = jnp.broadcast_to(same_blk[None], incl.shape)

    kf = k.astype(F32)
    kb = kf * beta
    decay = jnp.where(incl, jnp.exp(jnp.where(incl, gcc - gcr, 0.0)), 0.0)
    a = _bmm_nt(kb, k) * decay
    qk = _bmm_nt(q, k) * (decay * scale)
    dg = jnp.where(blk & strict, a, 0.0)
    lo = jnp.where(jnp.logical_not(blk) & strict, a, 0.0)
    m1 = _bmm(dg, dg)
    p = m1 - dg - _bmm(dg, m1)
    pw = m1
    for _ in range(int(math.log2(SUB)) - 2):
        pw = _bmm(pw, pw)
        p = p + pw + _bmm(p, pw)
    nm = lo + _bmm(p, lo)
    n2 = _bmm(nm, nm)
    t1 = n2 - nm - _bmm(nm, n2)
    tm = t1 + p + _bmm(t1, p)
    e_col = jnp.exp(gcc)
    rhs = jnp.concatenate([v.astype(F32) * beta, kb * e_col], axis=2)
    x = rhs + _bmm(tm, rhs)
    u, w = x[:, :, :HEAD_DIM_A], x[:, :, HEAD_DIM_A:]
    qe = q.astype(F32) * (e_col * scale)
    ke = kf * jnp.exp(glast - gcc)
    dec = jnp.exp(glast)
    zpad = jnp.zeros((c, HEAD_DIM_A - c), BF16)
    for i, (g, d, hh) in enumerate(insts):
        rows = slice(g * c, (g + 1) * c)
        cs = slice(hh * HEAD_DIM_A, (hh + 1) * HEAD_DIM_A)
        u_ref, w_ref, qe_ref, ke_ref, qk_ref = outs[d]
        u_ref[0, rows, cs] = u[i]
        w_ref[0, rows, cs] = w[i].astype(BF16)
        qe_ref[0, rows, cs] = qe[i].astype(BF16)
        ke_ref[0, rows, cs] = ke[i].astype(BF16)
        qk_ref[0, rows, cs] = jnp.concatenate([qk[i].astype(BF16), zpad], axis=1)
        dec_ref[0, g, d * H_A + hh:d * H_A + hh + 1, :] = jnp.broadcast_to(dec[i], (1, HEAD_DIM_A))


def _delta_prep(qkv, bg, bsz, seq_len):
    nblk = seq_len // DELTA_TB
    nc = seq_len // CHUNK
    tokf = jax.ShapeDtypeStruct((bsz, seq_len, A_WIDTH), F32)
    tokb = jax.ShapeDtypeStruct((bsz, seq_len, A_WIDTH), BF16)
    tspec = pl.BlockSpec((1, DELTA_TB, A_WIDTH), lambda b, j: (b, j, 0))
    return pl.pallas_call(
        _delta_prep_kernel,
        grid=(bsz, nblk),
        in_specs=[pl.BlockSpec((DELTA_TB, 3 * A_WIDTH), lambda b, j: (b * nblk + j, 0)),
                  pl.BlockSpec((DELTA_TB, LANES), lambda b, j: (b * nblk + j, 0))],
        out_specs=[tspec] * 10 + [pl.BlockSpec((1, DELTA_G, 2 * H_A, HEAD_DIM_A), lambda b, j: (b, j, 0, 0))],
        out_shape=[tokf, tokb, tokb, tokb, tokb] * 2 + [jax.ShapeDtypeStruct((bsz, nc, 2 * H_A, HEAD_DIM_A), F32)],
        compiler_params=_cp(("parallel", "parallel")),
        name="delta_prep",
    )(qkv, bg)


def _delta_recur_kernel(*refs, nb, nblk, has_s0, emit_state):
    it = iter(refs)
    fwd = [next(it) for _ in range(5)]
    bwd = [next(it) for _ in range(5)]
    decf_ref, decb_ref = next(it), next(it)
    s0_ref = next(it) if has_s0 else None
    of_ref, ob_ref = next(it), next(it)
    sfin_ref = next(it) if emit_state else None
    s_scr = next(it)
    j = pl.program_id(1)
    c = CHUNK

    @pl.when(j == 0)
    def _():
        if has_s0:
            s_scr[...] = s0_ref[...]
        else:
            s_scr[...] = jnp.zeros(s_scr.shape, F32)

    insts = [(bi, d, hh) for bi in range(nb) for d in range(2) for hh in range(H_A)]
    for step in range(DELTA_G):
        gsel = (step, DELTA_G - 1 - step)

        def stk(idx, width=HEAD_DIM_A):
            return jnp.stack([(fwd, bwd)[d][idx][bi, gsel[d] * c:(gsel[d] + 1) * c,
                                                 hh * HEAD_DIM_A:hh * HEAD_DIM_A + width]
                              for bi, d, hh in insts])

        u, w, qe, ke, qk = stk(0), stk(1), stk(2), stk(3), stk(4, c)
        dec = jnp.stack([(decf_ref, decb_ref)[d][bi, gsel[d], d * H_A + hh:d * H_A + hh + 1, :]
                         for bi, d, hh in insts])
        s_old = jnp.stack([s_scr[bi, d, hh] for bi, d, hh in insts])
        r1 = _bmm(jnp.concatenate([w, qe], axis=1), s_old)
        v_new = (u - r1[:, :c, :]).astype(BF16)
        o = r1[:, c:, :] + _bmm(qk, v_new)
        s_new = s_old * dec + _bmm_tn(ke, v_new)
        for i, (bi, d, hh) in enumerate(insts):
            s_scr[bi, d, hh] = s_new[i]
            (of_ref, ob_ref)[d][bi, gsel[d] * c:(gsel[d] + 1) * c, hh * HEAD_DIM_A:(hh + 1) * HEAD_DIM_A] = o[i]

    if emit_state:
        @pl.when(j == nblk - 1)
        def _():
            sfin_ref[...] = s_scr[...]


def _delta_recur(prep, s0, bsz, seq_len, nb, emit_state):
    nblk = seq_len // DELTA_TB
    has_s0 = s0 is not None
    fspec = pl.BlockSpec((nb, DELTA_TB, A_WIDTH), lambda b, j: (b, j, 0))
    bspec = pl.BlockSpec((nb, DELTA_TB, A_WIDTH), lambda b, j: (b, nblk - 1 - j, 0))
    dspec_f = pl.BlockSpec((nb, DELTA_G, 2 * H_A, HEAD_DIM_A), lambda b, j: (b, j, 0, 0))
    dspec_b = pl.BlockSpec((nb, DELTA_G, 2 * H_A, HEAD_DIM_A), lambda b, j: (b, nblk - 1 - j, 0, 0))
    st_spec = pl.BlockSpec((nb, 2, H_A, HEAD_DIM_A, HEAD_DIM_A), lambda b, j: (b, 0, 0, 0, 0))
    in_specs = [fspec] * 5 + [bspec] * 5 + [dspec_f, dspec_b]
    args = list(prep[:10]) + [prep[10], prep[10]]
    if has_s0:
        in_specs.append(st_spec)
        args.append(s0)
    out_specs = [fspec, bspec]
    out_shape = [jax.ShapeDtypeStruct((bsz, seq_len, A_WIDTH), F32)] * 2
    if emit_state:
        out_specs.append(st_spec)
        out_shape.append(jax.ShapeDtypeStruct((bsz, 2, H_A, HEAD_DIM_A, HEAD_DIM_A), F32))
    return pl.pallas_call(
        functools.partial(_delta_recur_kernel, nb=nb, nblk=nblk, has_s0=has_s0, emit_state=emit_state),
        grid=(bsz // nb, nblk),
        in_specs=in_specs,
        out_specs=out_specs,
        out_shape=out_shape,
        scratch_shapes=[pltpu.VMEM((nb, 2, H_A, HEAD_DIM_A, HEAD_DIM_A), F32)],
        compiler_params=_cp(("parallel", "arbitrary")),
        name="delta_recur",
    )(*args)


def _s5prep_kernel(lre_ref, lim_ref, ldt_ref, bre_ref, bim_ref, cre_ref, cim_ref,
                   m_out, ef_out, eb_out, ff_out, fb_out, lt_out, xcat_scr, qcat_scr, ft_scr):
    gs, p, cg, t8, half = S5_SG_GROUPS, S5_STATE, S5_GROUP, S5_T, S5_K // 2
    j = (lax.broadcasted_iota(jnp.int32, (S5_ROWS, 1), 0) // cg).astype(F32)
    tile = lambda m: jnp.concatenate([m] * t8, axis=0)
    one = jnp.ones((1, 1), F32)
    bd = {}
    for d in range(2):
        e_out = (ef_out, eb_out)[d]
        f_out = (ff_out, fb_out)[d]
        xcat_scr[...] = jnp.zeros(xcat_scr.shape, F32)
        qcat_scr[...] = jnp.zeros(qcat_scr.shape, F32)
        ft_scr[...] = jnp.zeros(ft_scr.shape, F32)
        e_out[0] = jnp.zeros((S5_K, S5_K), BF16)
        for gl in range(gs):
            lre, lim = lre_ref[0, d, 0, gl:gl + 1, :], lim_ref[0, d, 0, gl:gl + 1, :]
            dt = jnp.exp(ldt_ref[0, d, 0, gl:gl + 1, :])
            zr, zi = lre * dt, lim * dt

            def lam_pow(e):
                mag = jnp.exp(e * zr)
                return mag * jnp.cos(e * zi), mag * jnp.sin(e * zi)

            l1r, l1i = lam_pow(one)
            den = lre * lre + lim * lim
            nr, ni = l1r - 1.0, l1i
            cfr = (nr * lre + ni * lim) / den
            cfi = (ni * lre - nr * lim) / den
            bre, bim = bre_ref[0, gl], bim_ref[0, gl]
            bbr = cfr * bre - cfi * bim
            bbi = cfr * bim + cfi * bre
            cre, cim = cre_ref[0, gl], cim_ref[0, gl]
            c_r, c_i = tile(cre), tile(cim)
            b_r, b_i = tile(bbr), tile(bbi)
            grow = slice(gl * cg, (gl + 1) * cg)
            lre_c, lim_c = slice(gl * p, (gl + 1) * p), slice(half + gl * p, half + (gl + 1) * p)

            xcat_scr[grow, 2 * gl * p:(2 * gl + 1) * p] = bbr
            xcat_scr[grow, (2 * gl + 1) * p:(2 * gl + 2) * p] = bbi
            pr, pi = lam_pow(j)
            qr = pr * c_r - pi * c_i
            qi = pr * c_i + pi * c_r
            for jj in range(t8):
                qcat_scr[jj, grow, 2 * gl * p:(2 * gl + 1) * p] = qr[jj * cg:(jj + 1) * cg, :]
                qcat_scr[jj, grow, (2 * gl + 1) * p:(2 * gl + 2) * p] = -qi[jj * cg:(jj + 1) * cg, :]

            er, ei = lam_pow(j if d else (t8 - 1.0) - j)
            e_r = (er * b_r - ei * b_i).astype(BF16)
            e_i = (er * b_i + ei * b_r).astype(BF16)
            fr, fi = lam_pow((t8 - j) if d else j + 1.0)
            f_r = fr * c_r - fi * c_i
            f_i = -(fr * c_i + fi * c_r)
            for s in range(t8):
                rows = slice(s * LANES + gl * cg, s * LANES + (gl + 1) * cg)
                e_out[0, rows, lre_c] = e_r[s * cg:(s + 1) * cg, :]
                e_out[0, rows, lim_c] = e_i[s * cg:(s + 1) * cg, :]
                ft_scr[rows, lre_c] = f_r[s * cg:(s + 1) * cg, :]
                ft_scr[rows, lim_c] = f_i[s * cg:(s + 1) * cg, :]

            ltr, lti = lam_pow(one * float(t8))
            lt_out[0, d:d + 1, lre_c] = ltr
            lt_out[0, d:d + 1, lim_c] = lti
        f_out[0] = ft_scr[...].T.astype(BF16)
        xcat = xcat_scr[...]
        for jj in range(t8):
            bd[(d, jj)] = _mm3_nt(xcat, qcat_scr[jj])
    for s in range(t8):
        for t in range(t8):
            blk = bd[(0, t - s)] if t > s else (bd[(1, s - t)] if s > t else bd[(0, 0)] + bd[(1, 0)])
            m_out[0, s * LANES:(s + 1) * LANES, t * LANES:(t + 1) * LANES] = blk.astype(BF16)


def _s5prep(lam_re, lam_im, log_dt, b_re, b_im, c_re, c_im):
    gs, p, cg = S5_SG_GROUPS, S5_STATE, S5_GROUP
    nq = DEPTH * S5_SG
    lam_idx = lambda q: (q // S5_SG, 0, q % S5_SG, 0, 0)
    par_idx = lambda q: (q, 0, 0, 0)
    lam5 = lambda v, w: v.reshape(DEPTH, 2, S5_SG, gs, w)
    b_t = lambda v: v.reshape(nq, gs, p, cg).transpose(0, 1, 3, 2)
    par4 = lambda v: v.reshape(nq, gs, cg, p)
    mat = pl.BlockSpec((1, S5_K, S5_K), lambda q: (q, 0, 0))
    return pl.pallas_call(
        _s5prep_kernel,
        grid=(nq,),
        in_specs=[pl.BlockSpec((1, 2, 1, gs, p), lam_idx), pl.BlockSpec((1, 2, 1, gs, p), lam_idx),
                  pl.BlockSpec((1, 2, 1, gs, 1), lam_idx)] + [pl.BlockSpec((1, gs, cg, p), par_idx)] * 4,
        out_specs=[mat] * 5 + [pl.BlockSpec((1, 2, S5_K), lambda q: (q, 0, 0))],
        out_shape=[jax.ShapeDtypeStruct((nq, S5_K, S5_K), BF16)] * 5 + [jax.ShapeDtypeStruct((nq, 2, S5_K), F32)],
        scratch_shapes=[pltpu.VMEM((S5_ROWS, S5_K), F32), pltpu.VMEM((S5_T, S5_ROWS, S5_K), F32),
                        pltpu.VMEM((S5_K, S5_K), F32)],
        compiler_params=_cp(("parallel",)),
        name="s5prep",
    )(lam5(lam_re, p), lam5(lam_im, p), lam5(log_dt, 1), b_t(b_re), b_t(b_im), par4(c_re), par4(c_im))


S5_RT = 256


def _s5_kernel(*refs, nchunk, nb, seq_len, has_h0, emit_state):
    it = iter(refs)
    u_ref = next(it)
    m_ref, ef_ref, eb_ref, ff_ref, fb_ref, lt_ref = (next(it) for _ in range(6))
    h0_ref = next(it) if has_h0 else None
    y_ref = next(it)
    hfin_ref = next(it) if emit_state else None
    up_scr, sf_scr, sb_scr = next(it), next(it), next(it)

    r = nchunk * nb
    rtile = min(S5_RT, r)
    half = S5_K // 2
    for t in range(S5_T):
        for b in range(nb):
            up_scr[t, pl.ds(b, nchunk, stride=nb), :] = u_ref[pl.ds(b * seq_len + t, nchunk, stride=S5_T), :]

    def u_tile(rt):
        rows = slice(rt * rtile, (rt + 1) * rtile)
        return jnp.concatenate([up_scr[t, rows, :] for t in range(S5_T)], axis=1).astype(BF16)

    for rt in range(r // rtile):
        rows = slice(rt * rtile, (rt + 1) * rtile)
        ub = u_tile(rt)
        sf_scr[rows, :] = jnp.dot(ub, ef_ref[0], preferred_element_type=F32)
        sb_scr[rows, :] = jnp.dot(ub, eb_ref[0], preferred_element_type=F32)

    lane = lax.broadcasted_iota(jnp.int32, (1, S5_K), 1)

    def coeffs(d):
        lt = lt_ref[0, d:d + 1, :]
        sw = pltpu.roll(lt, half, 1)
        return jnp.where(lane < half, lt, sw), jnp.where(lane < half, -sw, lt)

    af, bf = coeffs(0)
    ab, bb = coeffs(1)
    if has_h0:
        h0f, h0b = h0_ref[0, 0, 0], h0_ref[0, 1, 0]
    else:
        h0f = h0b = jnp.zeros((nb, S5_K), F32)

    tile_rows = max(nb, 8)
    cps = tile_rows // nb
    n_iter = nchunk // cps

    def step(h, a, b, seg):
        return a * h + b * pltpu.roll(h, half, 1) + seg

    def body(i, carry):
        hf, hb = carry
        rf = pl.ds(pl.multiple_of(i * tile_rows, tile_rows), tile_rows)
        rb = pl.ds(pl.multiple_of((n_iter - 1 - i) * tile_rows, tile_rows), tile_rows)
        xf, xb = sf_scr[rf, :], sb_scr[rb, :]
        of, ob = [], [None] * cps
        for jj in range(cps):
            of.append(hf)
            hf = step(hf, af, bf, xf[jj * nb:(jj + 1) * nb, :])
        for jj in reversed(range(cps)):
            ob[jj] = hb
            hb = step(hb, ab, bb, xb[jj * nb:(jj + 1) * nb, :])
        sf_scr[rf, :] = of[0] if cps == 1 else jnp.concatenate(of, axis=0)
        sb_scr[rb, :] = ob[0] if cps == 1 else jnp.concatenate(ob, axis=0)
        return hf, hb

    hf, hb = lax.fori_loop(0, n_iter, body, (h0f, h0b))
    if emit_state:
        hfin_ref[0, 0, 0] = hf
        hfin_ref[0, 1, 0] = hb

    for rt in range(r // rtile):
        rows = slice(rt * rtile, (rt + 1) * rtile)
        y = jnp.dot(u_tile(rt), m_ref[0], preferred_element_type=F32)
        y = y + jnp.dot(sf_scr[rows, :].astype(BF16), ff_ref[0], preferred_element_type=F32)
        y = y + jnp.dot(sb_scr[rows, :].astype(BF16), fb_ref[0], preferred_element_type=F32)
        for t in range(S5_T):
            up_scr[t, rows, :] = y[:, t * LANES:(t + 1) * LANES]
    for t in range(S5_T):
        for b in range(nb):
            y_ref[pl.ds(b * seq_len + t, nchunk, stride=S5_T), :] = up_scr[t, pl.ds(b, nchunk, stride=nb), :]


def _s5(u, mats_l, h0, bsz, seq_len, nb, emit_state):
    nchunk = seq_len // S5_T
    r = nchunk * nb
    has_h0 = h0 is not None
    qi = lambda q, j: (q, 0, 0)
    wspec = pl.BlockSpec((1, S5_K, S5_K), qi)
    in_specs = [pl.BlockSpec((nb * seq_len, LANES), lambda q, j: (j, q))] + [wspec] * 5 \
        + [pl.BlockSpec((1, 2, S5_K), qi)]
    args = [u] + list(mats_l)
    st_spec = pl.BlockSpec((1, 2, 1, nb, S5_K), lambda q, j: (q, 0, j, 0, 0))
    st_shape = (S5_SG, 2, bsz // nb, nb, S5_K)
    if has_h0:
        in_specs.append(st_spec)
        args.append(h0.reshape(st_shape))
    out_specs = [pl.BlockSpec((nb * seq_len, LANES), lambda q, j: (j, q))]
    out_shape = [jax.ShapeDtypeStruct((bsz * seq_len, B_WIDTH), F32)]
    if emit_state:
        out_specs.append(st_spec)
        out_shape.append(jax.ShapeDtypeStruct(st_shape, F32))
    outs = pl.pallas_call(
        functools.partial(_s5_kernel, nchunk=nchunk, nb=nb, seq_len=seq_len, has_h0=has_h0, emit_state=emit_state),
        grid=(S5_SG, bsz // nb),
        in_specs=in_specs,
        out_specs=out_specs,
        out_shape=out_shape,
        scratch_shapes=[pltpu.VMEM((S5_T, r, LANES), F32), pltpu.VMEM((r, S5_K), F32), pltpu.VMEM((r, S5_K), F32)],
        compiler_params=_cp(("parallel", "parallel")),
        name="s5",
    )(*args)
    if emit_state:
        return outs[0], outs[1].reshape(S5_SG, 2, bsz, S5_K)
    return (outs[0],)


def _merge_kernel(x_ref, shift_ref, scale_ref, gate_ref, gnorm_ref, of_ref, ob_ref, zs_ref, gout_ref, ys_ref, u_ref,
                  oc_ref, wg_ref, wba_ref, wbb_ref, wbc_ref, wout_ref, wglu_ref, bglu_ref, dskip_ref, xo_ref):
    x = x_ref[...]
    h = ((_rms(x) * gnorm_ref[...]) * (1.0 + scale_ref[0]) + shift_ref[0]).astype(BF16)
    od = of_ref[...] + ob_ref[...]
    gout = gout_ref[...]
    oa = jnp.concatenate([_rms(od[:, hh * HEAD_DIM_A:(hh + 1) * HEAD_DIM_A]) * gout for hh in range(H_A)], axis=1)
    oa = oa * zs_ref[...]
    yb = jax.nn.gelu(ys_ref[...] + dskip_ref[...] * u_ref[...])
    ob = yb * jax.nn.sigmoid(_mm(yb, wglu_ref[...]) + bglu_ref[...])
    acc = None
    for nbr, (o, w) in enumerate(((oa, wba_ref), (ob, wbb_ref), (oc_ref[...], wbc_ref))):
        gate = jax.nn.sigmoid(jnp.dot(h, wg_ref[:, nbr * D_MODEL:(nbr + 1) * D_MODEL], preferred_element_type=F32))
        term = gate * _mm(o, w[...])
        acc = term if acc is None else acc + term
    out = _mm(acc, wout_ref[...])
    xo_ref[...] = x + gate_ref[0] * out


def _merge(x, mods_rows, row_of_tile, lp, o_f, o_b, zs, ys, u, oc, tl):
    n = x.shape[0]
    const2 = lambda i: (0, 0)
    tok = lambda w: pl.BlockSpec((tl, w), lambda i: (i, 0))
    mod = lambda j: pl.BlockSpec((1, 1, D_MODEL), lambda i: (row_of_tile(i) * 6 + j, 0, 0))
    return pl.pallas_call(
        _merge_kernel,
        grid=(n // tl,),
        in_specs=[tok(D_MODEL), mod(0), mod(1), mod(2), pl.BlockSpec((1, D_MODEL), const2),
                  tok(A_WIDTH), tok(A_WIDTH), tok(A_WIDTH), pl.BlockSpec((1, HEAD_DIM_A), const2),
                  tok(B_WIDTH), tok(B_WIDTH), tok(MLA_W),
                  pl.BlockSpec((D_MODEL, N_BRANCH * D_MODEL), const2),
                  pl.BlockSpec((BRANCH_WIDTH, D_MODEL), const2), pl.BlockSpec((BRANCH_WIDTH, D_MODEL), const2),
                  pl.BlockSpec((MLA_W, D_MODEL), const2), pl.BlockSpec((D_MODEL, D_MODEL), const2),
                  pl.BlockSpec((B_WIDTH, B_WIDTH), const2), pl.BlockSpec((1, B_WIDTH), const2),
                  pl.BlockSpec((1, B_WIDTH), const2)],
        out_specs=tok(D_MODEL),
        out_shape=jax.ShapeDtypeStruct((n, D_MODEL), F32),
        compiler_params=_cp(("parallel",)),
        name="merge",
    )(x, mods_rows, mods_rows, mods_rows, lp['g_norm_mix'], o_f, o_b, zs, lp['g_delta_out'], ys, u, oc,
      lp['w_gates'], lp['w_br_a'], lp['w_br_b'],
      lp['w_br_c_pad'], lp['w_out'], lp['w_glu'], lp['b_glu'], lp['s5_d'])


FF_BLK = 256


def _ffn_kernel(*refs, tl, tiles_per_seq, final):
    it = iter(refs)
    xp_ref, xc_ref, xn_ref = next(it), next(it), next(it)
    shift_ref, scale_ref, gate_ref, gnorm_ref = next(it), next(it), next(it), next(it)
    wup_ref, convw_ref, convb_ref, wdown_ref = next(it), next(it), next(it), next(it)
    gfin_ref = next(it) if final else None
    xo_ref = next(it)
    yo_ref = next(it) if final else None
    act_scr = next(it)

    i = pl.program_id(0)
    pos = i % tiles_per_seq
    mod_scale = 1.0 + scale_ref[0]
    mod_shift = shift_ref[0]
    gain = gnorm_ref[...]

    def norm_mod(x):
        return ((_rms(x) * gain) * mod_scale + mod_shift).astype(BF16)

    x = xc_ref[...]
    h_ext = jnp.concatenate([norm_mod(xp_ref[...]), norm_mod(x), norm_mod(xn_ref[...])], axis=0)
    n_ext = tl + 2 * HALO
    rowi = lax.broadcasted_iota(jnp.int32, (n_ext, 1), 0)
    valid = jnp.logical_and(jnp.logical_or(pos > 0, rowi >= HALO),
                            jnp.logical_or(pos < tiles_per_seq - 1, rowi < HALO + tl)).astype(F32)
    pad = FFN_CONV // 2

    def conv_act(cols):
        up = jnp.dot(h_ext, wup_ref[:, cols], preferred_element_type=F32) * valid
        acc = None
        for t in range(FFN_CONV):
            sh = (pad - t) % n_ext
            src = up if sh == 0 else pltpu.roll(up, sh, 0)
            term = src[HALO:HALO + tl, :] * convw_ref[t:t + 1, cols]
            acc = term if acc is None else acc + term
        return acc + convb_ref[:, cols]

    for j in range(D_FF // FF_BLK):
        gcols = slice(j * FF_BLK, (j + 1) * FF_BLK)
        vcols = slice(D_FF + j * FF_BLK, D_FF + (j + 1) * FF_BLK)
        act_scr[:, gcols] = (_silu(conv_act(gcols)) * conv_act(vcols)).astype(BF16)
    out = jnp.dot(act_scr[...], wdown_ref[...], preferred_element_type=F32)
    xo = x + gate_ref[0] * out
    xo_ref[...] = xo
    if final:
        yo_ref[...] = _rms(xo) * gfin_ref[...]


def _ffn(x, mods_rows, row_of_tile, lp, seq_len, tl, g_final):
    n = x.shape[0]
    tps = seq_len // tl
    final = g_final is not None
    xprev, xnext = _halo_maps(n, tl)
    const2 = lambda i: (0, 0)
    mod = lambda j: pl.BlockSpec((1, 1, D_MODEL), lambda i: (row_of_tile(i) * 6 + j, 0, 0))
    tok = pl.BlockSpec((tl, D_MODEL), lambda i: (i, 0))
    in_specs = [pl.BlockSpec((HALO, D_MODEL), xprev), tok, pl.BlockSpec((HALO, D_MODEL), xnext),
                mod(3), mod(4), mod(5), pl.BlockSpec((1, D_MODEL), const2),
                pl.BlockSpec((D_MODEL, 2 * D_FF), const2), pl.BlockSpec((FFN_CONV, 2 * D_FF), const2),
                pl.BlockSpec((1, 2 * D_FF), const2), pl.BlockSpec((D_FF, D_MODEL), const2)]
    args = [x, x, x, mods_rows, mods_rows, mods_rows, lp['g_norm_ffn'], lp['w_ffn_up'], lp['conv_ffn'],
            lp['b_conv_ffn'], lp['w_ffn_down']]
    out_specs = [tok]
    out_shape = [jax.ShapeDtypeStruct((n, D_MODEL), F32)]
    if final:
        in_specs.append(pl.BlockSpec((1, D_MODEL), const2))
        args.append(g_final)
        out_specs.append(tok)
        out_shape.append(jax.ShapeDtypeStruct((n, D_MODEL), F32))
    return pl.pallas_call(
        functools.partial(_ffn_kernel, tl=tl, tiles_per_seq=tps, final=final),
        grid=(n // tl,),
        in_specs=in_specs,
        out_specs=out_specs,
        out_shape=out_shape,
        scratch_shapes=[pltpu.VMEM((tl, D_FF), BF16)],
        compiler_params=_cp(("parallel",)),
        name="ffn",
    )(*args)


def _pad_heads(w, head_w, n_heads):
    k = w.shape[0]
    w = w.reshape(k, n_heads, head_w)
    w = jnp.pad(w, ((0, 0), (0, 0), (0, HEAD_PAD - head_w)))
    return w.reshape(k, n_heads * HEAD_PAD)


def _layer_params(l, p):
    w_in = p['w_in'][l]
    o = 0
    parts = {}
    for name, wd in (('qkv', 3 * A_WIDTH), ('z', A_WIDTH), ('beta', 2 * H_A), ('alpha', 2 * H_A), ('u', B_WIDTH),
                     ('qa', Q_LORA), ('kva', KV_LORA), ('kr', QK_ROPE), ('gates', N_BRANCH * D_MODEL)):
        parts[name] = w_in[:, o:o + wd]
        o += wd
    small = jnp.concatenate([parts['beta'], parts['alpha'], parts['kr'],
                             jnp.zeros((D_MODEL, LANES - 4 * H_A - QK_ROPE), F32)], axis=1)
    w_rest = jnp.concatenate([parts['z'], parts['u'], parts['qa'], parts['kva'], small], axis=1)
    pad8 = lambda v: jnp.pad(v.reshape(1, 2 * H_A), ((0, 0), (2 * H_A, LANES - 4 * H_A)))
    w_kv = p['w_kv_b'][l].reshape(KV_LORA, H_C, QK_NOPE + V_HEAD)
    w_k = w_kv[:, :, :QK_NOPE].reshape(KV_LORA, H_C * QK_NOPE)
    w_v = w_kv[:, :, QK_NOPE:].reshape(KV_LORA, H_C * V_HEAD)
    w_br_c = p['w_branch'][l, 2].reshape(H_C, V_HEAD, D_MODEL)
    w_br_c = jnp.pad(w_br_c, ((0, 0), (0, HEAD_PAD - V_HEAD), (0, 0))).reshape(MLA_W, D_MODEL)
    row = lambda v: v.reshape(1, -1)
    return {
        'g_norm_mix': row(p['g_norm_mix'][l]), 'g_norm_ffn': row(p['g_norm_ffn'][l]),
        'w_qkv': parts['qkv'].astype(BF16), 'w_rest': w_rest.astype(BF16), 'w_gates': parts['gates'].astype(BF16),
        'conv_qkv': p['conv_qkv'][l],
        'a_log128': pad8(p['a_log'][l]), 'dt_bias128': pad8(p['dt_bias'][l]),
        'g_delta_out': row(p['g_delta_out'][l]),
        'g_q_a': row(p['g_q_a'][l]), 'g_kv_a': row(p['g_kv_a'][l]),
        'w_q_pad': _pad_heads(p['w_q_b'][l], QK_NOPE + QK_ROPE, H_C).astype(BF16),
        'w_k_pad': _pad_heads(w_k, QK_NOPE, H_C).astype(BF16),
        'w_v_pad': _pad_heads(w_v, V_HEAD, H_C).astype(BF16),
        'w_br_a': p['w_branch'][l, 0].astype(BF16), 'w_br_b': p['w_branch'][l, 1].astype(BF16),
        'w_br_c_pad': w_br_c.astype(BF16),
        'w_out': p['w_out'][l].astype(BF16),
        'w_glu': p['w_glu'][l].astype(BF16), 'b_glu': row(p['b_glu'][l]), 's5_d': row(p['s5_d'][l]),
        'w_ffn_up': p['w_ffn_up'][l].astype(BF16), 'conv_ffn': p['conv_ffn'][l],
        'b_conv_ffn': row(p['b_conv_ffn'][l]), 'w_ffn_down': p['w_ffn_down'][l].astype(BF16),
    }


def _rope_tables(length):
    rows = length // GRID_W
    row = jnp.repeat(jnp.arange(rows, dtype=F32), GRID_W)
    col = (jnp.arange(length) % GRID_W).astype(F32)
    n_freq = QK_ROPE // 4
    inv_freq = 1.0 / (ROPE_BASE ** (jnp.arange(n_freq, dtype=F32) / n_freq))
    ang = jnp.concatenate([row[:, None] * inv_freq, col[:, None] * inv_freq], axis=-1)
    cos, sin = jnp.cos(ang), jnp.sin(ang)
    ones = jnp.ones((length, QK_NOPE), F32)
    zeros = jnp.zeros((length, QK_NOPE), F32)
    tail = jnp.zeros((length, HEAD_PAD - QK_NOPE - QK_ROPE), F32)
    cos_t = jnp.concatenate([ones, cos, cos, tail], axis=1)
    sin_t = jnp.concatenate([zeros, -sin, sin, tail], axis=1)
    return cos_t, sin_t


def _layer(x, bsz, seq_len, mods_rows, row_of_tile, lp, s5mats, ctx, rope_tabs, tl, g_final):
    is_ctx = ctx is None
    outs = _inproj(x, mods_rows, row_of_tile, lp, seq_len, tl, rope_tabs, emit_ctx=is_ctx)
    if is_ctx:
        qkv, zs, u, bg, ckv, kr, qh, kh, vh = outs
    else:
        qkv, zs, u, bg, qh, kh, vh = outs

    prep = _delta_prep(qkv, bg, bsz, seq_len)
    d_out = _delta_recur(prep, None if is_ctx else ctx['s0'], bsz, seq_len, min(bsz, 4), emit_state=is_ctx)
    o_f = d_out[0].reshape(bsz * seq_len, A_WIDTH)
    o_b = d_out[1].reshape(bsz * seq_len, A_WIDTH)

    s5_nb = max(1, min(bsz, (1024 * S5_T) // seq_len))
    s_out = _s5(u, s5mats, None if is_ctx else ctx['h0'], bsz, seq_len, s5_nb, emit_state=is_ctx)
    ys = s_out[0]

    segs = [(kh, vh, seq_len)]
    if not is_ctx:
        segs = [(ctx['kh'], ctx['vh'], ctx['past'])] + segs
    oc = _attention(qh, segs, bsz, seq_len, min(seq_len, 512), H_C if seq_len <= 512 else 2)

    x = _merge(x, mods_rows, row_of_tile, lp, o_f, o_b, zs, ys, u, oc, tl)
    f_out = _ffn(x, mods_rows, row_of_tile, lp, seq_len, tl, g_final)
    extras = None
    if is_ctx:
        hfin = s_out[1]
        hfin = hfin.reshape(S5_SG, 2, bsz, 2, S5_SG_GROUPS, S5_STATE).transpose(3, 2, 1, 0, 4, 5)
        hfin = hfin.reshape(2, bsz, 2, S5_GROUPS, S5_STATE)
        extras = (d_out[2], hfin[0], hfin[1], ckv.reshape(bsz, seq_len, KV_LORA), kr.reshape(bsz, seq_len, QK_ROPE))
    return f_out, extras


def kernel(x_prompt, x_sample, state_delta, state_s5_re, state_s5_im, cache_ckv, cache_krope, c, c_ctx, w_mod, b_mod, g_norm_mix, g_norm_ffn, w_in, conv_qkv, a_log, dt_bias, g_delta_out, s5_lam_re, s5_lam_im, s5_log_dt, s5_b_re, s5_b_im, s5_c_re, s5_c_im, s5_d, w_glu, b_glu, g_q_a, w_q_b, g_kv_a, w_kv_b, w_branch, w_out, w_ffn_up, conv_ffn, b_conv_ffn, w_ffn_down, g_final):
    p = dict(g_norm_mix=g_norm_mix, g_norm_ffn=g_norm_ffn, w_in=w_in, conv_qkv=conv_qkv, a_log=a_log,
             dt_bias=dt_bias, g_delta_out=g_delta_out, s5_d=s5_d, w_glu=w_glu, b_glu=b_glu, g_q_a=g_q_a,
             w_q_b=w_q_b, g_kv_a=g_kv_a, w_kv_b=w_kv_b, w_branch=w_branch, w_out=w_out, w_ffn_up=w_ffn_up,
             conv_ffn=conv_ffn, b_conv_ffn=b_conv_ffn, w_ffn_down=w_ffn_down)
    bp, lp_len, _ = x_prompt.shape
    bs, ls_len, _ = x_sample.shape
    past = cache_ckv.shape[2]
    depth = w_in.shape[0]

    mod_rows = 16
    cvec = jnp.concatenate([c_ctx[None, :], c, jnp.zeros((mod_rows - 1 - bs, D_MODEL), F32)], axis=0).astype(F32)
    mods = _modulation(cvec, w_mod, b_mod)

    s5_all = _s5prep(s5_lam_re, s5_lam_im, s5_log_dt, s5_b_re, s5_b_im, s5_c_re, s5_c_im)

    rope_tabs = _rope_tables(ls_len)
    tl_p = min(lp_len, 512)
    tl_s = min(ls_len, 512)
    tiles_per_seq_s = ls_len // tl_s

    def sg_state(re, im):
        t = jnp.stack([re, im], axis=0).astype(F32).reshape(2, bs, 2, S5_SG, S5_SG_GROUPS, S5_STATE)
        return t.transpose(3, 2, 1, 0, 4, 5).reshape(S5_SG, 2, bs, S5_K)

    xp = x_prompt.astype(F32).reshape(bp * lp_len, D_MODEL)
    xs = x_sample.astype(F32).reshape(bs * ls_len, D_MODEL)
    deltas, s5_res, s5_ims, ckvs, kropes = [], [], [], [], []
    yp = ys = None
    for l in range(depth):
        lp = _layer_params(l, p)
        mods_rows = mods[l].reshape(mod_rows * 6, 1, D_MODEL)
        sl = slice(l * S5_SG, (l + 1) * S5_SG)
        s5mats = tuple(m[sl] for m in s5_all)
        last = l == depth - 1
        gfin = g_final.reshape(1, D_MODEL) if last else None

        out_p, extras = _layer(xp, bp, lp_len, mods_rows, lambda i: 0, lp, s5mats, None, None, tl_p, gfin)
        deltas.append(extras[0])
        s5_res.append(extras[1])
        s5_ims.append(extras[2])
        ckvs.append(extras[3])
        kropes.append(extras[4])

        kh_c, vh_c = _kvcache(cache_ckv[:, l].astype(F32).reshape(bs * past, KV_LORA),
                              cache_krope[:, l].astype(F32).reshape(bs * past, QK_ROPE), lp, past)
        ctx = dict(s0=state_delta[:, l].astype(F32), h0=sg_state(state_s5_re[:, l], state_s5_im[:, l]),
                   kh=kh_c, vh=vh_c, past=past)
        out_s, _ = _layer(xs, bs, ls_len, mods_rows, lambda i: 1 + i // tiles_per_seq_s, lp, s5mats, ctx, rope_tabs,
                          tl_s, gfin)
        xp, xs = out_p[0], out_s[0]
        if last:
            yp, ys = out_p[1], out_s[1]

    y_prompt = yp.reshape(bp, lp_len, D_MODEL)
    y_sample = ys.reshape(bs, ls_len, D_MODEL)
    return (y_prompt, y_sample, jnp.stack(deltas, axis=1), jnp.stack(s5_res, axis=1), jnp.stack(s5_ims, axis=1),
            jnp.stack(ckvs, axis=1), jnp.stack(kropes, axis=1))
```

```python
import functools
import math

import jax
import jax.numpy as jnp
from jax import lax
from jax.experimental import pallas as pl
from jax.experimental.pallas import tpu as pltpu

F32 = jnp.float32
BF16 = jnp.bfloat16

D_MODEL = 1024
DEPTH = 4
GRID_W = 64
H_A = 4
HEAD_DIM_A = 128
A_WIDTH = H_A * HEAD_DIM_A
SHORT_CONV = 5
CHUNK = 64
S5_GROUP = 16
S5_STATE = 64
B_WIDTH = 512
S5_GROUPS = B_WIDTH // S5_GROUP
H_C = 8
QK_NOPE = 64
QK_ROPE = 32
V_HEAD = 64
Q_LORA = 384
KV_LORA = 256
ROPE_BASE = 10000.0
N_BRANCH = 3
BRANCH_WIDTH = 512
D_FF = 2816
FFN_CONV = 3
NORM_EPS = 1e-6

LANES = 128
HEAD_PAD = LANES
SUB = 16
S5_T = 8
S5_ROWS = S5_T * S5_GROUP
S5_SG_GROUPS = LANES // S5_GROUP
S5_SG = S5_GROUPS // S5_SG_GROUPS
S5_K = S5_T * LANES

VMEM_LIMIT = 56 * 1024 * 1024
HALO = 16


def _cp(sem):
    return pltpu.CompilerParams(dimension_semantics=sem, vmem_limit_bytes=VMEM_LIMIT)


def _mm(a, b):
    return jnp.dot(a.astype(BF16), b.astype(BF16), preferred_element_type=F32)


def _bmm(a, b):
    return lax.dot_general(a.astype(BF16), b.astype(BF16), (((2,), (1,)), ((0,), (0,))),
                           preferred_element_type=F32)


def _bmm_nt(a, b):
    return lax.dot_general(a.astype(BF16), b.astype(BF16), (((2,), (2,)), ((0,), (0,))),
                           preferred_element_type=F32)


def _bmm_tn(a, b):
    return lax.dot_general(a.astype(BF16), b.astype(BF16), (((1,), (1,)), ((0,), (0,))),
                           preferred_element_type=F32)


def _split3(x):
    x1 = x.astype(BF16)
    r = x - x1.astype(F32)
    x2 = r.astype(BF16)
    x3 = (r - x2.astype(F32)).astype(BF16)
    return x1, x2, x3


def _mm_exact_rhs(a, b_bf16):
    a1, a2, a3 = _split3(a)
    d = lambda t: jnp.dot(t, b_bf16, preferred_element_type=F32)
    return d(a1) + d(a2) + d(a3)


def _mm_exact_lhs(a_bf16, b):
    b1, b2, b3 = _split3(b)
    d = lambda t: jnp.dot(a_bf16, t, preferred_element_type=F32)
    return d(b1) + d(b2) + d(b3)


def _mm3_nt(a, b):
    a1 = a.astype(BF16)
    a2 = (a - a1.astype(F32)).astype(BF16)
    b1 = b.astype(BF16)
    b2 = (b - b1.astype(F32)).astype(BF16)
    d = lambda s, t: lax.dot_general(s, t, (((1,), (1,)), ((), ())), preferred_element_type=F32)
    return d(a1, b1) + d(a1, b2) + d(a2, b1)


def _silu(x):
    return x * jax.nn.sigmoid(x)


def _rms(x):
    return x * lax.rsqrt(jnp.mean(x * x, axis=-1, keepdims=True) + NORM_EPS)


def _mod_kernel(c_ref, w_ref, b_ref, o_ref):
    s = _silu(c_ref[...])
    o_ref[0] = _mm(s, w_ref[0]) + b_ref[0]


def _modulation(cvec, w_mod, b_mod):
    rows = cvec.shape[0]
    nblk = w_mod.shape[-1] // D_MODEL
    return pl.pallas_call(
        _mod_kernel,
        grid=(DEPTH, nblk),
        in_specs=[
            pl.BlockSpec((rows, D_MODEL), lambda l, j: (0, 0)),
            pl.BlockSpec((1, D_MODEL, D_MODEL), lambda l, j: (l, 0, j)),
            pl.BlockSpec((1, 1, D_MODEL), lambda l, j: (l, 0, j)),
        ],
        out_specs=pl.BlockSpec((1, rows, D_MODEL), lambda l, j: (l, 0, j)),
        out_shape=jax.ShapeDtypeStruct((DEPTH, rows, w_mod.shape[-1]), F32),
        compiler_params=_cp(("parallel", "parallel")),
        name="modulation",
    )(cvec, w_mod, b_mod.reshape(DEPTH, 1, -1))


REST_W = 512 + 512 + Q_LORA + KV_LORA + LANES
OFF_Z, OFF_U, OFF_QA, OFF_KVA, OFF_SM = 0, 512, 1024, 1024 + Q_LORA, 1024 + Q_LORA + KV_LORA
SM_KR = 16
MLA_W = H_C * HEAD_PAD
INPROJ_SUB = 256


def _rope_apply(x, cos, sin):
    lane = lax.broadcasted_iota(jnp.int32, x.shape, 1)
    partner = jnp.where(lane < QK_NOPE + QK_ROPE // 2,
                        pltpu.roll(x, HEAD_PAD - QK_ROPE // 2, 1),
                        pltpu.roll(x, QK_ROPE // 2, 1))
    return x * cos + partner * sin


def _inproj_kernel(*refs, tl, tiles_per_seq, rope, emit_ctx):
    it = iter(refs)
    xp_ref, xc_ref, xn_ref = next(it), next(it), next(it)
    shift_ref, scale_ref, gnorm_ref = next(it), next(it), next(it)
    wqkv_ref, convw_ref, wrest_ref = next(it), next(it), next(it)
    gqa_ref, gkva_ref, alog_ref, dtb_ref = next(it), next(it), next(it), next(it)
    wq_ref, wk_ref, wv_ref = next(it), next(it), next(it)
    cos_ref = sin_ref = None
    if rope:
        cos_ref, sin_ref = next(it), next(it)
    qkv_out, zs_out, u_out, bg_out = next(it), next(it), next(it), next(it)
    ckv_out = kr_out = None
    if emit_ctx:
        ckv_out, kr_out = next(it), next(it)
    qh_out, kh_out, vh_out = next(it), next(it), next(it)
    qkv_scr = next(it)

    i = pl.program_id(0)
    pos = i % tiles_per_seq
    mod_scale = 1.0 + scale_ref[0]
    mod_shift = shift_ref[0]
    gain = gnorm_ref[...]

    def norm_mod(x):
        return ((_rms(x) * gain) * mod_scale + mod_shift).astype(BF16)

    nsub = tl // INPROJ_SUB
    ts = INPROJ_SUB
    n_ext = ts + 2 * HALO
    rowi = lax.broadcasted_iota(jnp.int32, (n_ext, 1), 0)
    pad = SHORT_CONV // 2
    qscale = (QK_NOPE + QK_ROPE) ** -0.5 * math.log2(math.e)
    for sub in range(nsub):
        r0 = sub * ts
        rows = slice(r0, r0 + ts)
        h_cur = norm_mod(xc_ref[rows, :])
        h_before = norm_mod(xp_ref[...] if sub == 0 else xc_ref[r0 - HALO:r0, :])
        h_after = norm_mod(xn_ref[...] if sub == nsub - 1 else xc_ref[r0 + ts:r0 + ts + HALO, :])
        h_ext = jnp.concatenate([h_before, h_cur, h_after], axis=0)

        qkv_ext = _mm(h_ext, wqkv_ref[...])
        if sub == 0 or sub == nsub - 1:
            ok_lo = jnp.logical_or(pos > 0, rowi >= HALO) if sub == 0 else True
            ok_hi = jnp.logical_or(pos < tiles_per_seq - 1, rowi < HALO + ts) if sub == nsub - 1 else True
            qkv_ext = qkv_ext * jnp.logical_and(ok_lo, ok_hi).astype(F32)
        qkv_scr[sub] = qkv_ext
        for blk in range(3):
            cols = slice(blk * A_WIDTH, (blk + 1) * A_WIDTH)
            acc = None
            for t in range(SHORT_CONV):
                term = qkv_scr[sub, HALO - pad + t:HALO - pad + t + ts, cols] * convw_ref[t:t + 1, cols]
                acc = term if acc is None else acc + term
            acc = _silu(acc)
            if blk < 2:
                parts = []
                for hh in range(H_A):
                    a = acc[:, hh * HEAD_DIM_A:(hh + 1) * HEAD_DIM_A]
                    parts.append(a * lax.rsqrt(jnp.sum(a * a, axis=-1, keepdims=True) + NORM_EPS))
                acc = jnp.concatenate(parts, axis=1)
            qkv_out[rows, cols] = acc.astype(BF16)

        rest = _mm(h_cur, wrest_ref[...])
        zs_out[rows, :] = _silu(rest[:, OFF_Z:OFF_Z + 512])
        u_out[rows, :] = rest[:, OFF_U:OFF_U + 512]

        small = rest[:, OFF_SM:OFF_SM + LANES]
        lane = lax.broadcasted_iota(jnp.int32, small.shape, 1)
        beta = jax.nn.sigmoid(small)
        glog = -jnp.exp(alog_ref[...]) * jax.nn.softplus(small + dtb_ref[...])
        bg_out[rows, :] = jnp.where(lane < 2 * H_A, beta, glog)

        qa = _rms(rest[:, OFF_QA:OFF_QA + Q_LORA]) * gqa_ref[...]
        ckv = _rms(rest[:, OFF_KVA:OFF_KVA + KV_LORA]) * gkva_ref[...]
        if emit_ctx:
            ckv_out[rows, :] = ckv
            kr_out[rows, :] = small[:, SM_KR:SM_KR + QK_ROPE]
        ckv_b = ckv.astype(BF16)
        qhat = _mm(qa, wq_ref[...])
        khat = _mm(ckv_b, wk_ref[...])
        vh_out[rows, :] = _mm(ckv_b, wv_ref[...]).astype(BF16)
        kr_al = jnp.where((lane >= QK_NOPE) & (lane < QK_NOPE + QK_ROPE),
                          pltpu.roll(small, QK_NOPE - SM_KR, 1), 0.0)
        if rope:
            cos, sin = cos_ref[rows, :], sin_ref[rows, :]
            kr_al = _rope_apply(kr_al, cos, sin)
        for hh in range(H_C):
            cs = slice(hh * HEAD_PAD, (hh + 1) * HEAD_PAD)
            qh = qhat[:, cs]
            if rope:
                qh = _rope_apply(qh, cos, sin)
            qh_out[rows, cs] = (qh * qscale).astype(BF16)
            kh_out[rows, cs] = (khat[:, cs] + kr_al).astype(BF16)


def _halo_maps(n, tl):
    nh = n // HALO

    def xprev(i):
        return (jnp.maximum(i * (tl // HALO) - 1, 0), 0)

    def xnext(i):
        return (jnp.minimum((i + 1) * (tl // HALO), nh - 1), 0)

    return xprev, xnext


def _inproj(x, mods_rows, row_of_tile, lp, seq_len, tl, rope_tabs, emit_ctx):
    n = x.shape[0]
    nt = n // tl
    tps = seq_len // tl
    rope = rope_tabs is not None
    xprev, xnext = _halo_maps(n, tl)
    const2 = lambda i: (0, 0)
    in_specs = [
        pl.BlockSpec((HALO, D_MODEL), xprev),
        pl.BlockSpec((tl, D_MODEL), lambda i: (i, 0)),
        pl.BlockSpec((HALO, D_MODEL), xnext),
        pl.BlockSpec((1, 1, D_MODEL), lambda i: (row_of_tile(i) * 6 + 0, 0, 0)),
        pl.BlockSpec((1, 1, D_MODEL), lambda i: (row_of_tile(i) * 6 + 1, 0, 0)),
        pl.BlockSpec((1, D_MODEL), const2),
        pl.BlockSpec((D_MODEL, 3 * A_WIDTH), const2),
        pl.BlockSpec((SHORT_CONV, 3 * A_WIDTH), const2),
        pl.BlockSpec((D_MODEL, REST_W), const2),
        pl.BlockSpec((1, Q_LORA), const2),
        pl.BlockSpec((1, KV_LORA), const2),
        pl.BlockSpec((1, LANES), const2),
        pl.BlockSpec((1, LANES), const2),
        pl.BlockSpec((Q_LORA, MLA_W), const2),
        pl.BlockSpec((KV_LORA, MLA_W), const2),
        pl.BlockSpec((KV_LORA, MLA_W), const2),
    ]
    args = [x, x, x, mods_rows, mods_rows, lp['g_norm_mix'], lp['w_qkv'], lp['conv_qkv'], lp['w_rest'],
            lp['g_q_a'], lp['g_kv_a'], lp['a_log128'], lp['dt_bias128'], lp['w_q_pad'], lp['w_k_pad'], lp['w_v_pad']]
    if rope:
        in_specs += [pl.BlockSpec((tl, HEAD_PAD), lambda i: (i % tps, 0))] * 2
        args += list(rope_tabs)
    tok = lambda w: pl.BlockSpec((tl, w), lambda i: (i, 0))
    out_specs = [tok(3 * A_WIDTH), tok(512), tok(512), tok(LANES)]
    out_shape = [jax.ShapeDtypeStruct((n, 3 * A_WIDTH), BF16), jax.ShapeDtypeStruct((n, 512), F32),
                 jax.ShapeDtypeStruct((n, 512), F32), jax.ShapeDtypeStruct((n, LANES), F32)]
    if emit_ctx:
        out_specs += [tok(KV_LORA), tok(QK_ROPE)]
        out_shape += [jax.ShapeDtypeStruct((n, KV_LORA), F32), jax.ShapeDtypeStruct((n, QK_ROPE), F32)]
    out_specs += [tok(MLA_W)] * 3
    out_shape += [jax.ShapeDtypeStruct((n, MLA_W), BF16)] * 3
    return pl.pallas_call(
        functools.partial(_inproj_kernel, tl=tl, tiles_per_seq=tps, rope=rope, emit_ctx=emit_ctx),
        grid=(nt,),
        in_specs=in_specs,
        out_specs=out_specs,
        out_shape=out_shape,
        scratch_shapes=[pltpu.VMEM((tl // INPROJ_SUB, INPROJ_SUB + 2 * HALO, 3 * A_WIDTH), F32)],
        compiler_params=_cp(("parallel",)),
        name="inproj",
    )(*args)


def _kvcache_kernel(ckv_ref, kr_ref, wk_ref, wv_ref, kh_out, vh_out):
    ckv_b = ckv_ref[...].astype(BF16)
    khat = _mm(ckv_b, wk_ref[...])
    vh_out[...] = _mm(ckv_b, wv_ref[...]).astype(BF16)
    kr_al = kr_ref[...]
    for hh in range(H_C):
        cs = slice(hh * HEAD_PAD, (hh + 1) * HEAD_PAD)
        kh_out[:, cs] = (khat[:, cs] + kr_al).astype(BF16)


def _kvcache(ckv, kr, lp, tl):
    n = ckv.shape[0]
    const2 = lambda i: (0, 0)
    kr = jnp.pad(kr, ((0, 0), (QK_NOPE, HEAD_PAD - QK_NOPE - QK_ROPE)))
    return pl.pallas_call(
        _kvcache_kernel,
        grid=(n // tl,),
        in_specs=[pl.BlockSpec((tl, KV_LORA), lambda i: (i, 0)), pl.BlockSpec((tl, HEAD_PAD), lambda i: (i, 0)),
                  pl.BlockSpec((KV_LORA, MLA_W), const2), pl.BlockSpec((KV_LORA, MLA_W), const2)],
        out_specs=[pl.BlockSpec((tl, MLA_W), lambda i: (i, 0))] * 2,
        out_shape=[jax.ShapeDtypeStruct((n, MLA_W), BF16)] * 2,
        compiler_params=_cp(("parallel",)),
        name="kvcache",
    )(ckv, kr, lp['w_k_pad'], lp['w_v_pad'])


ATT_SUB = 256


def _attn_kernel(*refs, nseg, tq, hps, batched):
    q_ref = refs[0]
    k_refs = refs[1:1 + nseg]
    v_refs = refs[1 + nseg:1 + 2 * nseg]
    o_ref = refs[1 + 2 * nseg]
    def one_head(hh, rows):
        cs = slice(hh * HEAD_PAD, (hh + 1) * HEAD_PAD)
        q = q_ref[rows, cs]
        s = [lax.dot_general(q, k[:, cs], (((1,), (1,)), ((), ())), preferred_element_type=F32) for k in k_refs]
        m = s[0].max(axis=-1, keepdims=True)
        for t in s[1:]:
            m = jnp.maximum(m, t.max(axis=-1, keepdims=True))
        p = [jnp.exp2(t - m) for t in s]
        den = p[0].sum(axis=-1, keepdims=True)
        for t in p[1:]:
            den = den + t.sum(axis=-1, keepdims=True)
        acc = None
        for t, v in zip(p, v_refs):
            part = jnp.dot(t.astype(BF16), v[:, cs], preferred_element_type=F32)
            acc = part if acc is None else acc + part
        return acc / den

    def store_pair(pair, rows, o_even, o_odd):
        o = o_even + pltpu.roll(o_odd, V_HEAD, 1)
        o_ref[rows, pair * HEAD_PAD:(pair + 1) * HEAD_PAD] = o.astype(BF16)

    if batched:
        hsl = [slice(hh * HEAD_PAD, (hh + 1) * HEAD_PAD) for hh in range(hps)]
        q = jnp.stack([q_ref[:, cs] for cs in hsl])
        s = [_bmm_nt(q, jnp.stack([k[:, cs] for cs in hsl])) for k in k_refs]
        m = s[0].max(axis=-1, keepdims=True)
        for t in s[1:]:
            m = jnp.maximum(m, t.max(axis=-1, keepdims=True))
        p = [jnp.exp2(t - m) for t in s]
        den = p[0].sum(axis=-1, keepdims=True)
        for t in p[1:]:
            den = den + t.sum(axis=-1, keepdims=True)
        acc = None
        for t, v in zip(p, v_refs):
            part = _bmm(t, jnp.stack([v[:, cs] for cs in hsl]))
            acc = part if acc is None else acc + part
        o = acc / den
        for pair in range(hps // 2):
            store_pair(pair, slice(0, tq), o[2 * pair], o[2 * pair + 1])
        return

    for pair in range(hps // 2):
        for sb in range(tq // ATT_SUB):
            rows = slice(sb * ATT_SUB, (sb + 1) * ATT_SUB)
            store_pair(pair, rows, one_head(2 * pair, rows), one_head(2 * pair + 1, rows))


def _attention(qh, segs, bsz, seq_len, tq, hps):
    nq = seq_len // tq
    wblk = hps * HEAD_PAD
    in_specs = [pl.BlockSpec((tq, wblk), lambda b, h, i: (b * nq + i, h))]
    args = [qh]
    for which in (0, 1):
        for seg in segs:
            in_specs.append(pl.BlockSpec((seg[2], wblk), lambda b, h, i: (b, h)))
            args.append(seg[which])
    return pl.pallas_call(
        functools.partial(_attn_kernel, nseg=len(segs), tq=tq, hps=hps, batched=hps == H_C),
        grid=(bsz, H_C // hps, nq),
        in_specs=in_specs,
        out_specs=pl.BlockSpec((tq, hps * V_HEAD), lambda b, h, i: (b * nq + i, h)),
        out_shape=jax.ShapeDtypeStruct((bsz * seq_len, H_C * V_HEAD), BF16),
        compiler_params=_cp(("parallel", "parallel", "parallel")),
        name="attention",
    )(*args)


DELTA_TB = 256
DELTA_G = DELTA_TB // CHUNK


def _delta_prep_kernel(qkv_ref, bg_ref, uf_ref, wf_ref, qef_ref, kef_ref, qkf_ref,
                       ub_ref, wb_ref, qeb_ref, keb_ref, qkb_ref, dec_ref):
    c = CHUNK
    outs = ((uf_ref, wf_ref, qef_ref, kef_ref, qkf_ref), (ub_ref, wb_ref, qeb_ref, keb_ref, qkb_ref))
    ri = lax.broadcasted_iota(jnp.int32, (c, c), 0)
    ci = lax.broadcasted_iota(jnp.int32, (c, c), 1)
    tril = (ri >= ci).astype(BF16)
    triu = (ri <= ci).astype(BF16)
    same_blk = (ri // SUB) == (ci // SUB)
    scale = HEAD_DIM_A ** -0.5

    insts = [(g, d, hh) for g in range(DELTA_G) for d in range(2) for hh in range(H_A)]
    n = len(insts)
    qkv = qkv_ref[...]
    bg = bg_ref[...]

    def head_stack(base):
        return jnp.stack([qkv[g * c:(g + 1) * c, base + hh * HEAD_DIM_A:base + (hh + 1) * HEAD_DIM_A]
                          for g, d, hh in insts])

    q, k, v = head_stack(0), head_stack(A_WIDTH), head_stack(2 * A_WIDTH)

    per_slab = LANES // c
    bg_t = [bg[s * LANES:(s + 1) * LANES, :].T for s in range(DELTA_TB // LANES)]
    cols, rws = {}, {}
    for g in range(DELTA_G):
        bgc = bg[g * c:(g + 1) * c, :]
        bgr = bg_t[g // per_slab][0:16, (g % per_slab) * c:(g % per_slab + 1) * c]
        cols[g] = (bgc, _mm_exact_lhs(tril, bgc), _mm_exact_lhs(triu, bgc))
        rws[g] = (_mm_exact_rhs(bgr, triu), _mm_exact_rhs(bgr, tril))
    beta = jnp.stack([cols[g][0][:, d * H_A + hh:d * H_A + hh + 1] for g, d, hh in insts])
    gcc = jnp.stack([cols[g][1 + d][:, 2 * H_A + d * H_A + hh:2 * H_A + d * H_A + hh + 1]
                     for g, d, hh in insts])
    gcr = jnp.stack([rws[g][d][2 * H_A + d * H_A + hh:2 * H_A + d * H_A + hh + 1, :] for g, d, hh in insts])
    glast = jnp.stack([gcc[i, (c - 1 if insts[i][1] == 0 else 0):(c if insts[i][1] == 0 else 1), :]
                       for i in range(n)])

    sign = jnp.stack([jnp.full((1, 1), 1 - 2 * d, jnp.int32) for g, d, hh in insts])
    tri = (ri - ci)[None] * sign
    incl = tri >= 0
    strict = tri > 0
    blk = jnp.broadcast_to(same_blk[None], incl.shape)

    kf = k.astype(F32)
    kb = kf * beta
    decay = jnp.where(incl, jnp.exp(jnp.where(incl, gcc - gcr, 0.0)), 0.0)
    a = _bmm_nt(kb, k) * decay
    qk = _bmm_nt(q, k) * (decay * scale)
    dg = jnp.where(blk & strict, a, 0.0)
    lo = jnp.where(jnp.logical_not(blk) & strict, a, 0.0)
    m1 = _bmm(dg, dg)
    p = m1 - dg - _bmm(dg, m1)
    pw = m1
    for _ in range(int(math.log2(SUB)) - 2):
        pw = _bmm(pw, pw)
        p = p + pw + _bmm(p, pw)
    nm = lo + _bmm(p, lo)
    n2 = _bmm(nm, nm)
    t1 = n2 - nm - _bmm(nm, n2)
    tm = t1 + p + _bmm(t1, p)
    e_col = jnp.exp(gcc)
    rhs = jnp.concatenate([v.astype(F32) * beta, kb * e_col], axis=2)
    x = rhs + _bmm(tm, rhs)
    u, w = x[:, :, :HEAD_DIM_A], x[:, :, HEAD_DIM_A:]
    qe = q.astype(F32) * (e_col * scale)
    ke = kf * jnp.exp(glast - gcc)
    dec = jnp.exp(glast)
    zpad = jnp.zeros((c, HEAD_DIM_A - c), BF16)
    for i, (g, d, hh) in enumerate(insts):
        rows = slice(g * c, (g + 1) * c)
        cs = slice(hh * HEAD_DIM_A, (hh + 1) * HEAD_DIM_A)
        u_ref, w_ref, qe_ref, ke_ref, qk_ref = outs[d]
        u_ref[0, rows, cs] = u[i]
        w_ref[0, rows, cs] = w[i].astype(BF16)
        qe_ref[0, rows, cs] = qe[i].astype(BF16)
        ke_ref[0, rows, cs] = ke[i].astype(BF16)
        qk_ref[0, rows, cs] = jnp.concatenate([qk[i].astype(BF16), zpad], axis=1)
        dec_ref[0, g, d * H_A + hh:d * H_A + hh + 1, :] = jnp.broadcast_to(dec[i], (1, HEAD_DIM_A))


def _delta_prep(qkv, bg, bsz, seq_len):
    nblk = seq_len // DELTA_TB
    nc = seq_len // CHUNK
    tokf = jax.ShapeDtypeStruct((bsz, seq_len, A_WIDTH), F32)
    tokb = jax.ShapeDtypeStruct((bsz, seq_len, A_WIDTH), BF16)
    tspec = pl.BlockSpec((1, DELTA_TB, A_WIDTH), lambda b, j: (b, j, 0))
    return pl.pallas_call(
        _delta_prep_kernel,
        grid=(bsz, nblk),
        in_specs=[pl.BlockSpec((DELTA_TB, 3 * A_WIDTH), lambda b, j: (b * nblk + j, 0)),
                  pl.BlockSpec((DELTA_TB, LANES), lambda b, j: (b * nblk + j, 0))],
        out_specs=[tspec] * 10 + [pl.BlockSpec((1, DELTA_G, 2 * H_A, HEAD_DIM_A), lambda b, j: (b, j, 0, 0))],
        out_shape=[tokf, tokb, tokb, tokb, tokb] * 2 + [jax.ShapeDtypeStruct((bsz, nc, 2 * H_A, HEAD_DIM_A), F32)],
        compiler_params=_cp(("parallel", "parallel")),
        name="delta_prep",
    )(qkv, bg)


def _delta_recur_kernel(*refs, nb, nblk, has_s0, emit_state):
    it = iter(refs)
    fwd = [next(it) for _ in range(5)]
    bwd = [next(it) for _ in range(5)]
    decf_ref, decb_ref = next(it), next(it)
    s0_ref = next(it) if has_s0 else None
    of_ref, ob_ref = next(it), next(it)
    sfin_ref = next(it) if emit_state else None
    s_scr = next(it)
    j = pl.program_id(1)
    c = CHUNK

    @pl.when(j == 0)
    def _():
        if has_s0:
            s_scr[...] = s0_ref[...]
        else:
            s_scr[...] = jnp.zeros(s_scr.shape, F32)

    insts = [(bi, d, hh) for bi in range(nb) for d in range(2) for hh in range(H_A)]
    for step in range(DELTA_G):
        gsel = (step, DELTA_G - 1 - step)

        def stk(idx, width=HEAD_DIM_A):
            return jnp.stack([(fwd, bwd)[d][idx][bi, gsel[d] * c:(gsel[d] + 1) * c,
                                                 hh * HEAD_DIM_A:hh * HEAD_DIM_A + width]
                              for bi, d, hh in insts])

        u, w, qe, ke, qk = stk(0), stk(1), stk(2), stk(3), stk(4, c)
        dec = jnp.stack([(decf_ref, decb_ref)[d][bi, gsel[d], d * H_A + hh:d * H_A + hh + 1, :]
                         for bi, d, hh in insts])
        s_old = jnp.stack([s_scr[bi, d, hh] for bi, d, hh in insts])
        r1 = _bmm(jnp.concatenate([w, qe], axis=1), s_old)
        v_new = (u - r1[:, :c, :]).astype(BF16)
        o = r1[:, c:, :] + _bmm(qk, v_new)
        s_new = s_old * dec + _bmm_tn(ke, v_new)
        for i, (bi, d, hh) in enumerate(insts):
            s_scr[bi, d, hh] = s_new[i]
            (of_ref, ob_ref)[d][bi, gsel[d] * c:(gsel[d] + 1) * c, hh * HEAD_DIM_A:(hh + 1) * HEAD_DIM_A] = o[i]

    if emit_state:
        @pl.when(j == nblk - 1)
        def _():
            sfin_ref[...] = s_scr[...]


def _delta_recur(prep, s0, bsz, seq_len, nb, emit_state):
    nblk = seq_len // DELTA_TB
    has_s0 = s0 is not None
    fspec = pl.BlockSpec((nb, DELTA_TB, A_WIDTH), lambda b, j: (b, j, 0))
    bspec = pl.BlockSpec((nb, DELTA_TB, A_WIDTH), lambda b, j: (b, nblk - 1 - j, 0))
    dspec_f = pl.BlockSpec((nb, DELTA_G, 2 * H_A, HEAD_DIM_A), lambda b, j: (b, j, 0, 0))
    dspec_b = pl.BlockSpec((nb, DELTA_G, 2 * H_A, HEAD_DIM_A), lambda b, j: (b, nblk - 1 - j, 0, 0))
    st_spec = pl.BlockSpec((nb, 2, H_A, HEAD_DIM_A, HEAD_DIM_A), lambda b, j: (b, 0, 0, 0, 0))
    in_specs = [fspec] * 5 + [bspec] * 5 + [dspec_f, dspec_b]
    args = list(prep[:10]) + [prep[10], prep[10]]
    if has_s0:
        in_specs.append(st_spec)
        args.append(s0)
    out_specs = [fspec, bspec]
    out_shape = [jax.ShapeDtypeStruct((bsz, seq_len, A_WIDTH), F32)] * 2
    if emit_state:
        out_specs.append(st_spec)
        out_shape.append(jax.ShapeDtypeStruct((bsz, 2, H_A, HEAD_DIM_A, HEAD_DIM_A), F32))
    return pl.pallas_call(
        functools.partial(_delta_recur_kernel, nb=nb, nblk=nblk, has_s0=has_s0, emit_state=emit_state),
        grid=(bsz // nb, nblk),
        in_specs=in_specs,
        out_specs=out_specs,
        out_shape=out_shape,
        scratch_shapes=[pltpu.VMEM((nb, 2, H_A, HEAD_DIM_A, HEAD_DIM_A), F32)],
        compiler_params=_cp(("parallel", "arbitrary")),
        name="delta_recur",
    )(*args)


def _s5prep_kernel(lre_ref, lim_ref, ldt_ref, bre_ref, bim_ref, cre_ref, cim_ref,
                   m_out, ef_out, eb_out, ff_out, fb_out, lt_out, xcat_scr, qcat_scr, ft_scr):
    gs, p, cg, t8, half = S5_SG_GROUPS, S5_STATE, S5_GROUP, S5_T, S5_K // 2
    j = (lax.broadcasted_iota(jnp.int32, (S5_ROWS, 1), 0) // cg).astype(F32)
    tile = lambda m: jnp.concatenate([m] * t8, axis=0)
    one = jnp.ones((1, 1), F32)
    bd = {}
    for d in range(2):
        e_out = (ef_out, eb_out)[d]
        f_out = (ff_out, fb_out)[d]
        xcat_scr[...] = jnp.zeros(xcat_scr.shape, F32)
        qcat_scr[...] = jnp.zeros(qcat_scr.shape, F32)
        ft_scr[...] = jnp.zeros(ft_scr.shape, F32)
        e_out[0] = jnp.zeros((S5_K, S5_K), BF16)
        for gl in range(gs):
            lre, lim = lre_ref[0, d, 0, gl:gl + 1, :], lim_ref[0, d, 0, gl:gl + 1, :]
            dt = jnp.exp(ldt_ref[0, d, 0, gl:gl + 1, :])
            zr, zi = lre * dt, lim * dt

            def lam_pow(e):
                mag = jnp.exp(e * zr)
                return mag * jnp.cos(e * zi), mag * jnp.sin(e * zi)

            l1r, l1i = lam_pow(one)
            den = lre * lre + lim * lim
            nr, ni = l1r - 1.0, l1i
            cfr = (nr * lre + ni * lim) / den
            cfi = (ni * lre - nr * lim) / den
            bre, bim = bre_ref[0, gl], bim_ref[0, gl]
            bbr = cfr * bre - cfi * bim
            bbi = cfr * bim + cfi * bre
            cre, cim = cre_ref[0, gl], cim_ref[0, gl]
            c_r, c_i = tile(cre), tile(cim)
            b_r, b_i = tile(bbr), tile(bbi)
            grow = slice(gl * cg, (gl + 1) * cg)
            lre_c, lim_c = slice(gl * p, (gl + 1) * p), slice(half + gl * p, half + (gl + 1) * p)

            xcat_scr[grow, 2 * gl * p:(2 * gl + 1) * p] = bbr
            xcat_scr[grow, (2 * gl + 1) * p:(2 * gl + 2) * p] = bbi
            pr, pi = lam_pow(j)
            qr = pr * c_r - pi * c_i
            qi = pr * c_i + pi * c_r
            for jj in range(t8):
                qcat_scr[jj, grow, 2 * gl * p:(2 * gl + 1) * p] = qr[jj * cg:(jj + 1) * cg, :]
                qcat_scr[jj, grow, (2 * gl + 1) * p:(2 * gl + 2) * p] = -qi[jj * cg:(jj + 1) * cg, :]

            er, ei = lam_pow(j if d else (t8 - 1.0) - j)
            e_r = (er * b_r - ei * b_i).astype(BF16)
            e_i = (er * b_i + ei * b_r).astype(BF16)
            fr, fi = lam_pow((t8 - j) if d else j + 1.0)
            f_r = fr * c_r - fi * c_i
            f_i = -(fr * c_i + fi * c_r)
            for s in range(t8):
                rows = slice(s * LANES + gl * cg, s * LANES + (gl + 1) * cg)
                e_out[0, rows, lre_c] = e_r[s * cg:(s + 1) * cg, :]
                e_out[0, rows, lim_c] = e_i[s * cg:(s + 1) * cg, :]
                ft_scr[rows, lre_c] = f_r[s * cg:(s + 1) * cg, :]
                ft_scr[rows, lim_c] = f_i[s * cg:(s + 1) * cg, :]

            ltr, lti = lam_pow(one * float(t8))
            lt_out[0, d:d + 1, lre_c] = ltr
            lt_out[0, d:d + 1, lim_c] = lti
        f_out[0] = ft_scr[...].T.astype(BF16)
        xcat = xcat_scr[...]
        for jj in range(t8):
            bd[(d, jj)] = _mm3_nt(xcat, qcat_scr[jj])
    for s in range(t8):
        for t in range(t8):
            blk = bd[(0, t - s)] if t > s else (bd[(1, s - t)] if s > t else bd[(0, 0)] + bd[(1, 0)])
            m_out[0, s * LANES:(s + 1) * LANES, t * LANES:(t + 1) * LANES] = blk.astype(BF16)


def _s5prep(lam_re, lam_im, log_dt, b_re, b_im, c_re, c_im):
    gs, p, cg = S5_SG_GROUPS, S5_STATE, S5_GROUP
    nq = DEPTH * S5_SG
    lam_idx = lambda q: (q // S5_SG, 0, q % S5_SG, 0, 0)
    par_idx = lambda q: (q, 0, 0, 0)
    lam5 = lambda v, w: v.reshape(DEPTH, 2, S5_SG, gs, w)
    b_t = lambda v: v.reshape(nq, gs, p, cg).transpose(0, 1, 3, 2)
    par4 = lambda v: v.reshape(nq, gs, cg, p)
    mat = pl.BlockSpec((1, S5_K, S5_K), lambda q: (q, 0, 0))
    return pl.pallas_call(
        _s5prep_kernel,
        grid=(nq,),
        in_specs=[pl.BlockSpec((1, 2, 1, gs, p), lam_idx), pl.BlockSpec((1, 2, 1, gs, p), lam_idx),
                  pl.BlockSpec((1, 2, 1, gs, 1), lam_idx)] + [pl.BlockSpec((1, gs, cg, p), par_idx)] * 4,
        out_specs=[mat] * 5 + [pl.BlockSpec((1, 2, S5_K), lambda q: (q, 0, 0))],
        out_shape=[jax.ShapeDtypeStruct((nq, S5_K, S5_K), BF16)] * 5 + [jax.ShapeDtypeStruct((nq, 2, S5_K), F32)],
        scratch_shapes=[pltpu.VMEM((S5_ROWS, S5_K), F32), pltpu.VMEM((S5_T, S5_ROWS, S5_K), F32),
                        pltpu.VMEM((S5_K, S5_K), F32)],
        compiler_params=_cp(("parallel",)),
        name="s5prep",
    )(lam5(lam_re, p), lam5(lam_im, p), lam5(log_dt, 1), b_t(b_re), b_t(b_im), par4(c_re), par4(c_im))


S5_RT = 256


def _s5_kernel(*refs, nchunk, nb, seq_len, has_h0, emit_state):
    it = iter(refs)
    u_ref = next(it)
    m_ref, ef_ref, eb_ref, ff_ref, fb_ref, lt_ref = (next(it) for _ in range(6))
    h0_ref = next(it) if has_h0 else None
    y_ref = next(it)
    hfin_ref = next(it) if emit_state else None
    up_scr, sf_scr, sb_scr = next(it), next(it), next(it)

    r = nchunk * nb
    rtile = min(S5_RT, r)
    half = S5_K // 2
    for t in range(S5_T):
        for b in range(nb):
            up_scr[t, pl.ds(b, nchunk, stride=nb), :] = u_ref[pl.ds(b * seq_len + t, nchunk, stride=S5_T), :]

    def u_tile(rt):
        rows = slice(rt * rtile, (rt + 1) * rtile)
        return jnp.concatenate([up_scr[t, rows, :] for t in range(S5_T)], axis=1).astype(BF16)

    for rt in range(r // rtile):
        rows = slice(rt * rtile, (rt + 1) * rtile)
        ub = u_tile(rt)
        sf_scr[rows, :] = jnp.dot(ub, ef_ref[0], preferred_element_type=F32)
        sb_scr[rows, :] = jnp.dot(ub, eb_ref[0], preferred_element_type=F32)

    lane = lax.broadcasted_iota(jnp.int32, (1, S5_K), 1)

    def coeffs(d):
        lt = lt_ref[0, d:d + 1, :]
        sw = pltpu.roll(lt, half, 1)
        return jnp.where(lane < half, lt, sw), jnp.where(lane < half, -sw, lt)

    af, bf = coeffs(0)
    ab, bb = coeffs(1)
    if has_h0:
        h0f, h0b = h0_ref[0, 0, 0], h0_ref[0, 1, 0]
    else:
        h0f = h0b = jnp.zeros((nb, S5_K), F32)

    tile_rows = max(nb, 8)
    cps = tile_rows // nb
    n_iter = nchunk // cps

    def step(h, a, b, seg):
        return a * h + b * pltpu.roll(h, half, 1) + seg

    def body(i, carry):
        hf, hb = carry
        rf = pl.ds(pl.multiple_of(i * tile_rows, tile_rows), tile_rows)
        rb = pl.ds(pl.multiple_of((n_iter - 1 - i) * tile_rows, tile_rows), tile_rows)
        xf, xb = sf_scr[rf, :], sb_scr[rb, :]
        of, ob = [], [None] * cps
        for jj in range(cps):
            of.append(hf)
            hf = step(hf, af, bf, xf[jj * nb:(jj + 1) * nb, :])
        for jj in reversed(range(cps)):
            ob[jj] = hb
            hb = step(hb, ab, bb, xb[jj * nb:(jj + 1) * nb, :])
        sf_scr[rf, :] = of[0] if cps == 1 else jnp.concatenate(of, axis=0)
        sb_scr[rb, :] = ob[0] if cps == 1 else jnp.concatenate(ob, axis=0)
        return hf, hb

    hf, hb = lax.fori_loop(0, n_iter, body, (h0f, h0b))
    if emit_state:
        hfin_ref[0, 0, 0] = hf
        hfin_ref[0, 1, 0] = hb

    for rt in range(r // rtile):
        rows = slice(rt * rtile, (rt + 1) * rtile)
        y = jnp.dot(u_tile(rt), m_ref[0], preferred_element_type=F32)
        y = y + jnp.dot(sf_scr[rows, :].astype(BF16), ff_ref[0], preferred_element_type=F32)
        y = y + jnp.dot(sb_scr[rows, :].astype(BF16), fb_ref[0], preferred_element_type=F32)
        for t in range(S5_T):
            up_scr[t, rows, :] = y[:, t * LANES:(t + 1) * LANES]
    for t in range(S5_T):
        for b in range(nb):
            y_ref[pl.ds(b * seq_len + t, nchunk, stride=S5_T), :] = up_scr[t, pl.ds(b, nchunk, stride=nb), :]


def _s5(u, mats_l, h0, bsz, seq_len, nb, emit_state):
    nchunk = seq_len // S5_T
    r = nchunk * nb
    has_h0 = h0 is not None
    qi = lambda q, j: (q, 0, 0)
    wspec = pl.BlockSpec((1, S5_K, S5_K), qi)
    in_specs = [pl.BlockSpec((nb * seq_len, LANES), lambda q, j: (j, q))] + [wspec] * 5 \
        + [pl.BlockSpec((1, 2, S5_K), qi)]
    args = [u] + list(mats_l)
    st_spec = pl.BlockSpec((1, 2, 1, nb, S5_K), lambda q, j: (q, 0, j, 0, 0))
    st_shape = (S5_SG, 2, bsz // nb, nb, S5_K)
    if has_h0:
        in_specs.append(st_spec)
        args.append(h0.reshape(st_shape))
    out_specs = [pl.BlockSpec((nb * seq_len, LANES), lambda q, j: (j, q))]
    out_shape = [jax.ShapeDtypeStruct((bsz * seq_len, B_WIDTH), F32)]
    if emit_state:
        out_specs.append(st_spec)
        out_shape.append(jax.ShapeDtypeStruct(st_shape, F32))
    outs = pl.pallas_call(
        functools.partial(_s5_kernel, nchunk=nchunk, nb=nb, seq_len=seq_len, has_h0=has_h0, emit_state=emit_state),
        grid=(S5_SG, bsz // nb),
        in_specs=in_specs,
        out_specs=out_specs,
        out_shape=out_shape,
        scratch_shapes=[pltpu.VMEM((S5_T, r, LANES), F32), pltpu.VMEM((r, S5_K), F32), pltpu.VMEM((r, S5_K), F32)],
        compiler_params=_cp(("parallel", "parallel")),
        name="s5",
    )(*args)
    if emit_state:
        return outs[0], outs[1].reshape(S5_SG, 2, bsz, S5_K)
    return (outs[0],)


def _merge_kernel(x_ref, shift_ref, scale_ref, gate_ref, gnorm_ref, of_ref, ob_ref, zs_ref, gout_ref, ys_ref, u_ref,
                  oc_ref, wg_ref, wba_ref, wbb_ref, wbc_ref, wout_ref, wglu_ref, bglu_ref, dskip_ref, xo_ref):
    x = x_ref[...]
    h = ((_rms(x) * gnorm_ref[...]) * (1.0 + scale_ref[0]) + shift_ref[0]).astype(BF16)
    od = of_ref[...] + ob_ref[...]
    gout = gout_ref[...]
    oa = jnp.concatenate([_rms(od[:, hh * HEAD_DIM_A:(hh + 1) * HEAD_DIM_A]) * gout for hh in range(H_A)], axis=1)
    oa = oa * zs_ref[...]
    yb = jax.nn.gelu(ys_ref[...] + dskip_ref[...] * u_ref[...])
    ob = yb * jax.nn.sigmoid(_mm(yb, wglu_ref[...]) + bglu_ref[...])
    acc = None
    for nbr, (o, w) in enumerate(((oa, wba_ref), (ob, wbb_ref), (oc_ref[...], wbc_ref))):
        gate = jax.nn.sigmoid(jnp.dot(h, wg_ref[:, nbr * D_MODEL:(nbr + 1) * D_MODEL], preferred_element_type=F32))
        term = gate * _mm(o, w[...])
        acc = term if acc is None else acc + term
    out = _mm(acc, wout_ref[...])
    xo_ref[...] = x + gate_ref[0] * out


def _merge(x, mods_rows, row_of_tile, lp, o_f, o_b, zs, ys, u, oc, tl):
    n = x.shape[0]
    const2 = lambda i: (0, 0)
    tok = lambda w: pl.BlockSpec((tl, w), lambda i: (i, 0))
    mod = lambda j: pl.BlockSpec((1, 1, D_MODEL), lambda i: (row_of_tile(i) * 6 + j, 0, 0))
    return pl.pallas_call(
        _merge_kernel,
        grid=(n // tl,),
        in_specs=[tok(D_MODEL), mod(0), mod(1), mod(2), pl.BlockSpec((1, D_MODEL), const2),
                  tok(A_WIDTH), tok(A_WIDTH), tok(A_WIDTH), pl.BlockSpec((1, HEAD_DIM_A), const2),
                  tok(B_WIDTH), tok(B_WIDTH), tok(BRANCH_WIDTH),
                  pl.BlockSpec((D_MODEL, N_BRANCH * D_MODEL), const2),
                  pl.BlockSpec((BRANCH_WIDTH, D_MODEL), const2), pl.BlockSpec((BRANCH_WIDTH, D_MODEL), const2),
                  pl.BlockSpec((BRANCH_WIDTH, D_MODEL), const2), pl.BlockSpec((D_MODEL, D_MODEL), const2),
                  pl.BlockSpec((B_WIDTH, B_WIDTH), const2), pl.BlockSpec((1, B_WIDTH), const2),
                  pl.BlockSpec((1, B_WIDTH), const2)],
        out_specs=tok(D_MODEL),
        out_shape=jax.ShapeDtypeStruct((n, D_MODEL), F32),
        compiler_params=_cp(("parallel",)),
        name="merge",
    )(x, mods_rows, mods_rows, mods_rows, lp['g_norm_mix'], o_f, o_b, zs, lp['g_delta_out'], ys, u, oc,
      lp['w_gates'], lp['w_br_a'], lp['w_br_b'],
      lp['w_br_c'], lp['w_out'], lp['w_glu'], lp['b_glu'], lp['s5_d'])


FF_BLK = 256


def _ffn_kernel(*refs, tl, tiles_per_seq, final):
    it = iter(refs)
    xp_ref, xc_ref, xn_ref = next(it), next(it), next(it)
    shift_ref, scale_ref, gate_ref, gnorm_ref = next(it), next(it), next(it), next(it)
    wup_ref, convw_ref, convb_ref, wdown_ref = next(it), next(it), next(it), next(it)
    gfin_ref = next(it) if final else None
    xo_ref = next(it)
    yo_ref = next(it) if final else None
    act_scr = next(it)

    i = pl.program_id(0)
    pos = i % tiles_per_seq
    mod_scale = 1.0 + scale_ref[0]
    mod_shift = shift_ref[0]
    gain = gnorm_ref[...]

    def norm_mod(x):
        return ((_rms(x) * gain) * mod_scale + mod_shift).astype(BF16)

    x = xc_ref[...]
    h_ext = jnp.concatenate([norm_mod(xp_ref[...]), norm_mod(x), norm_mod(xn_ref[...])], axis=0)
    n_ext = tl + 2 * HALO
    rowi = lax.broadcasted_iota(jnp.int32, (n_ext, 1), 0)
    valid = jnp.logical_and(jnp.logical_or(pos > 0, rowi >= HALO),
                            jnp.logical_or(pos < tiles_per_seq - 1, rowi < HALO + tl)).astype(F32)
    pad = FFN_CONV // 2

    def conv_act(cols):
        up = jnp.dot(h_ext, wup_ref[:, cols], preferred_element_type=F32) * valid
        acc = None
        for t in range(FFN_CONV):
            sh = (pad - t) % n_ext
            src = up if sh == 0 else pltpu.roll(up, sh, 0)
            term = src[HALO:HALO + tl, :] * convw_ref[t:t + 1, cols]
            acc = term if acc is None else acc + term
        return acc + convb_ref[:, cols]

    for j in range(D_FF // FF_BLK):
        gcols = slice(j * FF_BLK, (j + 1) * FF_BLK)
        vcols = slice(D_FF + j * FF_BLK, D_FF + (j + 1) * FF_BLK)
        act_scr[:, gcols] = (_silu(conv_act(gcols)) * conv_act(vcols)).astype(BF16)
    out = jnp.dot(act_scr[...], wdown_ref[...], preferred_element_type=F32)
    xo = x + gate_ref[0] * out
    xo_ref[...] = xo
    if final:
        yo_ref[...] = _rms(xo) * gfin_ref[...]


def _ffn(x, mods_rows, row_of_tile, lp, seq_len, tl, g_final):
    n = x.shape[0]
    tps = seq_len // tl
    final = g_final is not None
    xprev, xnext = _halo_maps(n, tl)
    const2 = lambda i: (0, 0)
    mod = lambda j: pl.BlockSpec((1, 1, D_MODEL), lambda i: (row_of_tile(i) * 6 + j, 0, 0))
    tok = pl.BlockSpec((tl, D_MODEL), lambda i: (i, 0))
    in_specs = [pl.BlockSpec((HALO, D_MODEL), xprev), tok, pl.BlockSpec((HALO, D_MODEL), xnext),
                mod(3), mod(4), mod(5), pl.BlockSpec((1, D_MODEL), const2),
                pl.BlockSpec((D_MODEL, 2 * D_FF), const2), pl.BlockSpec((FFN_CONV, 2 * D_FF), const2),
                pl.BlockSpec((1, 2 * D_FF), const2), pl.BlockSpec((D_FF, D_MODEL), const2)]
    args = [x, x, x, mods_rows, mods_rows, mods_rows, lp['g_norm_ffn'], lp['w_ffn_up'], lp['conv_ffn'],
            lp['b_conv_ffn'], lp['w_ffn_down']]
    out_specs = [tok]
    out_shape = [jax.ShapeDtypeStruct((n, D_MODEL), F32)]
    if final:
        in_specs.append(pl.BlockSpec((1, D_MODEL), const2))
        args.append(g_final)
        out_specs.append(tok)
        out_shape.append(jax.ShapeDtypeStruct((n, D_MODEL), F32))
    return pl.pallas_call(
        functools.partial(_ffn_kernel, tl=tl, tiles_per_seq=tps, final=final),
        grid=(n // tl,),
        in_specs=in_specs,
        out_specs=out_specs,
        out_shape=out_shape,
        scratch_shapes=[pltpu.VMEM((tl, D_FF), BF16)],
        compiler_params=_cp(("parallel",)),
        name="ffn",
    )(*args)


def _pad_heads(w, head_w, n_heads):
    k = w.shape[0]
    w = w.reshape(k, n_heads, head_w)
    w = jnp.pad(w, ((0, 0), (0, 0), (0, HEAD_PAD - head_w)))
    return w.reshape(k, n_heads * HEAD_PAD)


def _layer_params(l, p):
    w_in = p['w_in'][l]
    o = 0
    parts = {}
    for name, wd in (('qkv', 3 * A_WIDTH), ('z', A_WIDTH), ('beta', 2 * H_A), ('alpha', 2 * H_A), ('u', B_WIDTH),
                     ('qa', Q_LORA), ('kva', KV_LORA), ('kr', QK_ROPE), ('gates', N_BRANCH * D_MODEL)):
        parts[name] = w_in[:, o:o + wd]
        o += wd
    small = jnp.concatenate([parts['beta'], parts['alpha'], parts['kr'],
                             jnp.zeros((D_MODEL, LANES - 4 * H_A - QK_ROPE), F32)], axis=1)
    w_rest = jnp.concatenate([parts['z'], parts['u'], parts['qa'], parts['kva'], small], axis=1)
    pad8 = lambda v: jnp.pad(v.reshape(1, 2 * H_A), ((0, 0), (2 * H_A, LANES - 4 * H_A)))
    w_kv = p['w_kv_b'][l].reshape(KV_LORA, H_C, QK_NOPE + V_HEAD)
    w_k = w_kv[:, :, :QK_NOPE].reshape(KV_LORA, H_C * QK_NOPE)
    w_v = w_kv[:, :, QK_NOPE:].reshape(KV_LORA, H_C * V_HEAD)
    row = lambda v: v.reshape(1, -1)
    return {
        'g_norm_mix': row(p['g_norm_mix'][l]), 'g_norm_ffn': row(p['g_norm_ffn'][l]),
        'w_qkv': parts['qkv'].astype(BF16), 'w_rest': w_rest.astype(BF16), 'w_gates': parts['gates'].astype(BF16),
        'conv_qkv': p['conv_qkv'][l],
        'a_log128': pad8(p['a_log'][l]), 'dt_bias128': pad8(p['dt_bias'][l]),
        'g_delta_out': row(p['g_delta_out'][l]),
        'g_q_a': row(p['g_q_a'][l]), 'g_kv_a': row(p['g_kv_a'][l]),
        'w_q_pad': _pad_heads(p['w_q_b'][l], QK_NOPE + QK_ROPE, H_C).astype(BF16),
        'w_k_pad': _pad_heads(w_k, QK_NOPE, H_C).astype(BF16),
        'w_v_pad': _pad_heads(w_v, V_HEAD, H_C).astype(BF16),
        'w_br_a': p['w_branch'][l, 0].astype(BF16), 'w_br_b': p['w_branch'][l, 1].astype(BF16),
        'w_br_c': p['w_branch'][l, 2].astype(BF16),
        'w_out': p['w_out'][l].astype(BF16),
        'w_glu': p['w_glu'][l].astype(BF16), 'b_glu': row(p['b_glu'][l]), 's5_d': row(p['s5_d'][l]),
        'w_ffn_up': p['w_ffn_up'][l].astype(BF16), 'conv_ffn': p['conv_ffn'][l],
        'b_conv_ffn': row(p['b_conv_ffn'][l]), 'w_ffn_down': p['w_ffn_down'][l].astype(BF16),
    }


def _rope_tables(length):
    rows = length // GRID_W
    row = jnp.repeat(jnp.arange(rows, dtype=F32), GRID_W)
    col = (jnp.arange(length) % GRID_W).astype(F32)
    n_freq = QK_ROPE // 4
    inv_freq = 1.0 / (ROPE_BASE ** (jnp.arange(n_freq, dtype=F32) / n_freq))
    ang = jnp.concatenate([row[:, None] * inv_freq, col[:, None] * inv_freq], axis=-1)
    cos, sin = jnp.cos(ang), jnp.sin(ang)
    ones = jnp.ones((length, QK_NOPE), F32)
    zeros = jnp.zeros((length, QK_NOPE), F32)
    tail = jnp.zeros((length, HEAD_PAD - QK_NOPE - QK_ROPE), F32)
    cos_t = jnp.concatenate([ones, cos, cos, tail], axis=1)
    sin_t = jnp.concatenate([zeros, -sin, sin, tail], axis=1)
    return cos_t, sin_t


def _layer(x, bsz, seq_len, mods_rows, row_of_tile, lp, s5mats, ctx, rope_tabs, tl, g_final):
    is_ctx = ctx is None
    outs = _inproj(x, mods_rows, row_of_tile, lp, seq_len, tl, rope_tabs, emit_ctx=is_ctx)
    if is_ctx:
        qkv, zs, u, bg, ckv, kr, qh, kh, vh = outs
    else:
        qkv, zs, u, bg, qh, kh, vh = outs

    prep = _delta_prep(qkv, bg, bsz, seq_len)
    d_out = _delta_recur(prep, None if is_ctx else ctx['s0'], bsz, seq_len, min(bsz, 4), emit_state=is_ctx)
    o_f = d_out[0].reshape(bsz * seq_len, A_WIDTH)
    o_b = d_out[1].reshape(bsz * seq_len, A_WIDTH)

    s5_nb = max(1, min(bsz, (1024 * S5_T) // seq_len))
    s_out = _s5(u, s5mats, None if is_ctx else ctx['h0'], bsz, seq_len, s5_nb, emit_state=is_ctx)
    ys = s_out[0]

    segs = [(kh, vh, seq_len)]
    if not is_ctx:
        segs = [(ctx['kh'], ctx['vh'], ctx['past'])] + segs
    oc = _attention(qh, segs, bsz, seq_len, min(seq_len, 512), H_C if seq_len <= 512 else 2)

    x = _merge(x, mods_rows, row_of_tile, lp, o_f, o_b, zs, ys, u, oc, tl)
    f_out = _ffn(x, mods_rows, row_of_tile, lp, seq_len, tl, g_final)
    extras = None
    if is_ctx:
        hfin = s_out[1]
        hfin = hfin.reshape(S5_SG, 2, bsz, 2, S5_SG_GROUPS, S5_STATE).transpose(3, 2, 1, 0, 4, 5)
        hfin = hfin.reshape(2, bsz, 2, S5_GROUPS, S5_STATE)
        extras = (d_out[2], hfin[0], hfin[1], ckv.reshape(bsz, seq_len, KV_LORA), kr.reshape(bsz, seq_len, QK_ROPE))
    return f_out, extras


def kernel(x_prompt, x_sample, state_delta, state_s5_re, state_s5_im, cache_ckv, cache_krope, c, c_ctx, w_mod, b_mod, g_norm_mix, g_norm_ffn, w_in, conv_qkv, a_log, dt_bias, g_delta_out, s5_lam_re, s5_lam_im, s5_log_dt, s5_b_re, s5_b_im, s5_c_re, s5_c_im, s5_d, w_glu, b_glu, g_q_a, w_q_b, g_kv_a, w_kv_b, w_branch, w_out, w_ffn_up, conv_ffn, b_conv_ffn, w_ffn_down, g_final):
    p = dict(g_norm_mix=g_norm_mix, g_norm_ffn=g_norm_ffn, w_in=w_in, conv_qkv=conv_qkv, a_log=a_log,
             dt_bias=dt_bias, g_delta_out=g_delta_out, s5_d=s5_d, w_glu=w_glu, b_glu=b_glu, g_q_a=g_q_a,
             w_q_b=w_q_b, g_kv_a=g_kv_a, w_kv_b=w_kv_b, w_branch=w_branch, w_out=w_out, w_ffn_up=w_ffn_up,
             conv_ffn=conv_ffn, b_conv_ffn=b_conv_ffn, w_ffn_down=w_ffn_down)
    bp, lp_len, _ = x_prompt.shape
    bs, ls_len, _ = x_sample.shape
    past = cache_ckv.shape[2]
    depth = w_in.shape[0]

    mod_rows = 16
    cvec = jnp.concatenate([c_ctx[None, :], c, jnp.zeros((mod_rows - 1 - bs, D_MODEL), F32)], axis=0).astype(F32)
    mods = _modulation(cvec, w_mod, b_mod)

    s5_all = _s5prep(s5_lam_re, s5_lam_im, s5_log_dt, s5_b_re, s5_b_im, s5_c_re, s5_c_im)

    rope_tabs = _rope_tables(ls_len)
    tl_p = min(lp_len, 512)
    tl_s = min(ls_len, 512)
    tiles_per_seq_s = ls_len // tl_s

    def sg_state(re, im):
        t = jnp.stack([re, im], axis=0).astype(F32).reshape(2, bs, 2, S5_SG, S5_SG_GROUPS, S5_STATE)
        return t.transpose(3, 2, 1, 0, 4, 5).reshape(S5_SG, 2, bs, S5_K)

    xp = x_prompt.astype(F32).reshape(bp * lp_len, D_MODEL)
    xs = x_sample.astype(F32).reshape(bs * ls_len, D_MODEL)
    deltas, s5_res, s5_ims, ckvs, kropes = [], [], [], [], []
    yp = ys = None
    for l in range(depth):
        lp = _layer_params(l, p)
        mods_rows = mods[l].reshape(mod_rows * 6, 1, D_MODEL)
        sl = slice(l * S5_SG, (l + 1) * S5_SG)
        s5mats = tuple(m[sl] for m in s5_all)
        last = l == depth - 1
        gfin = g_final.reshape(1, D_MODEL) if last else None

        out_p, extras = _layer(xp, bp, lp_len, mods_rows, lambda i: 0, lp, s5mats, None, None, tl_p, gfin)
        deltas.append(extras[0])
        s5_res.append(extras[1])
        s5_ims.append(extras[2])
        ckvs.append(extras[3])
        kropes.append(extras[4])

        kh_c, vh_c = _kvcache(cache_ckv[:, l].astype(F32).reshape(bs * past, KV_LORA),
                              cache_krope[:, l].astype(F32).reshape(bs * past, QK_ROPE), lp, past)
        ctx = dict(s0=state_delta[:, l].astype(F32), h0=sg_state(state_s5_re[:, l], state_s5_im[:, l]),
                   kh=kh_c, vh=vh_c, past=past)
        out_s, _ = _layer(xs, bs, ls_len, mods_rows, lambda i: 1 + i // tiles_per_seq_s, lp, s5mats, ctx, rope_tabs,
                          tl_s, gfin)
        xp, xs = out_p[0], out_s[0]
        if last:
            yp, ys = out_p[1], out_s[1]

    y_prompt = yp.reshape(bp, lp_len, D_MODEL)
    y_sample = ys.reshape(bs, ls_len, D_MODEL)
    return (y_prompt, y_sample, jnp.stack(deltas, axis=1), jnp.stack(s5_res, axis=1), jnp.stack(s5_ims, axis=1),
            jnp.stack(ckvs, axis=1), jnp.stack(kropes, axis=1))
```

```python
import functools
import math

import jax
import jax.numpy as jnp
from jax import lax
from jax.experimental import pallas as pl
from jax.experimental.pallas import tpu as pltpu

F32 = jnp.float32
BF16 = jnp.bfloat16

D_MODEL = 1024
DEPTH = 4
GRID_W = 64
H_A = 4
HEAD_DIM_A = 128
A_WIDTH = H_A * HEAD_DIM_A
SHORT_CONV = 5
CHUNK = 64
S5_GROUP = 16
S5_STATE = 64
B_WIDTH = 512
S5_GROUPS = B_WIDTH // S5_GROUP
H_C = 8
QK_NOPE = 64
QK_ROPE = 32
V_HEAD = 64
Q_LORA = 384
KV_LORA = 256
ROPE_BASE = 10000.0
N_BRANCH = 3
BRANCH_WIDTH = 512
D_FF = 2816
FFN_CONV = 3
NORM_EPS = 1e-6

LANES = 128
HEAD_PAD = LANES
SUB = 16
S5_T = 8
S5_ROWS = S5_T * S5_GROUP
S5_SG_GROUPS = LANES // S5_GROUP
S5_SG = S5_GROUPS // S5_SG_GROUPS
S5_K = S5_T * LANES

VMEM_LIMIT = 56 * 1024 * 1024
HALO = 16
TOKEN_TILE = 512
ATT_TQ = 512
ATT_HEADS_LONG = 4
DELTA_NB = 4
S5_STEP_ROWS = 1024
MOD_ROWS = 16


def _cp(sem):
    return pltpu.CompilerParams(dimension_semantics=sem, vmem_limit_bytes=VMEM_LIMIT)


def _mm(a, b):
    return jnp.dot(a.astype(BF16), b.astype(BF16), preferred_element_type=F32)


def _bmm(a, b):
    return lax.dot_general(a.astype(BF16), b.astype(BF16), (((2,), (1,)), ((0,), (0,))),
                           preferred_element_type=F32)


def _bmm_nt(a, b):
    return lax.dot_general(a.astype(BF16), b.astype(BF16), (((2,), (2,)), ((0,), (0,))),
                           preferred_element_type=F32)


def _bmm_tn(a, b):
    return lax.dot_general(a.astype(BF16), b.astype(BF16), (((1,), (1,)), ((0,), (0,))),
                           preferred_element_type=F32)


def _split3(x):
    x1 = x.astype(BF16)
    r = x - x1.astype(F32)
    x2 = r.astype(BF16)
    x3 = (r - x2.astype(F32)).astype(BF16)
    return x1, x2, x3


def _mm_exact_rhs(a, b_bf16):
    a1, a2, a3 = _split3(a)
    d = lambda t: jnp.dot(t, b_bf16, preferred_element_type=F32)
    return d(a1) + d(a2) + d(a3)


def _mm_exact_lhs(a_bf16, b):
    b1, b2, b3 = _split3(b)
    d = lambda t: jnp.dot(a_bf16, t, preferred_element_type=F32)
    return d(b1) + d(b2) + d(b3)


def _mm3_nt(a, b):
    a1 = a.astype(BF16)
    a2 = (a - a1.astype(F32)).astype(BF16)
    b1 = b.astype(BF16)
    b2 = (b - b1.astype(F32)).astype(BF16)
    d = lambda s, t: lax.dot_general(s, t, (((1,), (1,)), ((), ())), preferred_element_type=F32)
    return d(a1, b1) + d(a1, b2) + d(a2, b1)


def _silu(x):
    return x * jax.nn.sigmoid(x)


def _rms(x):
    return x * lax.rsqrt(jnp.mean(x * x, axis=-1, keepdims=True) + NORM_EPS)


def _mod_kernel(c_ref, w_ref, b_ref, o_ref):
    s = _silu(c_ref[...])
    o_ref[0] = _mm(s, w_ref[0]) + b_ref[0]


def _modulation(cvec, w_mod, b_mod):
    rows = cvec.shape[0]
    nblk = w_mod.shape[-1] // D_MODEL
    return pl.pallas_call(
        _mod_kernel,
        grid=(DEPTH, nblk),
        in_specs=[
            pl.BlockSpec((rows, D_MODEL), lambda l, j: (0, 0)),
            pl.BlockSpec((1, D_MODEL, D_MODEL), lambda l, j: (l, 0, j)),
            pl.BlockSpec((1, 1, D_MODEL), lambda l, j: (l, 0, j)),
        ],
        out_specs=pl.BlockSpec((1, rows, D_MODEL), lambda l, j: (l, 0, j)),
        out_shape=jax.ShapeDtypeStruct((DEPTH, rows, w_mod.shape[-1]), F32),
        compiler_params=_cp(("parallel", "parallel")),
        name="modulation",
    )(cvec, w_mod, b_mod.reshape(DEPTH, 1, -1))


REST_W = A_WIDTH + B_WIDTH + Q_LORA + KV_LORA + LANES
OFF_Z, OFF_U, OFF_QA = 0, A_WIDTH, A_WIDTH + B_WIDTH
OFF_KVA, OFF_SM = OFF_QA + Q_LORA, OFF_QA + Q_LORA + KV_LORA
SM_KR = 16
MLA_W = H_C * HEAD_PAD
INPROJ_SUB = 256


def _rope_apply(x, cos, sin):
    lane = lax.broadcasted_iota(jnp.int32, x.shape, 1)
    partner = jnp.where(lane < QK_NOPE + QK_ROPE // 2,
                        pltpu.roll(x, HEAD_PAD - QK_ROPE // 2, 1),
                        pltpu.roll(x, QK_ROPE // 2, 1))
    return x * cos + partner * sin


def _inproj_kernel(*refs, tl, tiles_per_seq, rope, emit_ctx):
    it = iter(refs)
    xp_ref, xc_ref, xn_ref = next(it), next(it), next(it)
    shift_ref, scale_ref, gnorm_ref = next(it), next(it), next(it)
    wqkv_ref, convw_ref, wrest_ref = next(it), next(it), next(it)
    gqa_ref, gkva_ref, alog_ref, dtb_ref = next(it), next(it), next(it), next(it)
    wq_ref, wk_ref, wv_ref = next(it), next(it), next(it)
    cos_ref = sin_ref = None
    if rope:
        cos_ref, sin_ref = next(it), next(it)
    qkv_out, zs_out, u_out, bg_out = next(it), next(it), next(it), next(it)
    ckv_out = kr_out = None
    if emit_ctx:
        ckv_out, kr_out = next(it), next(it)
    qh_out, kh_out, vh_out = next(it), next(it), next(it)

    i = pl.program_id(0)
    pos = i % tiles_per_seq
    mod_scale = 1.0 + scale_ref[0]
    mod_shift = shift_ref[0]
    gain = gnorm_ref[...]

    def norm_mod(x):
        return ((_rms(x) * gain) * mod_scale + mod_shift).astype(BF16)

    nsub = tl // INPROJ_SUB
    ts = INPROJ_SUB
    n_ext = ts + 2 * HALO
    rowi = lax.broadcasted_iota(jnp.int32, (n_ext, 1), 0)
    pad = SHORT_CONV // 2
    qscale = (QK_NOPE + QK_ROPE) ** -0.5 * math.log2(math.e)
    for sub in range(nsub):
        r0 = sub * ts
        rows = slice(r0, r0 + ts)
        h_cur = norm_mod(xc_ref[rows, :])
        h_before = norm_mod(xp_ref[...] if sub == 0 else xc_ref[r0 - HALO:r0, :])
        h_after = norm_mod(xn_ref[...] if sub == nsub - 1 else xc_ref[r0 + ts:r0 + ts + HALO, :])
        h_ext = jnp.concatenate([h_before, h_cur, h_after], axis=0)

        qkv_ext = _mm(h_ext, wqkv_ref[...])
        if sub == 0 or sub == nsub - 1:
            ok_lo = jnp.logical_or(pos > 0, rowi >= HALO) if sub == 0 else True
            ok_hi = jnp.logical_or(pos < tiles_per_seq - 1, rowi < HALO + ts) if sub == nsub - 1 else True
            qkv_ext = qkv_ext * jnp.logical_and(ok_lo, ok_hi).astype(F32)
        for blk in range(3):
            cols = slice(blk * A_WIDTH, (blk + 1) * A_WIDTH)
            xblk = qkv_ext[:, cols]
            acc = None
            for t in range(SHORT_CONV):
                sh = (pad - t) % n_ext
                src = xblk if sh == 0 else pltpu.roll(xblk, sh, 0)
                term = src[HALO:HALO + ts, :] * convw_ref[t:t + 1, cols]
                acc = term if acc is None else acc + term
            acc = _silu(acc)
            if blk < 2:
                parts = []
                for hh in range(H_A):
                    a = acc[:, hh * HEAD_DIM_A:(hh + 1) * HEAD_DIM_A]
                    parts.append(a * lax.rsqrt(jnp.sum(a * a, axis=-1, keepdims=True) + NORM_EPS))
                acc = jnp.concatenate(parts, axis=1)
            qkv_out[rows, cols] = acc.astype(BF16)

        rest = _mm(h_cur, wrest_ref[...])
        zs_out[rows, :] = _silu(rest[:, OFF_Z:OFF_Z + A_WIDTH])
        u_out[rows, :] = rest[:, OFF_U:OFF_U + B_WIDTH]

        small = rest[:, OFF_SM:OFF_SM + LANES]
        lane = lax.broadcasted_iota(jnp.int32, small.shape, 1)
        beta = jax.nn.sigmoid(small)
        glog = -jnp.exp(alog_ref[...]) * jax.nn.softplus(small + dtb_ref[...])
        bg_out[rows, :] = jnp.where(lane < 2 * H_A, beta, glog)

        qa = _rms(rest[:, OFF_QA:OFF_QA + Q_LORA]) * gqa_ref[...]
        ckv = _rms(rest[:, OFF_KVA:OFF_KVA + KV_LORA]) * gkva_ref[...]
        if emit_ctx:
            ckv_out[rows, :] = ckv
            kr_out[rows, :] = small[:, SM_KR:SM_KR + QK_ROPE]
        ckv_b = ckv.astype(BF16)
        qhat = _mm(qa, wq_ref[...])
        khat = _mm(ckv_b, wk_ref[...])
        vh_out[rows, :] = _mm(ckv_b, wv_ref[...]).astype(BF16)
        kr_al = jnp.where((lane >= QK_NOPE) & (lane < QK_NOPE + QK_ROPE),
                          pltpu.roll(small, QK_NOPE - SM_KR, 1), 0.0)
        if rope:
            cos, sin = cos_ref[rows, :], sin_ref[rows, :]
            kr_al = _rope_apply(kr_al, cos, sin)
        for hh in range(H_C):
            cs = slice(hh * HEAD_PAD, (hh + 1) * HEAD_PAD)
            qh = qhat[:, cs]
            if rope:
                qh = _rope_apply(qh, cos, sin)
            qh_out[rows, cs] = (qh * qscale).astype(BF16)
            kh_out[rows, cs] = (khat[:, cs] + kr_al).astype(BF16)


def _halo_maps(n, tl):
    nh = n // HALO

    def xprev(i):
        return (jnp.maximum(i * (tl // HALO) - 1, 0), 0)

    def xnext(i):
        return (jnp.minimum((i + 1) * (tl // HALO), nh - 1), 0)

    return xprev, xnext


def _inproj(x, mods_rows, row_of_tile, lp, seq_len, tl, rope_tabs, emit_ctx):
    n = x.shape[0]
    nt = n // tl
    tps = seq_len // tl
    rope = rope_tabs is not None
    xprev, xnext = _halo_maps(n, tl)
    const2 = lambda i: (0, 0)
    in_specs = [
        pl.BlockSpec((HALO, D_MODEL), xprev),
        pl.BlockSpec((tl, D_MODEL), lambda i: (i, 0)),
        pl.BlockSpec((HALO, D_MODEL), xnext),
        pl.BlockSpec((1, 1, D_MODEL), lambda i: (row_of_tile(i) * 6 + 0, 0, 0)),
        pl.BlockSpec((1, 1, D_MODEL), lambda i: (row_of_tile(i) * 6 + 1, 0, 0)),
        pl.BlockSpec((1, D_MODEL), const2),
        pl.BlockSpec((D_MODEL, 3 * A_WIDTH), const2),
        pl.BlockSpec((SHORT_CONV, 3 * A_WIDTH), const2),
        pl.BlockSpec((D_MODEL, REST_W), const2),
        pl.BlockSpec((1, Q_LORA), const2),
        pl.BlockSpec((1, KV_LORA), const2),
        pl.BlockSpec((1, LANES), const2),
        pl.BlockSpec((1, LANES), const2),
        pl.BlockSpec((Q_LORA, MLA_W), const2),
        pl.BlockSpec((KV_LORA, MLA_W), const2),
        pl.BlockSpec((KV_LORA, MLA_W), const2),
    ]
    args = [x, x, x, mods_rows, mods_rows, lp['g_norm_mix'], lp['w_qkv'], lp['conv_qkv'], lp['w_rest'],
            lp['g_q_a'], lp['g_kv_a'], lp['a_log128'], lp['dt_bias128'], lp['w_q_pad'], lp['w_k_pad'], lp['w_v_pad']]
    if rope:
        in_specs += [pl.BlockSpec((tl, HEAD_PAD), lambda i: (i % tps, 0))] * 2
        args += list(rope_tabs)
    tok = lambda w: pl.BlockSpec((tl, w), lambda i: (i, 0))
    out_specs = [tok(3 * A_WIDTH), tok(A_WIDTH), tok(B_WIDTH), tok(LANES)]
    out_shape = [jax.ShapeDtypeStruct((n, 3 * A_WIDTH), BF16), jax.ShapeDtypeStruct((n, A_WIDTH), F32),
                 jax.ShapeDtypeStruct((n, B_WIDTH), F32), jax.ShapeDtypeStruct((n, LANES), F32)]
    if emit_ctx:
        out_specs += [tok(KV_LORA), tok(QK_ROPE)]
        out_shape += [jax.ShapeDtypeStruct((n, KV_LORA), F32), jax.ShapeDtypeStruct((n, QK_ROPE), F32)]
    out_specs += [tok(MLA_W)] * 3
    out_shape += [jax.ShapeDtypeStruct((n, MLA_W), BF16)] * 3
    return pl.pallas_call(
        functools.partial(_inproj_kernel, tl=tl, tiles_per_seq=tps, rope=rope, emit_ctx=emit_ctx),
        grid=(nt,),
        in_specs=in_specs,
        out_specs=out_specs,
        out_shape=out_shape,
        compiler_params=_cp(("parallel",)),
        name="inproj",
    )(*args)


def _kvcache_kernel(ckv_ref, kr_ref, wk_ref, wv_ref, kh_out, vh_out):
    ckv_b = ckv_ref[...].astype(BF16)
    khat = _mm(ckv_b, wk_ref[...])
    vh_out[...] = _mm(ckv_b, wv_ref[...]).astype(BF16)
    kr_al = kr_ref[...]
    for hh in range(H_C):
        cs = slice(hh * HEAD_PAD, (hh + 1) * HEAD_PAD)
        kh_out[:, cs] = (khat[:, cs] + kr_al).astype(BF16)


def _kvcache(ckv, kr, lp, tl):
    n = ckv.shape[0]
    const2 = lambda i: (0, 0)
    kr = jnp.pad(kr, ((0, 0), (QK_NOPE, HEAD_PAD - QK_NOPE - QK_ROPE)))
    return pl.pallas_call(
        _kvcache_kernel,
        grid=(n // tl,),
        in_specs=[pl.BlockSpec((tl, KV_LORA), lambda i: (i, 0)), pl.BlockSpec((tl, HEAD_PAD), lambda i: (i, 0)),
                  pl.BlockSpec((KV_LORA, MLA_W), const2), pl.BlockSpec((KV_LORA, MLA_W), const2)],
        out_specs=[pl.BlockSpec((tl, MLA_W), lambda i: (i, 0))] * 2,
        out_shape=[jax.ShapeDtypeStruct((n, MLA_W), BF16)] * 2,
        compiler_params=_cp(("parallel",)),
        name="kvcache",
    )(ckv, kr, lp['w_k_pad'], lp['w_v_pad'])


ATT_SUB = 256
ATT_BATCHED_MAX = 512


def _attn_kernel(*refs, nseg, tq, hps, batched):
    q_ref = refs[0]
    k_refs = refs[1:1 + nseg]
    v_refs = refs[1 + nseg:1 + 2 * nseg]
    o_ref = refs[1 + 2 * nseg]
    def one_head(hh, rows):
        cs = slice(hh * HEAD_PAD, (hh + 1) * HEAD_PAD)
        q = q_ref[rows, cs]
        s = [lax.dot_general(q, k[:, cs], (((1,), (1,)), ((), ())), preferred_element_type=F32) for k in k_refs]
        m = s[0].max(axis=-1, keepdims=True)
        for t in s[1:]:
            m = jnp.maximum(m, t.max(axis=-1, keepdims=True))
        p = [jnp.exp2(t - m) for t in s]
        den = p[0].sum(axis=-1, keepdims=True)
        for t in p[1:]:
            den = den + t.sum(axis=-1, keepdims=True)
        acc = None
        for t, v in zip(p, v_refs):
            part = jnp.dot(t.astype(BF16), v[:, cs], preferred_element_type=F32)
            acc = part if acc is None else acc + part
        return acc / den

    def store_pair(pair, rows, o_even, o_odd):
        o = o_even + pltpu.roll(o_odd, V_HEAD, 1)
        o_ref[rows, pair * HEAD_PAD:(pair + 1) * HEAD_PAD] = o.astype(BF16)

    if batched:
        hsl = [slice(hh * HEAD_PAD, (hh + 1) * HEAD_PAD) for hh in range(hps)]
        q = jnp.stack([q_ref[:, cs] for cs in hsl])
        s = [_bmm_nt(q, jnp.stack([k[:, cs] for cs in hsl])) for k in k_refs]
        m = s[0].max(axis=-1, keepdims=True)
        for t in s[1:]:
            m = jnp.maximum(m, t.max(axis=-1, keepdims=True))
        p = [jnp.exp2(t - m) for t in s]
        den = p[0].sum(axis=-1, keepdims=True)
        for t in p[1:]:
            den = den + t.sum(axis=-1, keepdims=True)
        acc = None
        for t, v in zip(p, v_refs):
            part = _bmm(t, jnp.stack([v[:, cs] for cs in hsl]))
            acc = part if acc is None else acc + part
        o = acc / den
        for pair in range(hps // 2):
            store_pair(pair, slice(0, tq), o[2 * pair], o[2 * pair + 1])
        return

    for pair in range(hps // 2):
        for sb in range(tq // ATT_SUB):
            rows = slice(sb * ATT_SUB, (sb + 1) * ATT_SUB)
            store_pair(pair, rows, one_head(2 * pair, rows), one_head(2 * pair + 1, rows))


def _attention(qh, segs, bsz, seq_len, tq, hps):
    nq = seq_len // tq
    wblk = hps * HEAD_PAD
    in_specs = [pl.BlockSpec((tq, wblk), lambda b, h, i: (b * nq + i, h))]
    args = [qh]
    for which in (0, 1):
        for seg in segs:
            in_specs.append(pl.BlockSpec((seg[2], wblk), lambda b, h, i: (b, h)))
            args.append(seg[which])
    return pl.pallas_call(
        functools.partial(_attn_kernel, nseg=len(segs), tq=tq, hps=hps, batched=hps == H_C),
        grid=(bsz, H_C // hps, nq),
        in_specs=in_specs,
        out_specs=pl.BlockSpec((tq, hps * V_HEAD), lambda b, h, i: (b * nq + i, h)),
        out_shape=jax.ShapeDtypeStruct((bsz * seq_len, H_C * V_HEAD), BF16),
        compiler_params=_cp(("parallel", "parallel", "parallel")),
        name="attention",
    )(*args)


DELTA_TB = 256
DELTA_G = DELTA_TB // CHUNK


def _delta_prep_kernel(qkv_ref, bg_ref, uf_ref, wf_ref, qef_ref, kef_ref, qkf_ref,
                       ub_ref, wb_ref, qeb_ref, keb_ref, qkb_ref, dec_ref):
    c = CHUNK
    outs = ((uf_ref, wf_ref, qef_ref, kef_ref, qkf_ref), (ub_ref, wb_ref, qeb_ref, keb_ref, qkb_ref))
    ri = lax.broadcasted_iota(jnp.int32, (c, c), 0)
    ci = lax.broadcasted_iota(jnp.int32, (c, c), 1)
    tril = (ri >= ci).astype(BF16)
    triu = (ri <= ci).astype(BF16)
    same_blk = (ri // SUB) == (ci // SUB)
    scale = HEAD_DIM_A ** -0.5

    insts = [(g, d, hh) for g in range(DELTA_G) for d in range(2) for hh in range(H_A)]
    n = len(insts)
    qkv = qkv_ref[...]
    bg = bg_ref[...]

    def head_stack(base):
        return jnp.stack([qkv[g * c:(g + 1) * c, base + hh * HEAD_DIM_A:base + (hh + 1) * HEAD_DIM_A]
                          for g, d, hh in insts])

    q, k, v = head_stack(0), head_stack(A_WIDTH), head_stack(2 * A_WIDTH)

    per_slab = LANES // c
    bg_t = [bg[s * LANES:(s + 1) * LANES, :].T for s in range(DELTA_TB // LANES)]
    cols, rws = {}, {}
    for g in range(DELTA_G):
        bgc = bg[g * c:(g + 1) * c, :]
        bgr = bg_t[g // per_slab][0:16, (g % per_slab) * c:(g % per_slab + 1) * c]
        cols[g] = (bgc, _mm_exact_lhs(tril, bgc), _mm_exact_lhs(triu, bgc))
        rws[g] = (_mm_exact_rhs(bgr, triu), _mm_exact_rhs(bgr, tril))
    beta = jnp.stack([cols[g][0][:, d * H_A + hh:d * H_A + hh + 1] for g, d, hh in insts])
    gcc = jnp.stack([cols[g][1 + d][:, 2 * H_A + d * H_A + hh:2 * H_A + d * H_A + hh + 1]
                     for g, d, hh in insts])
    gcr = jnp.stack([rws[g][d][2 * H_A + d * H_A + hh:2 * H_A + d * H_A + hh + 1, :] for g, d, hh in insts])
    glast = jnp.stack([gcc[i, (c - 1 if insts[i][1] == 0 else 0):(c if insts[i][1] == 0 else 1), :]
                       for i in range(n)])

    sign = jnp.stack([jnp.full((1, 1), 1 - 2 * d, jnp.int32) for g, d, hh in insts])
    tri = (ri - ci)[None] * sign
    incl = tri >= 0
    strict = tri > 0
    blk = jnp.broadcast_to(same_blk[None], incl.shape)

    kf = k.astype(F32)
    kb = kf * beta
    decay = jnp.where(incl, jnp.exp(jnp.where(incl, gcc - gcr, 0.0)), 0.0)
    a = _bmm_nt(kb, k) * decay
    qk = _bmm_nt(q, k) * (decay * scale)
    dg = jnp.where(blk & strict, a, 0.0)
    lo = jnp.where(jnp.logical_not(blk) & strict, a, 0.0)
    m1 = _bmm(dg, dg)
    p = m1 - dg - _bmm(dg, m1)
    pw = m1
    for _ in range(int(math.log2(SUB)) - 2):
        pw = _bmm(pw, pw)
        p = p + pw + _bmm(p, pw)
    nm = lo + _bmm(p, lo)
    n2 = _bmm(nm, nm)
    t1 = n2 - nm - _bmm(nm, n2)
    tm = t1 + p + _bmm(t1, p)
    e_col = jnp.exp(gcc)
    rhs = jnp.concatenate([v.astype(F32) * beta, kb * e_col], axis=2)
    x = rhs + _bmm(tm, rhs)
    u, w = x[:, :, :HEAD_DIM_A], x[:, :, HEAD_DIM_A:]
    qe = q.astype(F32) * (e_col * scale)
    ke = kf * jnp.exp(glast - gcc)
    dec = jnp.exp(glast)
    zpad = jnp.zeros((c, HEAD_DIM_A - c), BF16)
    for i, (g, d, hh) in enumerate(insts):
        rows = slice(g * c, (g + 1) * c)
        cs = slice(hh * HEAD_DIM_A, (hh + 1) * HEAD_DIM_A)
        u_ref, w_ref, qe_ref, ke_ref, qk_ref = outs[d]
        u_ref[0, rows, cs] = u[i]
        w_ref[0, rows, cs] = w[i].astype(BF16)
        qe_ref[0, rows, cs] = qe[i].astype(BF16)
        ke_ref[0, rows, cs] = ke[i].astype(BF16)
        qk_ref[0, rows, cs] = jnp.concatenate([qk[i].astype(BF16), zpad], axis=1)
        dec_ref[0, g, d * H_A + hh:d * H_A + hh + 1, :] = jnp.broadcast_to(dec[i], (1, HEAD_DIM_A))


def _delta_prep(qkv, bg, bsz, seq_len):
    nblk = seq_len // DELTA_TB
    nc = seq_len // CHUNK
    tokf = jax.ShapeDtypeStruct((bsz, seq_len, A_WIDTH), F32)
    tokb = jax.ShapeDtypeStruct((bsz, seq_len, A_WIDTH), BF16)
    tspec = pl.BlockSpec((1, DELTA_TB, A_WIDTH), lambda b, j: (b, j, 0))
    return pl.pallas_call(
        _delta_prep_kernel,
        grid=(bsz, nblk),
        in_specs=[pl.BlockSpec((DELTA_TB, 3 * A_WIDTH), lambda b, j: (b * nblk + j, 0)),
                  pl.BlockSpec((DELTA_TB, LANES), lambda b, j: (b * nblk + j, 0))],
        out_specs=[tspec] * 10 + [pl.BlockSpec((1, DELTA_G, 2 * H_A, HEAD_DIM_A), lambda b, j: (b, j, 0, 0))],
        out_shape=[tokf, tokb, tokb, tokb, tokb] * 2 + [jax.ShapeDtypeStruct((bsz, nc, 2 * H_A, HEAD_DIM_A), F32)],
        compiler_params=_cp(("parallel", "parallel")),
        name="delta_prep",
    )(qkv, bg)


def _delta_recur_kernel(*refs, nb, nblk, has_s0, emit_state):
    it = iter(refs)
    fwd = [next(it) for _ in range(5)]
    bwd = [next(it) for _ in range(5)]
    decf_ref, decb_ref = next(it), next(it)
    s0_ref = next(it) if has_s0 else None
    of_ref, ob_ref = next(it), next(it)
    sfin_ref = next(it) if emit_state else None
    s_scr = next(it)
    j = pl.program_id(1)
    c = CHUNK

    @pl.when(j == 0)
    def _():
        if has_s0:
            s_scr[...] = s0_ref[...]
        else:
            s_scr[...] = jnp.zeros(s_scr.shape, F32)

    insts = [(bi, d, hh) for bi in range(nb) for d in range(2) for hh in range(H_A)]
    for step in range(DELTA_G):
        gsel = (step, DELTA_G - 1 - step)

        def stk(idx, width=HEAD_DIM_A):
            return jnp.stack([(fwd, bwd)[d][idx][bi, gsel[d] * c:(gsel[d] + 1) * c,
                                                 hh * HEAD_DIM_A:hh * HEAD_DIM_A + width]
                              for bi, d, hh in insts])

        u, w, qe, ke, qk = stk(0), stk(1), stk(2), stk(3), stk(4, c)
        dec = jnp.stack([(decf_ref, decb_ref)[d][bi, gsel[d], d * H_A + hh:d * H_A + hh + 1, :]
                         for bi, d, hh in insts])
        s_old = jnp.stack([s_scr[bi, d, hh] for bi, d, hh in insts])
        r1 = _bmm(jnp.concatenate([w, qe], axis=1), s_old)
        v_new = (u - r1[:, :c, :]).astype(BF16)
        o = r1[:, c:, :] + _bmm(qk, v_new)
        s_new = s_old * dec + _bmm_tn(ke, v_new)
        for i, (bi, d, hh) in enumerate(insts):
            s_scr[bi, d, hh] = s_new[i]
            (of_ref, ob_ref)[d][bi, gsel[d] * c:(gsel[d] + 1) * c, hh * HEAD_DIM_A:(hh + 1) * HEAD_DIM_A] = o[i]

    if emit_state:
        @pl.when(j == nblk - 1)
        def _():
            sfin_ref[...] = s_scr[...]


def _delta_recur(prep, s0, bsz, seq_len, nb, emit_state):
    nblk = seq_len // DELTA_TB
    has_s0 = s0 is not None
    fspec = pl.BlockSpec((nb, DELTA_TB, A_WIDTH), lambda b, j: (b, j, 0))
    bspec = pl.BlockSpec((nb, DELTA_TB, A_WIDTH), lambda b, j: (b, nblk - 1 - j, 0))
    dspec_f = pl.BlockSpec((nb, DELTA_G, 2 * H_A, HEAD_DIM_A), lambda b, j: (b, j, 0, 0))
    dspec_b = pl.BlockSpec((nb, DELTA_G, 2 * H_A, HEAD_DIM_A), lambda b, j: (b, nblk - 1 - j, 0, 0))
    st_spec = pl.BlockSpec((nb, 2, H_A, HEAD_DIM_A, HEAD_DIM_A), lambda b, j: (b, 0, 0, 0, 0))
    in_specs = [fspec] * 5 + [bspec] * 5 + [dspec_f, dspec_b]
    args = list(prep[:10]) + [prep[10], prep[10]]
    if has_s0:
        in_specs.append(st_spec)
        args.append(s0)
    out_specs = [fspec, bspec]
    out_shape = [jax.ShapeDtypeStruct((bsz, seq_len, A_WIDTH), F32)] * 2
    if emit_state:
        out_specs.append(st_spec)
        out_shape.append(jax.ShapeDtypeStruct((bsz, 2, H_A, HEAD_DIM_A, HEAD_DIM_A), F32))
    return pl.pallas_call(
        functools.partial(_delta_recur_kernel, nb=nb, nblk=nblk, has_s0=has_s0, emit_state=emit_state),
        grid=(bsz // nb, nblk),
        in_specs=in_specs,
        out_specs=out_specs,
        out_shape=out_shape,
        scratch_shapes=[pltpu.VMEM((nb, 2, H_A, HEAD_DIM_A, HEAD_DIM_A), F32)],
        compiler_params=_cp(("parallel", "arbitrary")),
        name="delta_recur",
    )(*args)


def _s5prep_kernel(lre_ref, lim_ref, ldt_ref, bre_ref, bim_ref, cre_ref, cim_ref,
                   m_out, ef_out, eb_out, ff_out, fb_out, lt_out, xcat_scr, qcat_scr, ft_scr):
    gs, p, cg, t8, half = S5_SG_GROUPS, S5_STATE, S5_GROUP, S5_T, S5_K // 2
    j = (lax.broadcasted_iota(jnp.int32, (S5_ROWS, 1), 0) // cg).astype(F32)
    tile = lambda m: jnp.concatenate([m] * t8, axis=0)
    one = jnp.ones((1, 1), F32)
    bd = {}
    for d in range(2):
        e_out = (ef_out, eb_out)[d]
        f_out = (ff_out, fb_out)[d]
        xcat_scr[...] = jnp.zeros(xcat_scr.shape, F32)
        qcat_scr[...] = jnp.zeros(qcat_scr.shape, F32)
        ft_scr[...] = jnp.zeros(ft_scr.shape, F32)
        e_out[0] = jnp.zeros((S5_K, S5_K), BF16)
        for gl in range(gs):
            lre, lim = lre_ref[0, d, 0, gl:gl + 1, :], lim_ref[0, d, 0, gl:gl + 1, :]
            dt = jnp.exp(ldt_ref[0, d, 0, gl:gl + 1, :])
            zr, zi = lre * dt, lim * dt

            def lam_pow(e):
                mag = jnp.exp(e * zr)
                return mag * jnp.cos(e * zi), mag * jnp.sin(e * zi)

            l1r, l1i = lam_pow(one)
            den = lre * lre + lim * lim
            nr, ni = l1r - 1.0, l1i
            cfr = (nr * lre + ni * lim) / den
            cfi = (ni * lre - nr * lim) / den
            bre, bim = bre_ref[0, gl], bim_ref[0, gl]
            bbr = cfr * bre - cfi * bim
            bbi = cfr * bim + cfi * bre
            cre, cim = cre_ref[0, gl], cim_ref[0, gl]
            c_r, c_i = tile(cre), tile(cim)
            b_r, b_i = tile(bbr), tile(bbi)
            grow = slice(gl * cg, (gl + 1) * cg)
            lre_c, lim_c = slice(gl * p, (gl + 1) * p), slice(half + gl * p, half + (gl + 1) * p)

            xcat_scr[grow, 2 * gl * p:(2 * gl + 1) * p] = bbr
            xcat_scr[grow, (2 * gl + 1) * p:(2 * gl + 2) * p] = bbi
            pr, pi = lam_pow(j)
            qr = pr * c_r - pi * c_i
            qi = pr * c_i + pi * c_r
            for jj in range(t8):
                qcat_scr[jj, grow, 2 * gl * p:(2 * gl + 1) * p] = qr[jj * cg:(jj + 1) * cg, :]
                qcat_scr[jj, grow, (2 * gl + 1) * p:(2 * gl + 2) * p] = -qi[jj * cg:(jj + 1) * cg, :]

            er, ei = lam_pow(j if d else (t8 - 1.0) - j)
            e_r = (er * b_r - ei * b_i).astype(BF16)
            e_i = (er * b_i + ei * b_r).astype(BF16)
            fr, fi = lam_pow((t8 - j) if d else j + 1.0)
            f_r = fr * c_r - fi * c_i
            f_i = -(fr * c_i + fi * c_r)
            for s in range(t8):
                rows = slice(s * LANES + gl * cg, s * LANES + (gl + 1) * cg)
                e_out[0, rows, lre_c] = e_r[s * cg:(s + 1) * cg, :]
                e_out[0, rows, lim_c] = e_i[s * cg:(s + 1) * cg, :]
                ft_scr[rows, lre_c] = f_r[s * cg:(s + 1) * cg, :]
                ft_scr[rows, lim_c] = f_i[s * cg:(s + 1) * cg, :]

            ltr, lti = lam_pow(one * float(t8))
            lt_out[0, d:d + 1, lre_c] = ltr
            lt_out[0, d:d + 1, lim_c] = lti
        f_out[0] = ft_scr[...].T.astype(BF16)
        xcat = xcat_scr[...]
        for jj in range(t8):
            bd[(d, jj)] = _mm3_nt(xcat, qcat_scr[jj])
    for s in range(t8):
        for t in range(t8):
            blk = bd[(0, t - s)] if t > s else (bd[(1, s - t)] if s > t else bd[(0, 0)] + bd[(1, 0)])
            m_out[0, s * LANES:(s + 1) * LANES, t * LANES:(t + 1) * LANES] = blk.astype(BF16)


def _s5prep(lam_re, lam_im, log_dt, b_re, b_im, c_re, c_im):
    gs, p, cg = S5_SG_GROUPS, S5_STATE, S5_GROUP
    nq = DEPTH * S5_SG
    lam_idx = lambda q: (q // S5_SG, 0, q % S5_SG, 0, 0)
    par_idx = lambda q: (q, 0, 0, 0)
    lam5 = lambda v, w: v.reshape(DEPTH, 2, S5_SG, gs, w)
    b_t = lambda v: v.reshape(nq, gs, p, cg).transpose(0, 1, 3, 2)
    par4 = lambda v: v.reshape(nq, gs, cg, p)
    mat = pl.BlockSpec((1, S5_K, S5_K), lambda q: (q, 0, 0))
    return pl.pallas_call(
        _s5prep_kernel,
        grid=(nq,),
        in_specs=[pl.BlockSpec((1, 2, 1, gs, p), lam_idx), pl.BlockSpec((1, 2, 1, gs, p), lam_idx),
                  pl.BlockSpec((1, 2, 1, gs, 1), lam_idx)] + [pl.BlockSpec((1, gs, cg, p), par_idx)] * 4,
        out_specs=[mat] * 5 + [pl.BlockSpec((1, 2, S5_K), lambda q: (q, 0, 0))],
        out_shape=[jax.ShapeDtypeStruct((nq, S5_K, S5_K), BF16)] * 5 + [jax.ShapeDtypeStruct((nq, 2, S5_K), F32)],
        scratch_shapes=[pltpu.VMEM((S5_ROWS, S5_K), F32), pltpu.VMEM((S5_T, S5_ROWS, S5_K), F32),
                        pltpu.VMEM((S5_K, S5_K), F32)],
        compiler_params=_cp(("parallel",)),
        name="s5prep",
    )(lam5(lam_re, p), lam5(lam_im, p), lam5(log_dt, 1), b_t(b_re), b_t(b_im), par4(c_re), par4(c_im))


S5_RT = 256


def _s5_kernel(*refs, nchunk, nb, seq_len, has_h0, emit_state):
    it = iter(refs)
    u_ref = next(it)
    m_ref, ef_ref, eb_ref, ff_ref, fb_ref, lt_ref = (next(it) for _ in range(6))
    h0_ref = next(it) if has_h0 else None
    y_ref = next(it)
    hfin_ref = next(it) if emit_state else None
    up_scr, sf_scr, sb_scr = next(it), next(it), next(it)

    r = nchunk * nb
    rtile = min(S5_RT, r)
    half = S5_K // 2
    for t in range(S5_T):
        for b in range(nb):
            up_scr[t, pl.ds(b, nchunk, stride=nb), :] = u_ref[pl.ds(b * seq_len + t, nchunk, stride=S5_T), :]

    def u_tile(rt):
        rows = slice(rt * rtile, (rt + 1) * rtile)
        return jnp.concatenate([up_scr[t, rows, :] for t in range(S5_T)], axis=1).astype(BF16)

    for rt in range(r // rtile):
        rows = slice(rt * rtile, (rt + 1) * rtile)
        ub = u_tile(rt)
        sf_scr[rows, :] = jnp.dot(ub, ef_ref[0], preferred_element_type=F32)
        sb_scr[rows, :] = jnp.dot(ub, eb_ref[0], preferred_element_type=F32)

    lane = lax.broadcasted_iota(jnp.int32, (1, S5_K), 1)

    def coeffs(d):
        lt = lt_ref[0, d:d + 1, :]
        sw = pltpu.roll(lt, half, 1)
        return jnp.where(lane < half, lt, sw), jnp.where(lane < half, -sw, lt)

    af, bf = coeffs(0)
    ab, bb = coeffs(1)
    if has_h0:
        h0f, h0b = h0_ref[0, 0, 0], h0_ref[0, 1, 0]
    else:
        h0f = h0b = jnp.zeros((nb, S5_K), F32)

    tile_rows = max(nb, 8)
    cps = tile_rows // nb
    n_iter = nchunk // cps

    def step(h, a, b, seg):
        return a * h + b * pltpu.roll(h, half, 1) + seg

    def body(i, carry):
        hf, hb = carry
        rf = pl.ds(pl.multiple_of(i * tile_rows, tile_rows), tile_rows)
        rb = pl.ds(pl.multiple_of((n_iter - 1 - i) * tile_rows, tile_rows), tile_rows)
        xf, xb = sf_scr[rf, :], sb_scr[rb, :]
        of, ob = [], [None] * cps
        for jj in range(cps):
            of.append(hf)
            hf = step(hf, af, bf, xf[jj * nb:(jj + 1) * nb, :])
        for jj in reversed(range(cps)):
            ob[jj] = hb
            hb = step(hb, ab, bb, xb[jj * nb:(jj + 1) * nb, :])
        sf_scr[rf, :] = of[0] if cps == 1 else jnp.concatenate(of, axis=0)
        sb_scr[rb, :] = ob[0] if cps == 1 else jnp.concatenate(ob, axis=0)
        return hf, hb

    hf, hb = lax.fori_loop(0, n_iter, body, (h0f, h0b))
    if emit_state:
        hfin_ref[0, 0, 0] = hf
        hfin_ref[0, 1, 0] = hb

    for rt in range(r // rtile):
        rows = slice(rt * rtile, (rt + 1) * rtile)
        y = jnp.dot(u_tile(rt), m_ref[0], preferred_element_type=F32)
        y = y + jnp.dot(sf_scr[rows, :].astype(BF16), ff_ref[0], preferred_element_type=F32)
        y = y + jnp.dot(sb_scr[rows, :].astype(BF16), fb_ref[0], preferred_element_type=F32)
        for t in range(S5_T):
            up_scr[t, rows, :] = y[:, t * LANES:(t + 1) * LANES]
    for t in range(S5_T):
        for b in range(nb):
            y_ref[pl.ds(b * seq_len + t, nchunk, stride=S5_T), :] = up_scr[t, pl.ds(b, nchunk, stride=nb), :]


def _s5(u, mats_l, h0, bsz, seq_len, nb, emit_state):
    nchunk = seq_len // S5_T
    r = nchunk * nb
    has_h0 = h0 is not None
    qi = lambda q, j: (q, 0, 0)
    wspec = pl.BlockSpec((1, S5_K, S5_K), qi)
    in_specs = [pl.BlockSpec((nb * seq_len, LANES), lambda q, j: (j, q))] + [wspec] * 5 \
        + [pl.BlockSpec((1, 2, S5_K), qi)]
    args = [u] + list(mats_l)
    st_spec = pl.BlockSpec((1, 2, 1, nb, S5_K), lambda q, j: (q, 0, j, 0, 0))
    st_shape = (S5_SG, 2, bsz // nb, nb, S5_K)
    if has_h0:
        in_specs.append(st_spec)
        args.append(h0.reshape(st_shape))
    out_specs = [pl.BlockSpec((nb * seq_len, LANES), lambda q, j: (j, q))]
    out_shape = [jax.ShapeDtypeStruct((bsz * seq_len, B_WIDTH), F32)]
    if emit_state:
        out_specs.append(st_spec)
        out_shape.append(jax.ShapeDtypeStruct(st_shape, F32))
    outs = pl.pallas_call(
        functools.partial(_s5_kernel, nchunk=nchunk, nb=nb, seq_len=seq_len, has_h0=has_h0, emit_state=emit_state),
        grid=(S5_SG, bsz // nb),
        in_specs=in_specs,
        out_specs=out_specs,
        out_shape=out_shape,
        scratch_shapes=[pltpu.VMEM((S5_T, r, LANES), F32), pltpu.VMEM((r, S5_K), F32), pltpu.VMEM((r, S5_K), F32)],
        compiler_params=_cp(("parallel", "parallel")),
        name="s5",
    )(*args)
    if emit_state:
        return outs[0], outs[1].reshape(S5_SG, 2, bsz, S5_K)
    return (outs[0],)


def _merge_kernel(x_ref, shift_ref, scale_ref, gate_ref, gnorm_ref, of_ref, ob_ref, zs_ref, gout_ref, ys_ref, u_ref,
                  oc_ref, wg_ref, wba_ref, wbb_ref, wbc_ref, wout_ref, wglu_ref, bglu_ref, dskip_ref, xo_ref):
    x = x_ref[...]
    h = ((_rms(x) * gnorm_ref[...]) * (1.0 + scale_ref[0]) + shift_ref[0]).astype(BF16)
    od = of_ref[...] + ob_ref[...]
    gout = gout_ref[...]
    oa = jnp.concatenate([_rms(od[:, hh * HEAD_DIM_A:(hh + 1) * HEAD_DIM_A]) * gout for hh in range(H_A)], axis=1)
    oa = oa * zs_ref[...]
    yb = jax.nn.gelu(ys_ref[...] + dskip_ref[...] * u_ref[...])
    ob = yb * jax.nn.sigmoid(_mm(yb, wglu_ref[...]) + bglu_ref[...])
    acc = None
    for nbr, (o, w) in enumerate(((oa, wba_ref), (ob, wbb_ref), (oc_ref[...], wbc_ref))):
        gate = jax.nn.sigmoid(jnp.dot(h, wg_ref[:, nbr * D_MODEL:(nbr + 1) * D_MODEL], preferred_element_type=F32))
        term = gate * _mm(o, w[...])
        acc = term if acc is None else acc + term
    out = _mm(acc, wout_ref[...])
    xo_ref[...] = x + gate_ref[0] * out


def _merge(x, mods_rows, row_of_tile, lp, o_f, o_b, zs, ys, u, oc, tl):
    n = x.shape[0]
    const2 = lambda i: (0, 0)
    tok = lambda w: pl.BlockSpec((tl, w), lambda i: (i, 0))
    mod = lambda j: pl.BlockSpec((1, 1, D_MODEL), lambda i: (row_of_tile(i) * 6 + j, 0, 0))
    return pl.pallas_call(
        _merge_kernel,
        grid=(n // tl,),
        in_specs=[tok(D_MODEL), mod(0), mod(1), mod(2), pl.BlockSpec((1, D_MODEL), const2),
                  tok(A_WIDTH), tok(A_WIDTH), tok(A_WIDTH), pl.BlockSpec((1, HEAD_DIM_A), const2),
                  tok(B_WIDTH), tok(B_WIDTH), tok(BRANCH_WIDTH),
                  pl.BlockSpec((D_MODEL, N_BRANCH * D_MODEL), const2),
                  pl.BlockSpec((BRANCH_WIDTH, D_MODEL), const2), pl.BlockSpec((BRANCH_WIDTH, D_MODEL), const2),
                  pl.BlockSpec((BRANCH_WIDTH, D_MODEL), const2), pl.BlockSpec((D_MODEL, D_MODEL), const2),
                  pl.BlockSpec((B_WIDTH, B_WIDTH), const2), pl.BlockSpec((1, B_WIDTH), const2),
                  pl.BlockSpec((1, B_WIDTH), const2)],
        out_specs=tok(D_MODEL),
        out_shape=jax.ShapeDtypeStruct((n, D_MODEL), F32),
        compiler_params=_cp(("parallel",)),
        name="merge",
    )(x, mods_rows, mods_rows, mods_rows, lp['g_norm_mix'], o_f, o_b, zs, lp['g_delta_out'], ys, u, oc,
      lp['w_gates'], lp['w_br_a'], lp['w_br_b'],
      lp['w_br_c'], lp['w_out'], lp['w_glu'], lp['b_glu'], lp['s5_d'])


FF_BLK = 256


def _ffn_kernel(*refs, tl, tiles_per_seq, final):
    it = iter(refs)
    xp_ref, xc_ref, xn_ref = next(it), next(it), next(it)
    shift_ref, scale_ref, gate_ref, gnorm_ref = next(it), next(it), next(it), next(it)
    wup_ref, convw_ref, convb_ref, wdown_ref = next(it), next(it), next(it), next(it)
    gfin_ref = next(it) if final else None
    xo_ref = next(it)
    yo_ref = next(it) if final else None
    act_scr = next(it)

    i = pl.program_id(0)
    pos = i % tiles_per_seq
    mod_scale = 1.0 + scale_ref[0]
    mod_shift = shift_ref[0]
    gain = gnorm_ref[...]

    def norm_mod(x):
        return ((_rms(x) * gain) * mod_scale + mod_shift).astype(BF16)

    x = xc_ref[...]
    h_prev = norm_mod(xp_ref[...]) * (pos > 0).astype(BF16)
    h_next = norm_mod(xn_ref[...]) * (pos < tiles_per_seq - 1).astype(BF16)
    h_ext = jnp.concatenate([h_prev, norm_mod(x), h_next], axis=0)
    n_ext = tl + 2 * HALO
    pad = FFN_CONV // 2

    def conv_act(cols):
        up = jnp.dot(h_ext, wup_ref[:, cols], preferred_element_type=F32)
        acc = None
        for t in range(FFN_CONV):
            sh = (pad - t) % n_ext
            src = up if sh == 0 else pltpu.roll(up, sh, 0)
            term = src[HALO:HALO + tl, :] * convw_ref[t:t + 1, cols]
            acc = term if acc is None else acc + term
        return acc + convb_ref[:, cols]

    for j in range(D_FF // FF_BLK):
        gcols = slice(j * FF_BLK, (j + 1) * FF_BLK)
        vcols = slice(D_FF + j * FF_BLK, D_FF + (j + 1) * FF_BLK)
        act_scr[:, gcols] = (_silu(conv_act(gcols)) * conv_act(vcols)).astype(BF16)
    out = jnp.dot(act_scr[...], wdown_ref[...], preferred_element_type=F32)
    xo = x + gate_ref[0] * out
    xo_ref[...] = xo
    if final:
        yo_ref[...] = _rms(xo) * gfin_ref[...]


def _ffn(x, mods_rows, row_of_tile, lp, seq_len, tl, g_final):
    n = x.shape[0]
    tps = seq_len // tl
    final = g_final is not None
    xprev, xnext = _halo_maps(n, tl)
    const2 = lambda i: (0, 0)
    mod = lambda j: pl.BlockSpec((1, 1, D_MODEL), lambda i: (row_of_tile(i) * 6 + j, 0, 0))
    tok = pl.BlockSpec((tl, D_MODEL), lambda i: (i, 0))
    in_specs = [pl.BlockSpec((HALO, D_MODEL), xprev), tok, pl.BlockSpec((HALO, D_MODEL), xnext),
                mod(3), mod(4), mod(5), pl.BlockSpec((1, D_MODEL), const2),
                pl.BlockSpec((D_MODEL, 2 * D_FF), const2), pl.BlockSpec((FFN_CONV, 2 * D_FF), const2),
                pl.BlockSpec((1, 2 * D_FF), const2), pl.BlockSpec((D_FF, D_MODEL), const2)]
    args = [x, x, x, mods_rows, mods_rows, mods_rows, lp['g_norm_ffn'], lp['w_ffn_up'], lp['conv_ffn'],
            lp['b_conv_ffn'], lp['w_ffn_down']]
    out_specs = [tok]
    out_shape = [jax.ShapeDtypeStruct((n, D_MODEL), F32)]
    if final:
        in_specs.append(pl.BlockSpec((1, D_MODEL), const2))
        args.append(g_final)
        out_specs.append(tok)
        out_shape.append(jax.ShapeDtypeStruct((n, D_MODEL), F32))
    return pl.pallas_call(
        functools.partial(_ffn_kernel, tl=tl, tiles_per_seq=tps, final=final),
        grid=(n // tl,),
        in_specs=in_specs,
        out_specs=out_specs,
        out_shape=out_shape,
        scratch_shapes=[pltpu.VMEM((tl, D_FF), BF16)],
        compiler_params=_cp(("parallel",)),
        name="ffn",
    )(*args)


def _pad_heads(w, head_w, n_heads):
    k = w.shape[0]
    w = w.reshape(k, n_heads, head_w)
    w = jnp.pad(w, ((0, 0), (0, 0), (0, HEAD_PAD - head_w)))
    return w.reshape(k, n_heads * HEAD_PAD)


def _layer_params(l, p):
    w_in = p['w_in'][l]
    o = 0
    parts = {}
    for name, wd in (('qkv', 3 * A_WIDTH), ('z', A_WIDTH), ('beta', 2 * H_A), ('alpha', 2 * H_A), ('u', B_WIDTH),
                     ('qa', Q_LORA), ('kva', KV_LORA), ('kr', QK_ROPE), ('gates', N_BRANCH * D_MODEL)):
        parts[name] = w_in[:, o:o + wd]
        o += wd
    small = jnp.concatenate([parts['beta'], parts['alpha'], parts['kr'],
                             jnp.zeros((D_MODEL, LANES - 4 * H_A - QK_ROPE), F32)], axis=1)
    w_rest = jnp.concatenate([parts['z'], parts['u'], parts['qa'], parts['kva'], small], axis=1)
    pad8 = lambda v: jnp.pad(v.reshape(1, 2 * H_A), ((0, 0), (2 * H_A, LANES - 4 * H_A)))
    w_kv = p['w_kv_b'][l].reshape(KV_LORA, H_C, QK_NOPE + V_HEAD)
    w_k = w_kv[:, :, :QK_NOPE].reshape(KV_LORA, H_C * QK_NOPE)
    w_v = w_kv[:, :, QK_NOPE:].reshape(KV_LORA, H_C * V_HEAD)
    row = lambda v: v.reshape(1, -1)
    return {
        'g_norm_mix': row(p['g_norm_mix'][l]), 'g_norm_ffn': row(p['g_norm_ffn'][l]),
        'w_qkv': parts['qkv'].astype(BF16), 'w_rest': w_rest.astype(BF16), 'w_gates': parts['gates'].astype(BF16),
        'conv_qkv': p['conv_qkv'][l],
        'a_log128': pad8(p['a_log'][l]), 'dt_bias128': pad8(p['dt_bias'][l]),
        'g_delta_out': row(p['g_delta_out'][l]),
        'g_q_a': row(p['g_q_a'][l]), 'g_kv_a': row(p['g_kv_a'][l]),
        'w_q_pad': _pad_heads(p['w_q_b'][l], QK_NOPE + QK_ROPE, H_C).astype(BF16),
        'w_k_pad': _pad_heads(w_k, QK_NOPE, H_C).astype(BF16),
        'w_v_pad': _pad_heads(w_v, V_HEAD, H_C).astype(BF16),
        'w_br_a': p['w_branch'][l, 0].astype(BF16), 'w_br_b': p['w_branch'][l, 1].astype(BF16),
        'w_br_c': p['w_branch'][l, 2].astype(BF16),
        'w_out': p['w_out'][l].astype(BF16),
        'w_glu': p['w_glu'][l].astype(BF16), 'b_glu': row(p['b_glu'][l]), 's5_d': row(p['s5_d'][l]),
        'w_ffn_up': p['w_ffn_up'][l].astype(BF16), 'conv_ffn': p['conv_ffn'][l],
        'b_conv_ffn': row(p['b_conv_ffn'][l]), 'w_ffn_down': p['w_ffn_down'][l].astype(BF16),
    }


def _rope_tables(length):
    rows = length // GRID_W
    row = jnp.repeat(jnp.arange(rows, dtype=F32), GRID_W)
    col = (jnp.arange(length) % GRID_W).astype(F32)
    n_freq = QK_ROPE // 4
    inv_freq = 1.0 / (ROPE_BASE ** (jnp.arange(n_freq, dtype=F32) / n_freq))
    ang = jnp.concatenate([row[:, None] * inv_freq, col[:, None] * inv_freq], axis=-1)
    cos, sin = jnp.cos(ang), jnp.sin(ang)
    ones = jnp.ones((length, QK_NOPE), F32)
    zeros = jnp.zeros((length, QK_NOPE), F32)
    tail = jnp.zeros((length, HEAD_PAD - QK_NOPE - QK_ROPE), F32)
    cos_t = jnp.concatenate([ones, cos, cos, tail], axis=1)
    sin_t = jnp.concatenate([zeros, -sin, sin, tail], axis=1)
    return cos_t, sin_t


def _layer(x, bsz, seq_len, mods_rows, row_of_tile, lp, s5mats, ctx, rope_tabs, tl, g_final):
    is_ctx = ctx is None
    outs = _inproj(x, mods_rows, row_of_tile, lp, seq_len, tl, rope_tabs, emit_ctx=is_ctx)
    if is_ctx:
        qkv, zs, u, bg, ckv, kr, qh, kh, vh = outs
    else:
        qkv, zs, u, bg, qh, kh, vh = outs

    prep = _delta_prep(qkv, bg, bsz, seq_len)
    d_out = _delta_recur(prep, None if is_ctx else ctx['s0'], bsz, seq_len, min(bsz, DELTA_NB), emit_state=is_ctx)
    o_f = d_out[0].reshape(bsz * seq_len, A_WIDTH)
    o_b = d_out[1].reshape(bsz * seq_len, A_WIDTH)

    s5_nb = max(1, min(bsz, (S5_STEP_ROWS * S5_T) // seq_len))
    s_out = _s5(u, s5mats, None if is_ctx else ctx['h0'], bsz, seq_len, s5_nb, emit_state=is_ctx)
    ys = s_out[0]

    segs = [(kh, vh, seq_len)]
    if not is_ctx:
        segs = [(ctx['kh'], ctx['vh'], ctx['past'])] + segs
    oc = _attention(qh, segs, bsz, seq_len, min(seq_len, ATT_TQ),
                    H_C if seq_len <= ATT_BATCHED_MAX else ATT_HEADS_LONG)

    x = _merge(x, mods_rows, row_of_tile, lp, o_f, o_b, zs, ys, u, oc, tl)
    f_out = _ffn(x, mods_rows, row_of_tile, lp, seq_len, tl, g_final)
    extras = None
    if is_ctx:
        hfin = s_out[1]
        hfin = hfin.reshape(S5_SG, 2, bsz, 2, S5_SG_GROUPS, S5_STATE).transpose(3, 2, 1, 0, 4, 5)
        hfin = hfin.reshape(2, bsz, 2, S5_GROUPS, S5_STATE)
        extras = (d_out[2], hfin[0], hfin[1], ckv.reshape(bsz, seq_len, KV_LORA), kr.reshape(bsz, seq_len, QK_ROPE))
    return f_out, extras


def kernel(x_prompt, x_sample, state_delta, state_s5_re, state_s5_im, cache_ckv, cache_krope, c, c_ctx, w_mod, b_mod, g_norm_mix, g_norm_ffn, w_in, conv_qkv, a_log, dt_bias, g_delta_out, s5_lam_re, s5_lam_im, s5_log_dt, s5_b_re, s5_b_im, s5_c_re, s5_c_im, s5_d, w_glu, b_glu, g_q_a, w_q_b, g_kv_a, w_kv_b, w_branch, w_out, w_ffn_up, conv_ffn, b_conv_ffn, w_ffn_down, g_final):
    p = dict(g_norm_mix=g_norm_mix, g_norm_ffn=g_norm_ffn, w_in=w_in, conv_qkv=conv_qkv, a_log=a_log,
             dt_bias=dt_bias, g_delta_out=g_delta_out, s5_d=s5_d, w_glu=w_glu, b_glu=b_glu, g_q_a=g_q_a,
             w_q_b=w_q_b, g_kv_a=g_kv_a, w_kv_b=w_kv_b, w_branch=w_branch, w_out=w_out, w_ffn_up=w_ffn_up,
             conv_ffn=conv_ffn, b_conv_ffn=b_conv_ffn, w_ffn_down=w_ffn_down)
    bp, lp_len, _ = x_prompt.shape
    bs, ls_len, _ = x_sample.shape
    past = cache_ckv.shape[2]
    depth = w_in.shape[0]

    mod_rows = MOD_ROWS
    cvec = jnp.concatenate([c_ctx[None, :], c, jnp.zeros((mod_rows - 1 - bs, D_MODEL), F32)], axis=0).astype(F32)
    mods = _modulation(cvec, w_mod, b_mod)

    s5_all = _s5prep(s5_lam_re, s5_lam_im, s5_log_dt, s5_b_re, s5_b_im, s5_c_re, s5_c_im)

    rope_tabs = _rope_tables(ls_len)
    tl_p = min(lp_len, TOKEN_TILE)
    tl_s = min(ls_len, TOKEN_TILE)
    tiles_per_seq_s = ls_len // tl_s

    def sg_state(re, im):
        t = jnp.stack([re, im], axis=0).astype(F32).reshape(2, bs, 2, S5_SG, S5_SG_GROUPS, S5_STATE)
        return t.transpose(3, 2, 1, 0, 4, 5).reshape(S5_SG, 2, bs, S5_K)

    xp = x_prompt.astype(F32).reshape(bp * lp_len, D_MODEL)
    xs = x_sample.astype(F32).reshape(bs * ls_len, D_MODEL)
    deltas, s5_res, s5_ims, ckvs, kropes = [], [], [], [], []
    yp = ys = None
    for l in range(depth):
        lp = _layer_params(l, p)
        mods_rows = mods[l].reshape(mod_rows * 6, 1, D_MODEL)
        sl = slice(l * S5_SG, (l + 1) * S5_SG)
        s5mats = tuple(m[sl] for m in s5_all)
        last = l == depth - 1
        gfin = g_final.reshape(1, D_MODEL) if last else None

        out_p, extras = _layer(xp, bp, lp_len, mods_rows, lambda i: 0, lp, s5mats, None, None, tl_p, gfin)
        deltas.append(extras[0])
        s5_res.append(extras[1])
        s5_ims.append(extras[2])
        ckvs.append(extras[3])
        kropes.append(extras[4])

        kh_c, vh_c = _kvcache(cache_ckv[:, l].astype(F32).reshape(bs * past, KV_LORA),
                              cache_krope[:, l].astype(F32).reshape(bs * past, QK_ROPE), lp, past)
        ctx = dict(s0=state_delta[:, l].astype(F32), h0=sg_state(state_s5_re[:, l], state_s5_im[:, l]),
                   kh=kh_c, vh=vh_c, past=past)
        out_s, _ = _layer(xs, bs, ls_len, mods_rows, lambda i: 1 + i // tiles_per_seq_s, lp, s5mats, ctx, rope_tabs,
                          tl_s, gfin)
        xp, xs = out_p[0], out_s[0]
        if last:
            yp, ys = out_p[1], out_s[1]

    y_prompt = yp.reshape(bp, lp_len, D_MODEL)
    y_sample = ys.reshape(bs, ls_len, D_MODEL)
    return (y_prompt, y_sample, jnp.stack(deltas, axis=1), jnp.stack(s5_res, axis=1), jnp.stack(s5_ims, axis=1),
            jnp.stack(ckvs, axis=1), jnp.stack(kropes, axis=1))
```

```python
import functools
import math

import jax
import jax.numpy as jnp
from jax import lax
from jax.experimental import pallas as pl
from jax.experimental.pallas import tpu as pltpu

F32 = jnp.float32
BF16 = jnp.bfloat16

D_MODEL = 1024
DEPTH = 4
GRID_W = 64
H_A = 4
HEAD_DIM_A = 128
A_WIDTH = H_A * HEAD_DIM_A
SHORT_CONV = 5
CHUNK = 64
S5_GROUP = 16
S5_STATE = 64
B_WIDTH = 512
S5_GROUPS = B_WIDTH // S5_GROUP
H_C = 8
QK_NOPE = 64
QK_ROPE = 32
V_HEAD = 64
Q_LORA = 384
KV_LORA = 256
ROPE_BASE = 10000.0
N_BRANCH = 3
BRANCH_WIDTH = 512
D_FF = 2816
FFN_CONV = 3
NORM_EPS = 1e-6

LANES = 128
HEAD_PAD = LANES
SUB = 16
S5_T = 8
S5_ROWS = S5_T * S5_GROUP
S5_SG = B_WIDTH // LANES
S5_HALVES = 2
S5_HG = LANES // S5_HALVES // S5_GROUP
S5_TOKW = S5_HG * S5_GROUP
S5_K = S5_T * S5_TOKW
S5_W = S5_HALVES * S5_K

VMEM_LIMIT = 56 * 1024 * 1024
HALO = 16
TOKEN_TILE = 512
ATT_TQ = 512
ATT_HEADS_LONG = 4
DELTA_NB = 4
S5_STEP_ROWS = 1024
MOD_ROWS = 16


def _cp(sem):
    return pltpu.CompilerParams(dimension_semantics=sem, vmem_limit_bytes=VMEM_LIMIT)


def _mm(a, b):
    return jnp.dot(a.astype(BF16), b.astype(BF16), preferred_element_type=F32)


def _bmm(a, b):
    return lax.dot_general(a.astype(BF16), b.astype(BF16), (((2,), (1,)), ((0,), (0,))),
                           preferred_element_type=F32)


def _bmm_nt(a, b):
    return lax.dot_general(a.astype(BF16), b.astype(BF16), (((2,), (2,)), ((0,), (0,))),
                           preferred_element_type=F32)


def _bmm_tn(a, b):
    return lax.dot_general(a.astype(BF16), b.astype(BF16), (((1,), (1,)), ((0,), (0,))),
                           preferred_element_type=F32)


def _split3(x):
    x1 = x.astype(BF16)
    r = x - x1.astype(F32)
    x2 = r.astype(BF16)
    x3 = (r - x2.astype(F32)).astype(BF16)
    return x1, x2, x3


def _mm_exact_rhs(a, b_bf16):
    a1, a2, a3 = _split3(a)
    d = lambda t: jnp.dot(t, b_bf16, preferred_element_type=F32)
    return d(a1) + d(a2) + d(a3)


def _mm_exact_lhs(a_bf16, b):
    b1, b2, b3 = _split3(b)
    d = lambda t: jnp.dot(a_bf16, t, preferred_element_type=F32)
    return d(b1) + d(b2) + d(b3)


def _mm3_nt(a, b):
    a1 = a.astype(BF16)
    a2 = (a - a1.astype(F32)).astype(BF16)
    b1 = b.astype(BF16)
    b2 = (b - b1.astype(F32)).astype(BF16)
    d = lambda s, t: lax.dot_general(s, t, (((1,), (1,)), ((), ())), preferred_element_type=F32)
    return d(a1, b1) + d(a1, b2) + d(a2, b1)


def _silu(x):
    return x * jax.nn.sigmoid(x)


def _rms(x):
    return x * lax.rsqrt(jnp.mean(x * x, axis=-1, keepdims=True) + NORM_EPS)


def _mod_kernel(c_ref, w_ref, b_ref, o_ref):
    s = _silu(c_ref[...])
    o_ref[0] = _mm(s, w_ref[0]) + b_ref[0]


def _modulation(cvec, w_mod, b_mod):
    rows = cvec.shape[0]
    nblk = w_mod.shape[-1] // D_MODEL
    return pl.pallas_call(
        _mod_kernel,
        grid=(DEPTH, nblk),
        in_specs=[
            pl.BlockSpec((rows, D_MODEL), lambda l, j: (0, 0)),
            pl.BlockSpec((1, D_MODEL, D_MODEL), lambda l, j: (l, 0, j)),
            pl.BlockSpec((1, 1, D_MODEL), lambda l, j: (l, 0, j)),
        ],
        out_specs=pl.BlockSpec((1, rows, D_MODEL), lambda l, j: (l, 0, j)),
        out_shape=jax.ShapeDtypeStruct((DEPTH, rows, w_mod.shape[-1]), F32),
        compiler_params=_cp(("parallel", "parallel")),
        name="modulation",
    )(cvec, w_mod, b_mod.reshape(DEPTH, 1, -1))


REST_W = A_WIDTH + B_WIDTH + Q_LORA + KV_LORA + LANES
OFF_Z, OFF_U, OFF_QA = 0, A_WIDTH, A_WIDTH + B_WIDTH
OFF_KVA, OFF_SM = OFF_QA + Q_LORA, OFF_QA + Q_LORA + KV_LORA
SM_KR = 16
MLA_W = H_C * HEAD_PAD
INPROJ_SUB = 256


def _rope_apply(x, cos, sin):
    lane = lax.broadcasted_iota(jnp.int32, x.shape, 1)
    partner = jnp.where(lane < QK_NOPE + QK_ROPE // 2,
                        pltpu.roll(x, HEAD_PAD - QK_ROPE // 2, 1),
                        pltpu.roll(x, QK_ROPE // 2, 1))
    return x * cos + partner * sin


def _inproj_kernel(*refs, tl, tiles_per_seq, rope, emit_ctx):
    it = iter(refs)
    xp_ref, xc_ref, xn_ref = next(it), next(it), next(it)
    shift_ref, scale_ref, gnorm_ref = next(it), next(it), next(it)
    wqkv_ref, convw_ref, wrest_ref = next(it), next(it), next(it)
    gqa_ref, gkva_ref, alog_ref, dtb_ref = next(it), next(it), next(it), next(it)
    wq_ref, wk_ref, wv_ref = next(it), next(it), next(it)
    cos_ref = sin_ref = None
    if rope:
        cos_ref, sin_ref = next(it), next(it)
    qkv_out, zs_out, u_out, bg_out = next(it), next(it), next(it), next(it)
    ckv_out = kr_out = None
    if emit_ctx:
        ckv_out, kr_out = next(it), next(it)
    qh_out, kh_out, vh_out = next(it), next(it), next(it)

    i = pl.program_id(0)
    pos = i % tiles_per_seq
    mod_scale = 1.0 + scale_ref[0]
    mod_shift = shift_ref[0]
    gain = gnorm_ref[...]

    def norm_mod(x):
        return ((_rms(x) * gain) * mod_scale + mod_shift).astype(BF16)

    nsub = tl // INPROJ_SUB
    ts = INPROJ_SUB
    n_ext = ts + 2 * HALO
    rowi = lax.broadcasted_iota(jnp.int32, (n_ext, 1), 0)
    pad = SHORT_CONV // 2
    qscale = (QK_NOPE + QK_ROPE) ** -0.5 * math.log2(math.e)
    for sub in range(nsub):
        r0 = sub * ts
        rows = slice(r0, r0 + ts)
        h_cur = norm_mod(xc_ref[rows, :])
        h_before = norm_mod(xp_ref[...] if sub == 0 else xc_ref[r0 - HALO:r0, :])
        h_after = norm_mod(xn_ref[...] if sub == nsub - 1 else xc_ref[r0 + ts:r0 + ts + HALO, :])
        h_ext = jnp.concatenate([h_before, h_cur, h_after], axis=0)

        qkv_ext = _mm(h_ext, wqkv_ref[...])
        if sub == 0 or sub == nsub - 1:
            ok_lo = jnp.logical_or(pos > 0, rowi >= HALO) if sub == 0 else True
            ok_hi = jnp.logical_or(pos < tiles_per_seq - 1, rowi < HALO + ts) if sub == nsub - 1 else True
            qkv_ext = qkv_ext * jnp.logical_and(ok_lo, ok_hi).astype(F32)
        for blk in range(3):
            cols = slice(blk * A_WIDTH, (blk + 1) * A_WIDTH)
            xblk = qkv_ext[:, cols]
            acc = None
            for t in range(SHORT_CONV):
                sh = (pad - t) % n_ext
                src = xblk if sh == 0 else pltpu.roll(xblk, sh, 0)
                term = src[HALO:HALO + ts, :] * convw_ref[t:t + 1, cols]
                acc = term if acc is None else acc + term
            acc = _silu(acc)
            if blk < 2:
                parts = []
                for hh in range(H_A):
                    a = acc[:, hh * HEAD_DIM_A:(hh + 1) * HEAD_DIM_A]
                    parts.append(a * lax.rsqrt(jnp.sum(a * a, axis=-1, keepdims=True) + NORM_EPS))
                acc = jnp.concatenate(parts, axis=1)
            qkv_out[rows, cols] = acc.astype(BF16)

        rest = _mm(h_cur, wrest_ref[...])
        zs_out[rows, :] = _silu(rest[:, OFF_Z:OFF_Z + A_WIDTH])
        u_out[rows, :] = rest[:, OFF_U:OFF_U + B_WIDTH]

        small = rest[:, OFF_SM:OFF_SM + LANES]
        lane = lax.broadcasted_iota(jnp.int32, small.shape, 1)
        beta = jax.nn.sigmoid(small)
        glog = -jnp.exp(alog_ref[...]) * jax.nn.softplus(small + dtb_ref[...])
        bg_out[rows, :] = jnp.where(lane < 2 * H_A, beta, glog)

        qa = _rms(rest[:, OFF_QA:OFF_QA + Q_LORA]) * gqa_ref[...]
        ckv = _rms(rest[:, OFF_KVA:OFF_KVA + KV_LORA]) * gkva_ref[...]
        if emit_ctx:
            ckv_out[rows, :] = ckv
            kr_out[rows, :] = small[:, SM_KR:SM_KR + QK_ROPE]
        ckv_b = ckv.astype(BF16)
        qhat = _mm(qa, wq_ref[...])
        khat = _mm(ckv_b, wk_ref[...])
        vh_out[rows, :] = _mm(ckv_b, wv_ref[...]).astype(BF16)
        kr_al = jnp.where((lane >= QK_NOPE) & (lane < QK_NOPE + QK_ROPE),
                          pltpu.roll(small, QK_NOPE - SM_KR, 1), 0.0)
        if rope:
            cos, sin = cos_ref[rows, :], sin_ref[rows, :]
            kr_al = _rope_apply(kr_al, cos, sin)
        for hh in range(H_C):
            cs = slice(hh * HEAD_PAD, (hh + 1) * HEAD_PAD)
            qh = qhat[:, cs]
            if rope:
                qh = _rope_apply(qh, cos, sin)
            qh_out[rows, cs] = (qh * qscale).astype(BF16)
            kh_out[rows, cs] = (khat[:, cs] + kr_al).astype(BF16)


def _halo_maps(n, tl):
    nh = n // HALO

    def xprev(i):
        return (jnp.maximum(i * (tl // HALO) - 1, 0), 0)

    def xnext(i):
        return (jnp.minimum((i + 1) * (tl // HALO), nh - 1), 0)

    return xprev, xnext


def _inproj(x, mods_rows, row_of_tile, lp, seq_len, tl, rope_tabs, emit_ctx):
    n = x.shape[0]
    nt = n // tl
    tps = seq_len // tl
    rope = rope_tabs is not None
    xprev, xnext = _halo_maps(n, tl)
    const2 = lambda i: (0, 0)
    in_specs = [
        pl.BlockSpec((HALO, D_MODEL), xprev),
        pl.BlockSpec((tl, D_MODEL), lambda i: (i, 0)),
        pl.BlockSpec((HALO, D_MODEL), xnext),
        pl.BlockSpec((1, 1, D_MODEL), lambda i: (row_of_tile(i) * 6 + 0, 0, 0)),
        pl.BlockSpec((1, 1, D_MODEL), lambda i: (row_of_tile(i) * 6 + 1, 0, 0)),
        pl.BlockSpec((1, D_MODEL), const2),
        pl.BlockSpec((D_MODEL, 3 * A_WIDTH), const2),
        pl.BlockSpec((SHORT_CONV, 3 * A_WIDTH), const2),
        pl.BlockSpec((D_MODEL, REST_W), const2),
        pl.BlockSpec((1, Q_LORA), const2),
        pl.BlockSpec((1, KV_LORA), const2),
        pl.BlockSpec((1, LANES), const2),
        pl.BlockSpec((1, LANES), const2),
        pl.BlockSpec((Q_LORA, MLA_W), const2),
        pl.BlockSpec((KV_LORA, MLA_W), const2),
        pl.BlockSpec((KV_LORA, MLA_W), const2),
    ]
    args = [x, x, x, mods_rows, mods_rows, lp['g_norm_mix'], lp['w_qkv'], lp['conv_qkv'], lp['w_rest'],
            lp['g_q_a'], lp['g_kv_a'], lp['a_log128'], lp['dt_bias128'], lp['w_q_pad'], lp['w_k_pad'], lp['w_v_pad']]
    if rope:
        in_specs += [pl.BlockSpec((tl, HEAD_PAD), lambda i: (i % tps, 0))] * 2
        args += list(rope_tabs)
    tok = lambda w: pl.BlockSpec((tl, w), lambda i: (i, 0))
    out_specs = [tok(3 * A_WIDTH), tok(A_WIDTH), tok(B_WIDTH), tok(LANES)]
    out_shape = [jax.ShapeDtypeStruct((n, 3 * A_WIDTH), BF16), jax.ShapeDtypeStruct((n, A_WIDTH), F32),
                 jax.ShapeDtypeStruct((n, B_WIDTH), F32), jax.ShapeDtypeStruct((n, LANES), F32)]
    if emit_ctx:
        out_specs += [tok(KV_LORA), tok(QK_ROPE)]
        out_shape += [jax.ShapeDtypeStruct((n, KV_LORA), F32), jax.ShapeDtypeStruct((n, QK_ROPE), F32)]
    out_specs += [tok(MLA_W)] * 3
    out_shape += [jax.ShapeDtypeStruct((n, MLA_W), BF16)] * 3
    return pl.pallas_call(
        functools.partial(_inproj_kernel, tl=tl, tiles_per_seq=tps, rope=rope, emit_ctx=emit_ctx),
        grid=(nt,),
        in_specs=in_specs,
        out_specs=out_specs,
        out_shape=out_shape,
        compiler_params=_cp(("parallel",)),
        name="inproj",
    )(*args)


def _kvcache_kernel(ckv_ref, kr_ref, wk_ref, wv_ref, kh_out, vh_out):
    ckv_b = ckv_ref[...].astype(BF16)
    khat = _mm(ckv_b, wk_ref[...])
    vh_out[...] = _mm(ckv_b, wv_ref[...]).astype(BF16)
    kr_al = kr_ref[...]
    for hh in range(H_C):
        cs = slice(hh * HEAD_PAD, (hh + 1) * HEAD_PAD)
        kh_out[:, cs] = (khat[:, cs] + kr_al).astype(BF16)


def _kvcache(ckv, kr, lp, tl):
    n = ckv.shape[0]
    const2 = lambda i: (0, 0)
    kr = jnp.pad(kr, ((0, 0), (QK_NOPE, HEAD_PAD - QK_NOPE - QK_ROPE)))
    return pl.pallas_call(
        _kvcache_kernel,
        grid=(n // tl,),
        in_specs=[pl.BlockSpec((tl, KV_LORA), lambda i: (i, 0)), pl.BlockSpec((tl, HEAD_PAD), lambda i: (i, 0)),
                  pl.BlockSpec((KV_LORA, MLA_W), const2), pl.BlockSpec((KV_LORA, MLA_W), const2)],
        out_specs=[pl.BlockSpec((tl, MLA_W), lambda i: (i, 0))] * 2,
        out_shape=[jax.ShapeDtypeStruct((n, MLA_W), BF16)] * 2,
        compiler_params=_cp(("parallel",)),
        name="kvcache",
    )(ckv, kr, lp['w_k_pad'], lp['w_v_pad'])


ATT_SUB = 256
ATT_BATCHED_MAX = 512


def _attn_kernel(*refs, nseg, tq, hps, batched):
    q_ref = refs[0]
    k_refs = refs[1:1 + nseg]
    v_refs = refs[1 + nseg:1 + 2 * nseg]
    o_ref = refs[1 + 2 * nseg]
    def one_head(hh, rows):
        cs = slice(hh * HEAD_PAD, (hh + 1) * HEAD_PAD)
        q = q_ref[rows, cs]
        s = [lax.dot_general(q, k[:, cs], (((1,), (1,)), ((), ())), preferred_element_type=F32) for k in k_refs]
        m = s[0].max(axis=-1, keepdims=True)
        for t in s[1:]:
            m = jnp.maximum(m, t.max(axis=-1, keepdims=True))
        p = [jnp.exp2(t - m) for t in s]
        den = p[0].sum(axis=-1, keepdims=True)
        for t in p[1:]:
            den = den + t.sum(axis=-1, keepdims=True)
        acc = None
        for t, v in zip(p, v_refs):
            part = jnp.dot(t.astype(BF16), v[:, cs], preferred_element_type=F32)
            acc = part if acc is None else acc + part
        return acc / den

    def store_pair(pair, rows, o_even, o_odd):
        o = o_even + pltpu.roll(o_odd, V_HEAD, 1)
        o_ref[rows, pair * HEAD_PAD:(pair + 1) * HEAD_PAD] = o.astype(BF16)

    if batched:
        hsl = [slice(hh * HEAD_PAD, (hh + 1) * HEAD_PAD) for hh in range(hps)]
        q = jnp.stack([q_ref[:, cs] for cs in hsl])
        s = [_bmm_nt(q, jnp.stack([k[:, cs] for cs in hsl])) for k in k_refs]
        m = s[0].max(axis=-1, keepdims=True)
        for t in s[1:]:
            m = jnp.maximum(m, t.max(axis=-1, keepdims=True))
        p = [jnp.exp2(t - m) for t in s]
        den = p[0].sum(axis=-1, keepdims=True)
        for t in p[1:]:
            den = den + t.sum(axis=-1, keepdims=True)
        acc = None
        for t, v in zip(p, v_refs):
            part = _bmm(t, jnp.stack([v[:, cs] for cs in hsl]))
            acc = part if acc is None else acc + part
        o = acc / den
        for pair in range(hps // 2):
            store_pair(pair, slice(0, tq), o[2 * pair], o[2 * pair + 1])
        return

    for pair in range(hps // 2):
        for sb in range(tq // ATT_SUB):
            rows = slice(sb * ATT_SUB, (sb + 1) * ATT_SUB)
            store_pair(pair, rows, one_head(2 * pair, rows), one_head(2 * pair + 1, rows))


def _attention(qh, segs, bsz, seq_len, tq, hps):
    nq = seq_len // tq
    wblk = hps * HEAD_PAD
    in_specs = [pl.BlockSpec((tq, wblk), lambda b, h, i: (b * nq + i, h))]
    args = [qh]
    for which in (0, 1):
        for seg in segs:
            in_specs.append(pl.BlockSpec((seg[2], wblk), lambda b, h, i: (b, h)))
            args.append(seg[which])
    return pl.pallas_call(
        functools.partial(_attn_kernel, nseg=len(segs), tq=tq, hps=hps, batched=hps == H_C),
        grid=(bsz, H_C // hps, nq),
        in_specs=in_specs,
        out_specs=pl.BlockSpec((tq, hps * V_HEAD), lambda b, h, i: (b * nq + i, h)),
        out_shape=jax.ShapeDtypeStruct((bsz * seq_len, H_C * V_HEAD), BF16),
        compiler_params=_cp(("parallel", "parallel", "parallel")),
        name="attention",
    )(*args)


DELTA_TB = 256
DELTA_G = DELTA_TB // CHUNK


def _delta_prep_kernel(qkv_ref, bg_ref, uf_ref, wf_ref, qef_ref, kef_ref, qkf_ref,
                       ub_ref, wb_ref, qeb_ref, keb_ref, qkb_ref, dec_ref):
    c = CHUNK
    outs = ((uf_ref, wf_ref, qef_ref, kef_ref, qkf_ref), (ub_ref, wb_ref, qeb_ref, keb_ref, qkb_ref))
    ri = lax.broadcasted_iota(jnp.int32, (c, c), 0)
    ci = lax.broadcasted_iota(jnp.int32, (c, c), 1)
    tril = (ri >= ci).astype(BF16)
    triu = (ri <= ci).astype(BF16)
    same_blk = (ri // SUB) == (ci // SUB)
    scale = HEAD_DIM_A ** -0.5

    insts = [(g, d, hh) for g in range(DELTA_G) for d in range(2) for hh in range(H_A)]
    n = len(insts)
    qkv = qkv_ref[...]
    bg = bg_ref[...]

    def head_stack(base):
        return jnp.stack([qkv[g * c:(g + 1) * c, base + hh * HEAD_DIM_A:base + (hh + 1) * HEAD_DIM_A]
                          for g, d, hh in insts])

    q, k, v = head_stack(0), head_stack(A_WIDTH), head_stack(2 * A_WIDTH)

    per_slab = LANES // c
    bg_t = [bg[s * LANES:(s + 1) * LANES, :].T for s in range(DELTA_TB // LANES)]
    cols, rws = {}, {}
    for g in range(DELTA_G):
        bgc = bg[g * c:(g + 1) * c, :]
        bgr = bg_t[g // per_slab][0:16, (g % per_slab) * c:(g % per_slab + 1) * c]
        cols[g] = (bgc, _mm_exact_lhs(tril, bgc), _mm_exact_lhs(triu, bgc))
        rws[g] = (_mm_exact_rhs(bgr, triu), _mm_exact_rhs(bgr, tril))
    beta = jnp.stack([cols[g][0][:, d * H_A + hh:d * H_A + hh + 1] for g, d, hh in insts])
    gcc = jnp.stack([cols[g][1 + d][:, 2 * H_A + d * H_A + hh:2 * H_A + d * H_A + hh + 1]
                     for g, d, hh in insts])
    gcr = jnp.stack([rws[g][d][2 * H_A + d * H_A + hh:2 * H_A + d * H_A + hh + 1, :] for g, d, hh in insts])
    glast = jnp.stack([gcc[i, (c - 1 if insts[i][1] == 0 else 0):(c if insts[i][1] == 0 else 1), :]
                       for i in range(n)])

    sign = jnp.stack([jnp.full((1, 1), 1 - 2 * d, jnp.int32) for g, d, hh in insts])
    tri = (ri - ci)[None] * sign
    incl = tri >= 0
    strict = tri > 0
    blk = jnp.broadcast_to(same_blk[None], incl.shape)

    kf = k.astype(F32)
    kb = kf * beta
    decay = jnp.where(incl, jnp.exp(jnp.where(incl, gcc - gcr, 0.0)), 0.0)
    a = _bmm_nt(kb, k) * decay
    qk = _bmm_nt(q, k) * (decay * scale)
    dg = jnp.where(blk & strict, a, 0.0)
    lo = jnp.where(jnp.logical_not(blk) & strict, a, 0.0)
    m1 = _bmm(dg, dg)
    p = m1 - dg - _bmm(dg, m1)
    pw = m1
    for _ in range(int(math.log2(SUB)) - 2):
        pw = _bmm(pw, pw)
        p = p + pw + _bmm(p, pw)
    nm = lo + _bmm(p, lo)
    n2 = _bmm(nm, nm)
    t1 = n2 - nm - _bmm(nm, n2)
    tm = t1 + p + _bmm(t1, p)
    e_col = jnp.exp(gcc)
    rhs = jnp.concatenate([v.astype(F32) * beta, kb * e_col], axis=2)
    x = rhs + _bmm(tm, rhs)
    u, w = x[:, :, :HEAD_DIM_A], x[:, :, HEAD_DIM_A:]
    qe = q.astype(F32) * (e_col * scale)
    ke = kf * jnp.exp(glast - gcc)
    dec = jnp.exp(glast)
    zpad = jnp.zeros((c, HEAD_DIM_A - c), BF16)
    for i, (g, d, hh) in enumerate(insts):
        rows = slice(g * c, (g + 1) * c)
        cs = slice(hh * HEAD_DIM_A, (hh + 1) * HEAD_DIM_A)
        u_ref, w_ref, qe_ref, ke_ref, qk_ref = outs[d]
        u_ref[0, rows, cs] = u[i]
        w_ref[0, rows, cs] = w[i].astype(BF16)
        qe_ref[0, rows, cs] = qe[i].astype(BF16)
        ke_ref[0, rows, cs] = ke[i].astype(BF16)
        qk_ref[0, rows, cs] = jnp.concatenate([qk[i].astype(BF16), zpad], axis=1)
        dec_ref[0, g, d * H_A + hh:d * H_A + hh + 1, :] = jnp.broadcast_to(dec[i], (1, HEAD_DIM_A))


def _delta_prep(qkv, bg, bsz, seq_len):
    nblk = seq_len // DELTA_TB
    nc = seq_len // CHUNK
    tokf = jax.ShapeDtypeStruct((bsz, seq_len, A_WIDTH), F32)
    tokb = jax.ShapeDtypeStruct((bsz, seq_len, A_WIDTH), BF16)
    tspec = pl.BlockSpec((1, DELTA_TB, A_WIDTH), lambda b, j: (b, j, 0))
    return pl.pallas_call(
        _delta_prep_kernel,
        grid=(bsz, nblk),
        in_specs=[pl.BlockSpec((DELTA_TB, 3 * A_WIDTH), lambda b, j: (b * nblk + j, 0)),
                  pl.BlockSpec((DELTA_TB, LANES), lambda b, j: (b * nblk + j, 0))],
        out_specs=[tspec] * 10 + [pl.BlockSpec((1, DELTA_G, 2 * H_A, HEAD_DIM_A), lambda b, j: (b, j, 0, 0))],
        out_shape=[tokf, tokb, tokb, tokb, tokb] * 2 + [jax.ShapeDtypeStruct((bsz, nc, 2 * H_A, HEAD_DIM_A), F32)],
        compiler_params=_cp(("parallel", "parallel")),
        name="delta_prep",
    )(qkv, bg)


def _delta_recur_kernel(*refs, nb, nblk, has_s0, emit_state):
    it = iter(refs)
    fwd = [next(it) for _ in range(5)]
    bwd = [next(it) for _ in range(5)]
    decf_ref, decb_ref = next(it), next(it)
    s0_ref = next(it) if has_s0 else None
    of_ref, ob_ref = next(it), next(it)
    sfin_ref = next(it) if emit_state else None
    s_scr = next(it)
    j = pl.program_id(1)
    c = CHUNK

    @pl.when(j == 0)
    def _():
        if has_s0:
            s_scr[...] = s0_ref[...]
        else:
            s_scr[...] = jnp.zeros(s_scr.shape, F32)

    insts = [(bi, d, hh) for bi in range(nb) for d in range(2) for hh in range(H_A)]
    for step in range(DELTA_G):
        gsel = (step, DELTA_G - 1 - step)

        def stk(idx, width=HEAD_DIM_A):
            return jnp.stack([(fwd, bwd)[d][idx][bi, gsel[d] * c:(gsel[d] + 1) * c,
                                                 hh * HEAD_DIM_A:hh * HEAD_DIM_A + width]
                              for bi, d, hh in insts])

        u, w, qe, ke, qk = stk(0), stk(1), stk(2), stk(3), stk(4, c)
        dec = jnp.stack([(decf_ref, decb_ref)[d][bi, gsel[d], d * H_A + hh:d * H_A + hh + 1, :]
                         for bi, d, hh in insts])
        s_old = jnp.stack([s_scr[bi, d, hh] for bi, d, hh in insts])
        r1 = _bmm(jnp.concatenate([w, qe], axis=1), s_old)
        v_new = (u - r1[:, :c, :]).astype(BF16)
        o = r1[:, c:, :] + _bmm(qk, v_new)
        s_new = s_old * dec + _bmm_tn(ke, v_new)
        for i, (bi, d, hh) in enumerate(insts):
            s_scr[bi, d, hh] = s_new[i]
            (of_ref, ob_ref)[d][bi, gsel[d] * c:(gsel[d] + 1) * c, hh * HEAD_DIM_A:(hh + 1) * HEAD_DIM_A] = o[i]

    if emit_state:
        @pl.when(j == nblk - 1)
        def _():
            sfin_ref[...] = s_scr[...]


def _delta_recur(prep, s0, bsz, seq_len, nb, emit_state):
    nblk = seq_len // DELTA_TB
    has_s0 = s0 is not None
    fspec = pl.BlockSpec((nb, DELTA_TB, A_WIDTH), lambda b, j: (b, j, 0))
    bspec = pl.BlockSpec((nb, DELTA_TB, A_WIDTH), lambda b, j: (b, nblk - 1 - j, 0))
    dspec_f = pl.BlockSpec((nb, DELTA_G, 2 * H_A, HEAD_DIM_A), lambda b, j: (b, j, 0, 0))
    dspec_b = pl.BlockSpec((nb, DELTA_G, 2 * H_A, HEAD_DIM_A), lambda b, j: (b, nblk - 1 - j, 0, 0))
    st_spec = pl.BlockSpec((nb, 2, H_A, HEAD_DIM_A, HEAD_DIM_A), lambda b, j: (b, 0, 0, 0, 0))
    in_specs = [fspec] * 5 + [bspec] * 5 + [dspec_f, dspec_b]
    args = list(prep[:10]) + [prep[10], prep[10]]
    if has_s0:
        in_specs.append(st_spec)
        args.append(s0)
    out_specs = [fspec, bspec]
    out_shape = [jax.ShapeDtypeStruct((bsz, seq_len, A_WIDTH), F32)] * 2
    if emit_state:
        out_specs.append(st_spec)
        out_shape.append(jax.ShapeDtypeStruct((bsz, 2, H_A, HEAD_DIM_A, HEAD_DIM_A), F32))
    return pl.pallas_call(
        functools.partial(_delta_recur_kernel, nb=nb, nblk=nblk, has_s0=has_s0, emit_state=emit_state),
        grid=(bsz // nb, nblk),
        in_specs=in_specs,
        out_specs=out_specs,
        out_shape=out_shape,
        scratch_shapes=[pltpu.VMEM((nb, 2, H_A, HEAD_DIM_A, HEAD_DIM_A), F32)],
        compiler_params=_cp(("parallel", "arbitrary")),
        name="delta_recur",
    )(*args)


def _s5prep_kernel(lre_ref, lim_ref, ldt_ref, bre_ref, bim_ref, cre_ref, cim_ref,
                   m_out, ef_out, eb_out, ff_out, fb_out, lt_out, xcat_scr, qcat_scr, ft_scr):
    gs, p, cg, t8, half = S5_HG, S5_STATE, S5_GROUP, S5_T, S5_K // 2
    j = (lax.broadcasted_iota(jnp.int32, (S5_ROWS, 1), 0) // cg).astype(F32)
    tile = lambda m: jnp.concatenate([m] * t8, axis=0)
    one = jnp.ones((1, 1), F32)
    bd = {}
    for d in range(2):
        e_out = (ef_out, eb_out)[d]
        f_out = (ff_out, fb_out)[d]
        xcat_scr[...] = jnp.zeros(xcat_scr.shape, F32)
        qcat_scr[...] = jnp.zeros(qcat_scr.shape, F32)
        ft_scr[...] = jnp.zeros(ft_scr.shape, F32)
        e_out[0] = jnp.zeros((S5_K, S5_K), BF16)
        for gl in range(gs):
            lre, lim = lre_ref[0, d, 0, gl:gl + 1, :], lim_ref[0, d, 0, gl:gl + 1, :]
            dt = jnp.exp(ldt_ref[0, d, 0, gl:gl + 1, :])
            zr, zi = lre * dt, lim * dt

            def lam_pow(e):
                mag = jnp.exp(e * zr)
                return mag * jnp.cos(e * zi), mag * jnp.sin(e * zi)

            l1r, l1i = lam_pow(one)
            den = lre * lre + lim * lim
            nr, ni = l1r - 1.0, l1i
            cfr = (nr * lre + ni * lim) / den
            cfi = (ni * lre - nr * lim) / den
            bre, bim = bre_ref[0, gl], bim_ref[0, gl]
            bbr = cfr * bre - cfi * bim
            bbi = cfr * bim + cfi * bre
            cre, cim = cre_ref[0, gl], cim_ref[0, gl]
            c_r, c_i = tile(cre), tile(cim)
            b_r, b_i = tile(bbr), tile(bbi)
            grow = slice(gl * cg, (gl + 1) * cg)
            lre_c, lim_c = slice(gl * p, (gl + 1) * p), slice(half + gl * p, half + (gl + 1) * p)

            xcat_scr[grow, 2 * gl * p:(2 * gl + 1) * p] = bbr
            xcat_scr[grow, (2 * gl + 1) * p:(2 * gl + 2) * p] = bbi
            pr, pi = lam_pow(j)
            qr = pr * c_r - pi * c_i
            qi = pr * c_i + pi * c_r
            for jj in range(t8):
                qcat_scr[jj, grow, 2 * gl * p:(2 * gl + 1) * p] = qr[jj * cg:(jj + 1) * cg, :]
                qcat_scr[jj, grow, (2 * gl + 1) * p:(2 * gl + 2) * p] = -qi[jj * cg:(jj + 1) * cg, :]

            er, ei = lam_pow(j if d else (t8 - 1.0) - j)
            e_r = (er * b_r - ei * b_i).astype(BF16)
            e_i = (er * b_i + ei * b_r).astype(BF16)
            fr, fi = lam_pow((t8 - j) if d else j + 1.0)
            f_r = fr * c_r - fi * c_i
            f_i = -(fr * c_i + fi * c_r)
            for s in range(t8):
                rows = slice(s * S5_TOKW + gl * cg, s * S5_TOKW + (gl + 1) * cg)
                e_out[0, rows, lre_c] = e_r[s * cg:(s + 1) * cg, :]
                e_out[0, rows, lim_c] = e_i[s * cg:(s + 1) * cg, :]
                ft_scr[rows, lre_c] = f_r[s * cg:(s + 1) * cg, :]
                ft_scr[rows, lim_c] = f_i[s * cg:(s + 1) * cg, :]

            ltr, lti = lam_pow(one * float(t8))
            lt_out[0, d:d + 1, lre_c] = ltr
            lt_out[0, d:d + 1, lim_c] = lti
        f_out[0] = ft_scr[...].T.astype(BF16)
        xcat = xcat_scr[...]
        for jj in range(t8):
            bd[(d, jj)] = _mm3_nt(xcat, qcat_scr[jj])
    for s in range(t8):
        for t in range(t8):
            blk = bd[(0, t - s)] if t > s else (bd[(1, s - t)] if s > t else bd[(0, 0)] + bd[(1, 0)])
            m_out[0, s * S5_TOKW:(s + 1) * S5_TOKW, t * S5_TOKW:(t + 1) * S5_TOKW] = blk.astype(BF16)


def _s5prep(lam_re, lam_im, log_dt, b_re, b_im, c_re, c_im):
    gs, p, cg = S5_HG, S5_STATE, S5_GROUP
    per_layer = S5_SG * S5_HALVES
    nq = DEPTH * per_layer
    lam_idx = lambda q: (q // per_layer, 0, q % per_layer, 0, 0)
    par_idx = lambda q: (q, 0, 0, 0)
    lam5 = lambda v, w: v.reshape(DEPTH, 2, per_layer, gs, w)
    b_t = lambda v: v.reshape(nq, gs, p, cg).transpose(0, 1, 3, 2)
    par4 = lambda v: v.reshape(nq, gs, cg, p)
    mat = pl.BlockSpec((1, S5_K, S5_K), lambda q: (q, 0, 0))
    return pl.pallas_call(
        _s5prep_kernel,
        grid=(nq,),
        in_specs=[pl.BlockSpec((1, 2, 1, gs, p), lam_idx), pl.BlockSpec((1, 2, 1, gs, p), lam_idx),
                  pl.BlockSpec((1, 2, 1, gs, 1), lam_idx)] + [pl.BlockSpec((1, gs, cg, p), par_idx)] * 4,
        out_specs=[mat] * 5 + [pl.BlockSpec((1, 2, S5_K), lambda q: (q, 0, 0))],
        out_shape=[jax.ShapeDtypeStruct((nq, S5_K, S5_K), BF16)] * 5 + [jax.ShapeDtypeStruct((nq, 2, S5_K), F32)],
        scratch_shapes=[pltpu.VMEM((S5_TOKW, S5_K), F32), pltpu.VMEM((S5_T, S5_TOKW, S5_K), F32),
                        pltpu.VMEM((S5_K, S5_K), F32)],
        compiler_params=_cp(("parallel",)),
        name="s5prep",
    )(lam5(lam_re, p), lam5(lam_im, p), lam5(log_dt, 1), b_t(b_re), b_t(b_im), par4(c_re), par4(c_im))


S5_RT = 256


def _s5_kernel(*refs, nchunk, nb, seq_len, has_h0, emit_state):
    it = iter(refs)
    u_ref = next(it)
    m_ref, ef_ref, eb_ref, ff_ref, fb_ref, lt_ref = (next(it) for _ in range(6))
    h0_ref = next(it) if has_h0 else None
    y_ref = next(it)
    hfin_ref = next(it) if emit_state else None
    up_scr, sf_scr, sb_scr = next(it), next(it), next(it)

    r = nchunk * nb
    rtile = min(S5_RT, r)
    half = S5_K // 2
    for t in range(S5_T):
        for b in range(nb):
            up_scr[t, pl.ds(b, nchunk, stride=nb), :] = u_ref[pl.ds(b * seq_len + t, nchunk, stride=S5_T), :]

    def low_lanes(shape):
        return lax.broadcasted_iota(jnp.int32, shape, 1) < S5_TOKW

    def u_tiles(rt):
        rows = slice(rt * rtile, (rt + 1) * rtile)
        low = low_lanes((rtile, LANES))
        halves = [[], []]
        for jt in range(S5_T // 2):
            pa, pb = up_scr[2 * jt, rows, :], up_scr[2 * jt + 1, rows, :]
            halves[0].append(jnp.where(low, pa, pltpu.roll(pb, S5_TOKW, 1)))
            halves[1].append(jnp.where(low, pltpu.roll(pa, S5_TOKW, 1), pb))
        return [jnp.concatenate(hv, axis=1).astype(BF16) for hv in halves]

    for rt in range(r // rtile):
        rows = slice(rt * rtile, (rt + 1) * rtile)
        for hv, ub in enumerate(u_tiles(rt)):
            cs = slice(hv * S5_K, (hv + 1) * S5_K)
            sf_scr[rows, cs] = jnp.dot(ub, ef_ref[0, hv], preferred_element_type=F32)
            sb_scr[rows, cs] = jnp.dot(ub, eb_ref[0, hv], preferred_element_type=F32)

    def swap_re_im(x):
        parts = []
        for hv in range(S5_HALVES):
            parts += [x[:, hv * S5_K + half:(hv + 1) * S5_K], x[:, hv * S5_K:hv * S5_K + half]]
        return jnp.concatenate(parts, axis=1)

    def coeffs(d):
        lt = jnp.concatenate([lt_ref[0, hv, d:d + 1, :] for hv in range(S5_HALVES)], axis=1)
        sw = swap_re_im(lt)
        is_re = (lax.broadcasted_iota(jnp.int32, (1, S5_W), 1) % S5_K) < half
        return jnp.where(is_re, lt, sw), jnp.where(is_re, -sw, lt)

    af, bf = coeffs(0)
    ab, bb = coeffs(1)
    if has_h0:
        h0f, h0b = h0_ref[0, 0, 0], h0_ref[0, 1, 0]
    else:
        h0f = h0b = jnp.zeros((nb, S5_W), F32)

    tile_rows = max(nb, 8)
    cps = tile_rows // nb
    n_iter = nchunk // cps

    def step(h, a, b, seg):
        return a * h + b * swap_re_im(h) + seg

    def body(i, carry):
        hf, hb = carry
        rf = pl.ds(pl.multiple_of(i * tile_rows, tile_rows), tile_rows)
        rb = pl.ds(pl.multiple_of((n_iter - 1 - i) * tile_rows, tile_rows), tile_rows)
        xf, xb = sf_scr[rf, :], sb_scr[rb, :]
        of, ob = [], [None] * cps
        for jj in range(cps):
            of.append(hf)
            hf = step(hf, af, bf, xf[jj * nb:(jj + 1) * nb, :])
        for jj in reversed(range(cps)):
            ob[jj] = hb
            hb = step(hb, ab, bb, xb[jj * nb:(jj + 1) * nb, :])
        sf_scr[rf, :] = of[0] if cps == 1 else jnp.concatenate(of, axis=0)
        sb_scr[rb, :] = ob[0] if cps == 1 else jnp.concatenate(ob, axis=0)
        return hf, hb

    hf, hb = lax.fori_loop(0, n_iter, body, (h0f, h0b))
    if emit_state:
        hfin_ref[0, 0, 0] = hf
        hfin_ref[0, 1, 0] = hb

    for rt in range(r // rtile):
        rows = slice(rt * rtile, (rt + 1) * rtile)
        ys = []
        for hv, ub in enumerate(u_tiles(rt)):
            cs = slice(hv * S5_K, (hv + 1) * S5_K)
            y = jnp.dot(ub, m_ref[0, hv], preferred_element_type=F32)
            y = y + jnp.dot(sf_scr[rows, cs].astype(BF16), ff_ref[0, hv], preferred_element_type=F32)
            ys.append(y + jnp.dot(sb_scr[rows, cs].astype(BF16), fb_ref[0, hv], preferred_element_type=F32))
        low = low_lanes((rtile, LANES))
        for jt in range(S5_T // 2):
            y0, y1 = ys[0][:, jt * LANES:(jt + 1) * LANES], ys[1][:, jt * LANES:(jt + 1) * LANES]
            up_scr[2 * jt, rows, :] = jnp.where(low, y0, pltpu.roll(y1, S5_TOKW, 1))
            up_scr[2 * jt + 1, rows, :] = jnp.where(low, pltpu.roll(y0, S5_TOKW, 1), y1)
    for t in range(S5_T):
        for b in range(nb):
            y_ref[pl.ds(b * seq_len + t, nchunk, stride=S5_T), :] = up_scr[t, pl.ds(b, nchunk, stride=nb), :]


def _s5(u, mats_l, h0, bsz, seq_len, nb, emit_state):
    nchunk = seq_len // S5_T
    r = nchunk * nb
    has_h0 = h0 is not None
    qi = lambda q, j: (q, 0, 0, 0)
    wspec = pl.BlockSpec((1, S5_HALVES, S5_K, S5_K), qi)
    in_specs = [pl.BlockSpec((nb * seq_len, LANES), lambda q, j: (j, q))] + [wspec] * 5 \
        + [pl.BlockSpec((1, S5_HALVES, 2, S5_K), qi)]
    args = [u] + list(mats_l)
    st_spec = pl.BlockSpec((1, 2, 1, nb, S5_W), lambda q, j: (q, 0, j, 0, 0))
    st_shape = (S5_SG, 2, bsz // nb, nb, S5_W)
    if has_h0:
        in_specs.append(st_spec)
        args.append(h0.reshape(st_shape))
    out_specs = [pl.BlockSpec((nb * seq_len, LANES), lambda q, j: (j, q))]
    out_shape = [jax.ShapeDtypeStruct((bsz * seq_len, B_WIDTH), F32)]
    if emit_state:
        out_specs.append(st_spec)
        out_shape.append(jax.ShapeDtypeStruct(st_shape, F32))
    outs = pl.pallas_call(
        functools.partial(_s5_kernel, nchunk=nchunk, nb=nb, seq_len=seq_len, has_h0=has_h0, emit_state=emit_state),
        grid=(S5_SG, bsz // nb),
        in_specs=in_specs,
        out_specs=out_specs,
        out_shape=out_shape,
        scratch_shapes=[pltpu.VMEM((S5_T, r, LANES), F32), pltpu.VMEM((r, S5_W), F32), pltpu.VMEM((r, S5_W), F32)],
        compiler_params=_cp(("parallel", "parallel")),
        name="s5",
    )(*args)
    if emit_state:
        return outs[0], outs[1].reshape(S5_SG, 2, bsz, S5_W)
    return (outs[0],)


def _merge_kernel(x_ref, shift_ref, scale_ref, gate_ref, gnorm_ref, of_ref, ob_ref, zs_ref, gout_ref, ys_ref, u_ref,
                  oc_ref, wg_ref, wba_ref, wbb_ref, wbc_ref, wout_ref, wglu_ref, bglu_ref, dskip_ref, xo_ref):
    x = x_ref[...]
    h = ((_rms(x) * gnorm_ref[...]) * (1.0 + scale_ref[0]) + shift_ref[0]).astype(BF16)
    od = of_ref[...] + ob_ref[...]
    gout = gout_ref[...]
    oa = jnp.concatenate([_rms(od[:, hh * HEAD_DIM_A:(hh + 1) * HEAD_DIM_A]) * gout for hh in range(H_A)], axis=1)
    oa = oa * zs_ref[...]
    yb = jax.nn.gelu(ys_ref[...] + dskip_ref[...] * u_ref[...])
    ob = yb * jax.nn.sigmoid(_mm(yb, wglu_ref[...]) + bglu_ref[...])
    acc = None
    for nbr, (o, w) in enumerate(((oa, wba_ref), (ob, wbb_ref), (oc_ref[...], wbc_ref))):
        gate = jax.nn.sigmoid(jnp.dot(h, wg_ref[:, nbr * D_MODEL:(nbr + 1) * D_MODEL], preferred_element_type=F32))
        term = gate * _mm(o, w[...])
        acc = term if acc is None else acc + term
    out = _mm(acc, wout_ref[...])
    xo_ref[...] = x + gate_ref[0] * out


def _merge(x, mods_rows, row_of_tile, lp, o_f, o_b, zs, ys, u, oc, tl):
    n = x.shape[0]
    const2 = lambda i: (0, 0)
    tok = lambda w: pl.BlockSpec((tl, w), lambda i: (i, 0))
    mod = lambda j: pl.BlockSpec((1, 1, D_MODEL), lambda i: (row_of_tile(i) * 6 + j, 0, 0))
    return pl.pallas_call(
        _merge_kernel,
        grid=(n // tl,),
        in_specs=[tok(D_MODEL), mod(0), mod(1), mod(2), pl.BlockSpec((1, D_MODEL), const2),
                  tok(A_WIDTH), tok(A_WIDTH), tok(A_WIDTH), pl.BlockSpec((1, HEAD_DIM_A), const2),
                  tok(B_WIDTH), tok(B_WIDTH), tok(BRANCH_WIDTH),
                  pl.BlockSpec((D_MODEL, N_BRANCH * D_MODEL), const2),
                  pl.BlockSpec((BRANCH_WIDTH, D_MODEL), const2), pl.BlockSpec((BRANCH_WIDTH, D_MODEL), const2),
                  pl.BlockSpec((BRANCH_WIDTH, D_MODEL), const2), pl.BlockSpec((D_MODEL, D_MODEL), const2),
                  pl.BlockSpec((B_WIDTH, B_WIDTH), const2), pl.BlockSpec((1, B_WIDTH), const2),
                  pl.BlockSpec((1, B_WIDTH), const2)],
        out_specs=tok(D_MODEL),
        out_shape=jax.ShapeDtypeStruct((n, D_MODEL), F32),
        compiler_params=_cp(("parallel",)),
        name="merge",
    )(x, mods_rows, mods_rows, mods_rows, lp['g_norm_mix'], o_f, o_b, zs, lp['g_delta_out'], ys, u, oc,
      lp['w_gates'], lp['w_br_a'], lp['w_br_b'],
      lp['w_br_c'], lp['w_out'], lp['w_glu'], lp['b_glu'], lp['s5_d'])


FF_BLK = 256


def _ffn_kernel(*refs, tl, tiles_per_seq, final):
    it = iter(refs)
    xp_ref, xc_ref, xn_ref = next(it), next(it), next(it)
    shift_ref, scale_ref, gate_ref, gnorm_ref = next(it), next(it), next(it), next(it)
    wup_ref, convw_ref, convb_ref, wdown_ref = next(it), next(it), next(it), next(it)
    gfin_ref = next(it) if final else None
    xo_ref = next(it)
    yo_ref = next(it) if final else None
    act_scr = next(it)

    i = pl.program_id(0)
    pos = i % tiles_per_seq
    mod_scale = 1.0 + scale_ref[0]
    mod_shift = shift_ref[0]
    gain = gnorm_ref[...]

    def norm_mod(x):
        return ((_rms(x) * gain) * mod_scale + mod_shift).astype(BF16)

    x = xc_ref[...]
    h_prev = norm_mod(xp_ref[...]) * (pos > 0).astype(BF16)
    h_next = norm_mod(xn_ref[...]) * (pos < tiles_per_seq - 1).astype(BF16)
    h_ext = jnp.concatenate([h_prev, norm_mod(x), h_next], axis=0)
    n_ext = tl + 2 * HALO
    pad = FFN_CONV // 2

    def conv_act(cols):
        up = jnp.dot(h_ext, wup_ref[:, cols], preferred_element_type=F32)
        acc = None
        for t in range(FFN_CONV):
            sh = (pad - t) % n_ext
            src = up if sh == 0 else pltpu.roll(up, sh, 0)
            term = src[HALO:HALO + tl, :] * convw_ref[t:t + 1, cols]
            acc = term if acc is None else acc + term
        return acc + convb_ref[:, cols]

    for j in range(D_FF // FF_BLK):
        gcols = slice(j * FF_BLK, (j + 1) * FF_BLK)
        vcols = slice(D_FF + j * FF_BLK, D_FF + (j + 1) * FF_BLK)
        act_scr[:, gcols] = (_silu(conv_act(gcols)) * conv_act(vcols)).astype(BF16)
    out = jnp.dot(act_scr[...], wdown_ref[...], preferred_element_type=F32)
    xo = x + gate_ref[0] * out
    xo_ref[...] = xo
    if final:
        yo_ref[...] = _rms(xo) * gfin_ref[...]


def _ffn(x, mods_rows, row_of_tile, lp, seq_len, tl, g_final):
    n = x.shape[0]
    tps = seq_len // tl
    final = g_final is not None
    xprev, xnext = _halo_maps(n, tl)
    const2 = lambda i: (0, 0)
    mod = lambda j: pl.BlockSpec((1, 1, D_MODEL), lambda i: (row_of_tile(i) * 6 + j, 0, 0))
    tok = pl.BlockSpec((tl, D_MODEL), lambda i: (i, 0))
    in_specs = [pl.BlockSpec((HALO, D_MODEL), xprev), tok, pl.BlockSpec((HALO, D_MODEL), xnext),
                mod(3), mod(4), mod(5), pl.BlockSpec((1, D_MODEL), const2),
                pl.BlockSpec((D_MODEL, 2 * D_FF), const2), pl.BlockSpec((FFN_CONV, 2 * D_FF), const2),
                pl.BlockSpec((1, 2 * D_FF), const2), pl.BlockSpec((D_FF, D_MODEL), const2)]
    args = [x, x, x, mods_rows, mods_rows, mods_rows, lp['g_norm_ffn'], lp['w_ffn_up'], lp['conv_ffn'],
            lp['b_conv_ffn'], lp['w_ffn_down']]
    out_specs = [tok]
    out_shape = [jax.ShapeDtypeStruct((n, D_MODEL), F32)]
    if final:
        in_specs.append(pl.BlockSpec((1, D_MODEL), const2))
        args.append(g_final)
        out_specs.append(tok)
        out_shape.append(jax.ShapeDtypeStruct((n, D_MODEL), F32))
    return pl.pallas_call(
        functools.partial(_ffn_kernel, tl=tl, tiles_per_seq=tps, final=final),
        grid=(n // tl,),
        in_specs=in_specs,
        out_specs=out_specs,
        out_shape=out_shape,
        scratch_shapes=[pltpu.VMEM((tl, D_FF), BF16)],
        compiler_params=_cp(("parallel",)),
        name="ffn",
    )(*args)


def _pad_heads(w, head_w, n_heads):
    k = w.shape[0]
    w = w.reshape(k, n_heads, head_w)
    w = jnp.pad(w, ((0, 0), (0, 0), (0, HEAD_PAD - head_w)))
    return w.reshape(k, n_heads * HEAD_PAD)


def _layer_params(l, p):
    w_in = p['w_in'][l]
    o = 0
    parts = {}
    for name, wd in (('qkv', 3 * A_WIDTH), ('z', A_WIDTH), ('beta', 2 * H_A), ('alpha', 2 * H_A), ('u', B_WIDTH),
                     ('qa', Q_LORA), ('kva', KV_LORA), ('kr', QK_ROPE), ('gates', N_BRANCH * D_MODEL)):
        parts[name] = w_in[:, o:o + wd]
        o += wd
    small = jnp.concatenate([parts['beta'], parts['alpha'], parts['kr'],
                             jnp.zeros((D_MODEL, LANES - 4 * H_A - QK_ROPE), F32)], axis=1)
    w_rest = jnp.concatenate([parts['z'], parts['u'], parts['qa'], parts['kva'], small], axis=1)
    pad8 = lambda v: jnp.pad(v.reshape(1, 2 * H_A), ((0, 0), (2 * H_A, LANES - 4 * H_A)))
    w_kv = p['w_kv_b'][l].reshape(KV_LORA, H_C, QK_NOPE + V_HEAD)
    w_k = w_kv[:, :, :QK_NOPE].reshape(KV_LORA, H_C * QK_NOPE)
    w_v = w_kv[:, :, QK_NOPE:].reshape(KV_LORA, H_C * V_HEAD)
    row = lambda v: v.reshape(1, -1)
    return {
        'g_norm_mix': row(p['g_norm_mix'][l]), 'g_norm_ffn': row(p['g_norm_ffn'][l]),
        'w_qkv': parts['qkv'].astype(BF16), 'w_rest': w_rest.astype(BF16), 'w_gates': parts['gates'].astype(BF16),
        'conv_qkv': p['conv_qkv'][l],
        'a_log128': pad8(p['a_log'][l]), 'dt_bias128': pad8(p['dt_bias'][l]),
        'g_delta_out': row(p['g_delta_out'][l]),
        'g_q_a': row(p['g_q_a'][l]), 'g_kv_a': row(p['g_kv_a'][l]),
        'w_q_pad': _pad_heads(p['w_q_b'][l], QK_NOPE + QK_ROPE, H_C).astype(BF16),
        'w_k_pad': _pad_heads(w_k, QK_NOPE, H_C).astype(BF16),
        'w_v_pad': _pad_heads(w_v, V_HEAD, H_C).astype(BF16),
        'w_br_a': p['w_branch'][l, 0].astype(BF16), 'w_br_b': p['w_branch'][l, 1].astype(BF16),
        'w_br_c': p['w_branch'][l, 2].astype(BF16),
        'w_out': p['w_out'][l].astype(BF16),
        'w_glu': p['w_glu'][l].astype(BF16), 'b_glu': row(p['b_glu'][l]), 's5_d': row(p['s5_d'][l]),
        'w_ffn_up': p['w_ffn_up'][l].astype(BF16), 'conv_ffn': p['conv_ffn'][l],
        'b_conv_ffn': row(p['b_conv_ffn'][l]), 'w_ffn_down': p['w_ffn_down'][l].astype(BF16),
    }


def _rope_tables(length):
    rows = length // GRID_W
    row = jnp.repeat(jnp.arange(rows, dtype=F32), GRID_W)
    col = (jnp.arange(length) % GRID_W).astype(F32)
    n_freq = QK_ROPE // 4
    inv_freq = 1.0 / (ROPE_BASE ** (jnp.arange(n_freq, dtype=F32) / n_freq))
    ang = jnp.concatenate([row[:, None] * inv_freq, col[:, None] * inv_freq], axis=-1)
    cos, sin = jnp.cos(ang), jnp.sin(ang)
    ones = jnp.ones((length, QK_NOPE), F32)
    zeros = jnp.zeros((length, QK_NOPE), F32)
    tail = jnp.zeros((length, HEAD_PAD - QK_NOPE - QK_ROPE), F32)
    cos_t = jnp.concatenate([ones, cos, cos, tail], axis=1)
    sin_t = jnp.concatenate([zeros, -sin, sin, tail], axis=1)
    return cos_t, sin_t


def _layer(x, bsz, seq_len, mods_rows, row_of_tile, lp, s5mats, ctx, rope_tabs, tl, g_final):
    is_ctx = ctx is None
    outs = _inproj(x, mods_rows, row_of_tile, lp, seq_len, tl, rope_tabs, emit_ctx=is_ctx)
    if is_ctx:
        qkv, zs, u, bg, ckv, kr, qh, kh, vh = outs
    else:
        qkv, zs, u, bg, qh, kh, vh = outs

    prep = _delta_prep(qkv, bg, bsz, seq_len)
    d_out = _delta_recur(prep, None if is_ctx else ctx['s0'], bsz, seq_len, min(bsz, DELTA_NB), emit_state=is_ctx)
    o_f = d_out[0].reshape(bsz * seq_len, A_WIDTH)
    o_b = d_out[1].reshape(bsz * seq_len, A_WIDTH)

    s5_nb = max(1, min(bsz, (S5_STEP_ROWS * S5_T) // seq_len))
    s_out = _s5(u, s5mats, None if is_ctx else ctx['h0'], bsz, seq_len, s5_nb, emit_state=is_ctx)
    ys = s_out[0]

    segs = [(kh, vh, seq_len)]
    if not is_ctx:
        segs = [(ctx['kh'], ctx['vh'], ctx['past'])] + segs
    oc = _attention(qh, segs, bsz, seq_len, min(seq_len, ATT_TQ),
                    H_C if seq_len <= ATT_BATCHED_MAX else ATT_HEADS_LONG)

    x = _merge(x, mods_rows, row_of_tile, lp, o_f, o_b, zs, ys, u, oc, tl)
    f_out = _ffn(x, mods_rows, row_of_tile, lp, seq_len, tl, g_final)
    extras = None
    if is_ctx:
        hfin = s_out[1]
        hfin = hfin.reshape(S5_SG, 2, bsz, S5_HALVES, 2, S5_HG, S5_STATE).transpose(4, 2, 1, 0, 3, 5, 6)
        hfin = hfin.reshape(2, bsz, 2, S5_GROUPS, S5_STATE)
        extras = (d_out[2], hfin[0], hfin[1], ckv.reshape(bsz, seq_len, KV_LORA), kr.reshape(bsz, seq_len, QK_ROPE))
    return f_out, extras


def kernel(x_prompt, x_sample, state_delta, state_s5_re, state_s5_im, cache_ckv, cache_krope, c, c_ctx, w_mod, b_mod, g_norm_mix, g_norm_ffn, w_in, conv_qkv, a_log, dt_bias, g_delta_out, s5_lam_re, s5_lam_im, s5_log_dt, s5_b_re, s5_b_im, s5_c_re, s5_c_im, s5_d, w_glu, b_glu, g_q_a, w_q_b, g_kv_a, w_kv_b, w_branch, w_out, w_ffn_up, conv_ffn, b_conv_ffn, w_ffn_down, g_final):
    p = dict(g_norm_mix=g_norm_mix, g_norm_ffn=g_norm_ffn, w_in=w_in, conv_qkv=conv_qkv, a_log=a_log,
             dt_bias=dt_bias, g_delta_out=g_delta_out, s5_d=s5_d, w_glu=w_glu, b_glu=b_glu, g_q_a=g_q_a,
             w_q_b=w_q_b, g_kv_a=g_kv_a, w_kv_b=w_kv_b, w_branch=w_branch, w_out=w_out, w_ffn_up=w_ffn_up,
             conv_ffn=conv_ffn, b_conv_ffn=b_conv_ffn, w_ffn_down=w_ffn_down)
    bp, lp_len, _ = x_prompt.shape
    bs, ls_len, _ = x_sample.shape
    past = cache_ckv.shape[2]
    depth = w_in.shape[0]

    mod_rows = MOD_ROWS
    cvec = jnp.concatenate([c_ctx[None, :], c, jnp.zeros((mod_rows - 1 - bs, D_MODEL), F32)], axis=0).astype(F32)
    mods = _modulation(cvec, w_mod, b_mod)

    s5_all = _s5prep(s5_lam_re, s5_lam_im, s5_log_dt, s5_b_re, s5_b_im, s5_c_re, s5_c_im)

    rope_tabs = _rope_tables(ls_len)
    tl_p = min(lp_len, TOKEN_TILE)
    tl_s = min(ls_len, TOKEN_TILE)
    tiles_per_seq_s = ls_len // tl_s

    def sg_state(re, im):
        t = jnp.stack([re, im], axis=0).astype(F32).reshape(2, bs, 2, S5_SG, S5_HALVES, S5_HG, S5_STATE)
        return t.transpose(3, 2, 1, 4, 0, 5, 6).reshape(S5_SG, 2, bs, S5_W)

    xp = x_prompt.astype(F32).reshape(bp * lp_len, D_MODEL)
    xs = x_sample.astype(F32).reshape(bs * ls_len, D_MODEL)
    deltas, s5_res, s5_ims, ckvs, kropes = [], [], [], [], []
    yp = ys = None
    for l in range(depth):
        lp = _layer_params(l, p)
        mods_rows = mods[l].reshape(mod_rows * 6, 1, D_MODEL)
        per_layer = S5_SG * S5_HALVES
        s5mats = tuple(m[l * per_layer:(l + 1) * per_layer].reshape((S5_SG, S5_HALVES) + m.shape[1:]) for m in s5_all)
        last = l == depth - 1
        gfin = g_final.reshape(1, D_MODEL) if last else None

        out_p, extras = _layer(xp, bp, lp_len, mods_rows, lambda i: 0, lp, s5mats, None, None, tl_p, gfin)
        deltas.append(extras[0])
        s5_res.append(extras[1])
        s5_ims.append(extras[2])
        ckvs.append(extras[3])
        kropes.append(extras[4])

        kh_c, vh_c = _kvcache(cache_ckv[:, l].astype(F32).reshape(bs * past, KV_LORA),
                              cache_krope[:, l].astype(F32).reshape(bs * past, QK_ROPE), lp, past)
        ctx = dict(s0=state_delta[:, l].astype(F32), h0=sg_state(state_s5_re[:, l], state_s5_im[:, l]),
                   kh=kh_c, vh=vh_c, past=past)
        out_s, _ = _layer(xs, bs, ls_len, mods_rows, lambda i: 1 + i // tiles_per_seq_s, lp, s5mats, ctx, rope_tabs,
                          tl_s, gfin)
        xp, xs = out_p[0], out_s[0]
        if last:
            yp, ys = out_p[1], out_s[1]

    y_prompt = yp.reshape(bp, lp_len, D_MODEL)
    y_sample = ys.reshape(bs, ls_len, D_MODEL)
    return (y_prompt, y_sample, jnp.stack(deltas, axis=1), jnp.stack(s5_res, axis=1), jnp.stack(s5_ims, axis=1),
            jnp.stack(ckvs, axis=1), jnp.stack(kropes, axis=1))
```

```python
import functools
import math

import jax
import jax.numpy as jnp
from jax import lax
from jax.experimental import pallas as pl
from jax.experimental.pallas import tpu as pltpu

F32 = jnp.float32
BF16 = jnp.bfloat16

D_MODEL = 1024
DEPTH = 4
GRID_W = 64
H_A = 4
HEAD_DIM_A = 128
A_WIDTH = H_A * HEAD_DIM_A
SHORT_CONV = 5
CHUNK = 64
S5_GROUP = 16
S5_STATE = 64
B_WIDTH = 512
S5_GROUPS = B_WIDTH // S5_GROUP
H_C = 8
QK_NOPE = 64
QK_ROPE = 32
V_HEAD = 64
Q_LORA = 384
KV_LORA = 256
ROPE_BASE = 10000.0
N_BRANCH = 3
BRANCH_WIDTH = 512
D_FF = 2816
FFN_CONV = 3
NORM_EPS = 1e-6

LANES = 128
HEAD_PAD = LANES
SUB = 16
S5_T = 8
S5_ROWS = S5_T * S5_GROUP
S5_SG = B_WIDTH // LANES
S5_HALVES = 2
S5_HG = LANES // S5_HALVES // S5_GROUP
S5_TOKW = S5_HG * S5_GROUP
S5_K = S5_T * S5_TOKW
S5_W = S5_HALVES * S5_K

VMEM_LIMIT = 56 * 1024 * 1024
HALO = 16
TOKEN_TILE = 512
ATT_TQ = 1024
ATT_HEADS_LONG = 4
DELTA_NB = 4
S5_STEP_ROWS = 1024
S5_MAX_NB = 8
MOD_ROWS = 16


def _cp(sem):
    return pltpu.CompilerParams(dimension_semantics=sem, vmem_limit_bytes=VMEM_LIMIT)


def _mm(a, b):
    return jnp.dot(a.astype(BF16), b.astype(BF16), preferred_element_type=F32)


def _bmm(a, b):
    return lax.dot_general(a.astype(BF16), b.astype(BF16), (((2,), (1,)), ((0,), (0,))),
                           preferred_element_type=F32)


def _bmm_nt(a, b):
    return lax.dot_general(a.astype(BF16), b.astype(BF16), (((2,), (2,)), ((0,), (0,))),
                           preferred_element_type=F32)


def _bmm_tn(a, b):
    return lax.dot_general(a.astype(BF16), b.astype(BF16), (((1,), (1,)), ((0,), (0,))),
                           preferred_element_type=F32)


def _split3(x):
    x1 = x.astype(BF16)
    r = x - x1.astype(F32)
    x2 = r.astype(BF16)
    x3 = (r - x2.astype(F32)).astype(BF16)
    return x1, x2, x3


def _mm_exact_rhs(a, b_bf16):
    a1, a2, a3 = _split3(a)
    d = lambda t: jnp.dot(t, b_bf16, preferred_element_type=F32)
    return d(a1) + d(a2) + d(a3)


def _mm_exact_lhs(a_bf16, b):
    b1, b2, b3 = _split3(b)
    d = lambda t: jnp.dot(a_bf16, t, preferred_element_type=F32)
    return d(b1) + d(b2) + d(b3)


def _mm3_nt(a, b):
    a1 = a.astype(BF16)
    a2 = (a - a1.astype(F32)).astype(BF16)
    b1 = b.astype(BF16)
    b2 = (b - b1.astype(F32)).astype(BF16)
    d = lambda s, t: lax.dot_general(s, t, (((1,), (1,)), ((), ())), preferred_element_type=F32)
    return d(a1, b1) + d(a1, b2) + d(a2, b1)


def _silu(x):
    return x * jax.nn.sigmoid(x)


def _rms(x):
    return x * lax.rsqrt(jnp.mean(x * x, axis=-1, keepdims=True) + NORM_EPS)


def _mod_kernel(c_ref, w_ref, b_ref, o_ref):
    s = _silu(c_ref[...])
    o_ref[0] = _mm(s, w_ref[0]) + b_ref[0]


def _modulation(cvec, w_mod, b_mod):
    rows = cvec.shape[0]
    nblk = w_mod.shape[-1] // D_MODEL
    return pl.pallas_call(
        _mod_kernel,
        grid=(DEPTH, nblk),
        in_specs=[
            pl.BlockSpec((rows, D_MODEL), lambda l, j: (0, 0)),
            pl.BlockSpec((1, D_MODEL, D_MODEL), lambda l, j: (l, 0, j)),
            pl.BlockSpec((1, 1, D_MODEL), lambda l, j: (l, 0, j)),
        ],
        out_specs=pl.BlockSpec((1, rows, D_MODEL), lambda l, j: (l, 0, j)),
        out_shape=jax.ShapeDtypeStruct((DEPTH, rows, w_mod.shape[-1]), F32),
        compiler_params=_cp(("parallel", "parallel")),
        name="modulation",
    )(cvec, w_mod, b_mod.reshape(DEPTH, 1, -1))


REST_W = A_WIDTH + B_WIDTH + Q_LORA + KV_LORA + LANES
OFF_Z, OFF_U, OFF_QA = 0, A_WIDTH, A_WIDTH + B_WIDTH
OFF_KVA, OFF_SM = OFF_QA + Q_LORA, OFF_QA + Q_LORA + KV_LORA
SM_KR = 16
MLA_W = H_C * HEAD_PAD
INPROJ_SUB = 256


def _rope_apply(x, cos, sin):
    lane = lax.broadcasted_iota(jnp.int32, x.shape, 1)
    partner = jnp.where(lane < QK_NOPE + QK_ROPE // 2,
                        pltpu.roll(x, HEAD_PAD - QK_ROPE // 2, 1),
                        pltpu.roll(x, QK_ROPE // 2, 1))
    return x * cos + partner * sin


def _inproj_kernel(*refs, tl, tiles_per_seq, rope, emit_ctx):
    it = iter(refs)
    xp_ref, xc_ref, xn_ref = next(it), next(it), next(it)
    shift_ref, scale_ref, gnorm_ref = next(it), next(it), next(it)
    wqkv_ref, convw_ref, wrest_ref = next(it), next(it), next(it)
    gqa_ref, gkva_ref, alog_ref, dtb_ref = next(it), next(it), next(it), next(it)
    wq_ref, wk_ref, wv_ref = next(it), next(it), next(it)
    cos_ref = sin_ref = None
    if rope:
        cos_ref, sin_ref = next(it), next(it)
    qkv_out, zs_out, u_out, bg_out = next(it), next(it), next(it), next(it)
    ckv_out = kr_out = None
    if emit_ctx:
        ckv_out, kr_out = next(it), next(it)
    qh_out, kh_out, vh_out = next(it), next(it), next(it)

    i = pl.program_id(0)
    pos = i % tiles_per_seq
    mod_scale = 1.0 + scale_ref[0]
    mod_shift = shift_ref[0]
    gain = gnorm_ref[...]

    def norm_mod(x):
        return ((_rms(x) * gain) * mod_scale + mod_shift).astype(BF16)

    nsub = tl // INPROJ_SUB
    ts = INPROJ_SUB
    n_ext = ts + 2 * HALO
    rowi = lax.broadcasted_iota(jnp.int32, (n_ext, 1), 0)
    pad = SHORT_CONV // 2
    qscale = (QK_NOPE + QK_ROPE) ** -0.5 * math.log2(math.e)
    for sub in range(nsub):
        r0 = sub * ts
        rows = slice(r0, r0 + ts)
        h_cur = norm_mod(xc_ref[rows, :])
        h_before = norm_mod(xp_ref[...] if sub == 0 else xc_ref[r0 - HALO:r0, :])
        h_after = norm_mod(xn_ref[...] if sub == nsub - 1 else xc_ref[r0 + ts:r0 + ts + HALO, :])
        h_ext = jnp.concatenate([h_before, h_cur, h_after], axis=0)

        qkv_ext = _mm(h_ext, wqkv_ref[...])
        if sub == 0 or sub == nsub - 1:
            ok_lo = jnp.logical_or(pos > 0, rowi >= HALO) if sub == 0 else True
            ok_hi = jnp.logical_or(pos < tiles_per_seq - 1, rowi < HALO + ts) if sub == nsub - 1 else True
            qkv_ext = qkv_ext * jnp.logical_and(ok_lo, ok_hi).astype(F32)
        for blk in range(3):
            cols = slice(blk * A_WIDTH, (blk + 1) * A_WIDTH)
            xblk = qkv_ext[:, cols]
            acc = None
            for t in range(SHORT_CONV):
                sh = (pad - t) % n_ext
                src = xblk if sh == 0 else pltpu.roll(xblk, sh, 0)
                term = src[HALO:HALO + ts, :] * convw_ref[t:t + 1, cols]
                acc = term if acc is None else acc + term
            acc = _silu(acc)
            if blk < 2:
                parts = []
                for hh in range(H_A):
                    a = acc[:, hh * HEAD_DIM_A:(hh + 1) * HEAD_DIM_A]
                    parts.append(a * lax.rsqrt(jnp.sum(a * a, axis=-1, keepdims=True) + NORM_EPS))
                acc = jnp.concatenate(parts, axis=1)
            qkv_out[rows, cols] = acc.astype(BF16)

        rest = _mm(h_cur, wrest_ref[...])
        zs_out[rows, :] = _silu(rest[:, OFF_Z:OFF_Z + A_WIDTH])
        u_out[rows, :] = rest[:, OFF_U:OFF_U + B_WIDTH]

        small = rest[:, OFF_SM:OFF_SM + LANES]
        lane = lax.broadcasted_iota(jnp.int32, small.shape, 1)
        beta = jax.nn.sigmoid(small)
        glog = -jnp.exp(alog_ref[...]) * jax.nn.softplus(small + dtb_ref[...])
        bg_out[rows, :] = jnp.where(lane < 2 * H_A, beta, glog)

        qa = _rms(rest[:, OFF_QA:OFF_QA + Q_LORA]) * gqa_ref[...]
        ckv = _rms(rest[:, OFF_KVA:OFF_KVA + KV_LORA]) * gkva_ref[...]
        if emit_ctx:
            ckv_out[rows, :] = ckv
            kr_out[rows, :] = small[:, SM_KR:SM_KR + QK_ROPE]
        ckv_b = ckv.astype(BF16)
        qhat = _mm(qa, wq_ref[...])
        khat = _mm(ckv_b, wk_ref[...])
        vh_out[rows, :] = _mm(ckv_b, wv_ref[...]).astype(BF16)
        kr_al = jnp.where((lane >= QK_NOPE) & (lane < QK_NOPE + QK_ROPE),
                          pltpu.roll(small, QK_NOPE - SM_KR, 1), 0.0)
        if rope:
            cos, sin = cos_ref[rows, :], sin_ref[rows, :]
            kr_al = _rope_apply(kr_al, cos, sin)
        for hh in range(H_C):
            cs = slice(hh * HEAD_PAD, (hh + 1) * HEAD_PAD)
            qh = qhat[:, cs]
            if rope:
                qh = _rope_apply(qh, cos, sin)
            qh_out[rows, cs] = (qh * qscale).astype(BF16)
            kh_out[rows, cs] = (khat[:, cs] + kr_al).astype(BF16)


def _halo_maps(n, tl):
    nh = n // HALO

    def xprev(i):
        return (jnp.maximum(i * (tl // HALO) - 1, 0), 0)

    def xnext(i):
        return (jnp.minimum((i + 1) * (tl // HALO), nh - 1), 0)

    return xprev, xnext


def _inproj(x, mods_rows, row_of_tile, lp, seq_len, tl, rope_tabs, emit_ctx):
    n = x.shape[0]
    nt = n // tl
    tps = seq_len // tl
    rope = rope_tabs is not None
    xprev, xnext = _halo_maps(n, tl)
    const2 = lambda i: (0, 0)
    in_specs = [
        pl.BlockSpec((HALO, D_MODEL), xprev),
        pl.BlockSpec((tl, D_MODEL), lambda i: (i, 0)),
        pl.BlockSpec((HALO, D_MODEL), xnext),
        pl.BlockSpec((1, 1, D_MODEL), lambda i: (row_of_tile(i) * 6 + 0, 0, 0)),
        pl.BlockSpec((1, 1, D_MODEL), lambda i: (row_of_tile(i) * 6 + 1, 0, 0)),
        pl.BlockSpec((1, D_MODEL), const2),
        pl.BlockSpec((D_MODEL, 3 * A_WIDTH), const2),
        pl.BlockSpec((SHORT_CONV, 3 * A_WIDTH), const2),
        pl.BlockSpec((D_MODEL, REST_W), const2),
        pl.BlockSpec((1, Q_LORA), const2),
        pl.BlockSpec((1, KV_LORA), const2),
        pl.BlockSpec((1, LANES), const2),
        pl.BlockSpec((1, LANES), const2),
        pl.BlockSpec((Q_LORA, MLA_W), const2),
        pl.BlockSpec((KV_LORA, MLA_W), const2),
        pl.BlockSpec((KV_LORA, MLA_W), const2),
    ]
    args = [x, x, x, mods_rows, mods_rows, lp['g_norm_mix'], lp['w_qkv'], lp['conv_qkv'], lp['w_rest'],
            lp['g_q_a'], lp['g_kv_a'], lp['a_log128'], lp['dt_bias128'], lp['w_q_pad'], lp['w_k_pad'], lp['w_v_pad']]
    if rope:
        in_specs += [pl.BlockSpec((tl, HEAD_PAD), lambda i: (i % tps, 0))] * 2
        args += list(rope_tabs)
    tok = lambda w: pl.BlockSpec((tl, w), lambda i: (i, 0))
    out_specs = [tok(3 * A_WIDTH), tok(A_WIDTH), tok(B_WIDTH), tok(LANES)]
    out_shape = [jax.ShapeDtypeStruct((n, 3 * A_WIDTH), BF16), jax.ShapeDtypeStruct((n, A_WIDTH), F32),
                 jax.ShapeDtypeStruct((n, B_WIDTH), F32), jax.ShapeDtypeStruct((n, LANES), F32)]
    if emit_ctx:
        out_specs += [tok(KV_LORA), tok(QK_ROPE)]
        out_shape += [jax.ShapeDtypeStruct((n, KV_LORA), F32), jax.ShapeDtypeStruct((n, QK_ROPE), F32)]
    out_specs += [tok(MLA_W)] * 3
    out_shape += [jax.ShapeDtypeStruct((n, MLA_W), BF16)] * 3
    return pl.pallas_call(
        functools.partial(_inproj_kernel, tl=tl, tiles_per_seq=tps, rope=rope, emit_ctx=emit_ctx),
        grid=(nt,),
        in_specs=in_specs,
        out_specs=out_specs,
        out_shape=out_shape,
        compiler_params=_cp(("parallel",)),
        name="inproj",
    )(*args)


def _kvcache_kernel(ckv_ref, kr_ref, wk_ref, wv_ref, kh_out, vh_out):
    ckv_b = ckv_ref[...].astype(BF16)
    khat = _mm(ckv_b, wk_ref[...])
    vh_out[...] = _mm(ckv_b, wv_ref[...]).astype(BF16)
    kr_al = kr_ref[...]
    for hh in range(H_C):
        cs = slice(hh * HEAD_PAD, (hh + 1) * HEAD_PAD)
        kh_out[:, cs] = (khat[:, cs] + kr_al).astype(BF16)


def _kvcache(ckv, kr, lp, tl):
    n = ckv.shape[0]
    const2 = lambda i: (0, 0)
    kr = jnp.pad(kr, ((0, 0), (QK_NOPE, HEAD_PAD - QK_NOPE - QK_ROPE)))
    return pl.pallas_call(
        _kvcache_kernel,
        grid=(n // tl,),
        in_specs=[pl.BlockSpec((tl, KV_LORA), lambda i: (i, 0)), pl.BlockSpec((tl, HEAD_PAD), lambda i: (i, 0)),
                  pl.BlockSpec((KV_LORA, MLA_W), const2), pl.BlockSpec((KV_LORA, MLA_W), const2)],
        out_specs=[pl.BlockSpec((tl, MLA_W), lambda i: (i, 0))] * 2,
        out_shape=[jax.ShapeDtypeStruct((n, MLA_W), BF16)] * 2,
        compiler_params=_cp(("parallel",)),
        name="kvcache",
    )(ckv, kr, lp['w_k_pad'], lp['w_v_pad'])


ATT_SUB = 256
ATT_BATCHED_MAX = 512


def _attn_kernel(*refs, nseg, tq, hps, batched):
    q_ref = refs[0]
    k_refs = refs[1:1 + nseg]
    v_refs = refs[1 + nseg:1 + 2 * nseg]
    o_ref = refs[1 + 2 * nseg]
    def one_head(hh, rows):
        cs = slice(hh * HEAD_PAD, (hh + 1) * HEAD_PAD)
        q = q_ref[rows, cs]
        s = [lax.dot_general(q, k[:, cs], (((1,), (1,)), ((), ())), preferred_element_type=F32) for k in k_refs]
        m = s[0].max(axis=-1, keepdims=True)
        for t in s[1:]:
            m = jnp.maximum(m, t.max(axis=-1, keepdims=True))
        p = [jnp.exp2(t - m) for t in s]
        den = p[0].sum(axis=-1, keepdims=True)
        for t in p[1:]:
            den = den + t.sum(axis=-1, keepdims=True)
        acc = None
        for t, v in zip(p, v_refs):
            part = jnp.dot(t.astype(BF16), v[:, cs], preferred_element_type=F32)
            acc = part if acc is None else acc + part
        return acc / den

    def store_pair(pair, rows, o_even, o_odd):
        o = o_even + pltpu.roll(o_odd, V_HEAD, 1)
        o_ref[rows, pair * HEAD_PAD:(pair + 1) * HEAD_PAD] = o.astype(BF16)

    if batched:
        hsl = [slice(hh * HEAD_PAD, (hh + 1) * HEAD_PAD) for hh in range(hps)]
        q = jnp.stack([q_ref[:, cs] for cs in hsl])
        s = [_bmm_nt(q, jnp.stack([k[:, cs] for cs in hsl])) for k in k_refs]
        m = s[0].max(axis=-1, keepdims=True)
        for t in s[1:]:
            m = jnp.maximum(m, t.max(axis=-1, keepdims=True))
        p = [jnp.exp2(t - m) for t in s]
        den = p[0].sum(axis=-1, keepdims=True)
        for t in p[1:]:
            den = den + t.sum(axis=-1, keepdims=True)
        acc = None
        for t, v in zip(p, v_refs):
            part = _bmm(t, jnp.stack([v[:, cs] for cs in hsl]))
            acc = part if acc is None else acc + part
        o = acc / den
        for pair in range(hps // 2):
            store_pair(pair, slice(0, tq), o[2 * pair], o[2 * pair + 1])
        return

    for pair in range(hps // 2):
        for sb in range(tq // ATT_SUB):
            rows = slice(sb * ATT_SUB, (sb + 1) * ATT_SUB)
            store_pair(pair, rows, one_head(2 * pair, rows), one_head(2 * pair + 1, rows))


def _attention(qh, segs, bsz, seq_len, tq, hps):
    nq = seq_len // tq
    wblk = hps * HEAD_PAD
    in_specs = [pl.BlockSpec((tq, wblk), lambda b, h, i: (b * nq + i, h))]
    args = [qh]
    for which in (0, 1):
        for seg in segs:
            in_specs.append(pl.BlockSpec((seg[2], wblk), lambda b, h, i: (b, h)))
            args.append(seg[which])
    return pl.pallas_call(
        functools.partial(_attn_kernel, nseg=len(segs), tq=tq, hps=hps, batched=hps == H_C),
        grid=(bsz, H_C // hps, nq),
        in_specs=in_specs,
        out_specs=pl.BlockSpec((tq, hps * V_HEAD), lambda b, h, i: (b * nq + i, h)),
        out_shape=jax.ShapeDtypeStruct((bsz * seq_len, H_C * V_HEAD), BF16),
        compiler_params=_cp(("parallel", "parallel", "parallel")),
        name="attention",
    )(*args)


DELTA_TB = 256
DELTA_G = DELTA_TB // CHUNK


def _delta_prep_kernel(qkv_ref, bg_ref, uf_ref, wf_ref, qef_ref, kef_ref, qkf_ref,
                       ub_ref, wb_ref, qeb_ref, keb_ref, qkb_ref, dec_ref):
    c = CHUNK
    outs = ((uf_ref, wf_ref, qef_ref, kef_ref, qkf_ref), (ub_ref, wb_ref, qeb_ref, keb_ref, qkb_ref))
    ri = lax.broadcasted_iota(jnp.int32, (c, c), 0)
    ci = lax.broadcasted_iota(jnp.int32, (c, c), 1)
    tril = (ri >= ci).astype(BF16)
    triu = (ri <= ci).astype(BF16)
    same_blk = (ri // SUB) == (ci // SUB)
    scale = HEAD_DIM_A ** -0.5

    insts = [(g, d, hh) for g in range(DELTA_G) for d in range(2) for hh in range(H_A)]
    n = len(insts)
    qkv = qkv_ref[...]
    bg = bg_ref[...]

    def head_stack(base):
        return jnp.stack([qkv[g * c:(g + 1) * c, base + hh * HEAD_DIM_A:base + (hh + 1) * HEAD_DIM_A]
                          for g, d, hh in insts])

    q, k, v = head_stack(0), head_stack(A_WIDTH), head_stack(2 * A_WIDTH)

    per_slab = LANES // c
    bg_t = [bg[s * LANES:(s + 1) * LANES, :].T for s in range(DELTA_TB // LANES)]
    cols, rws = {}, {}
    for g in range(DELTA_G):
        bgc = bg[g * c:(g + 1) * c, :]
        bgr = bg_t[g // per_slab][0:16, (g % per_slab) * c:(g % per_slab + 1) * c]
        cols[g] = (bgc, _mm_exact_lhs(tril, bgc), _mm_exact_lhs(triu, bgc))
        rws[g] = (_mm_exact_rhs(bgr, triu), _mm_exact_rhs(bgr, tril))
    beta = jnp.stack([cols[g][0][:, d * H_A + hh:d * H_A + hh + 1] for g, d, hh in insts])
    gcc = jnp.stack([cols[g][1 + d][:, 2 * H_A + d * H_A + hh:2 * H_A + d * H_A + hh + 1]
                     for g, d, hh in insts])
    gcr = jnp.stack([rws[g][d][2 * H_A + d * H_A + hh:2 * H_A + d * H_A + hh + 1, :] for g, d, hh in insts])
    glast = jnp.stack([gcc[i, (c - 1 if insts[i][1] == 0 else 0):(c if insts[i][1] == 0 else 1), :]
                       for i in range(n)])

    sign = jnp.stack([jnp.full((1, 1), 1 - 2 * d, jnp.int32) for g, d, hh in insts])
    tri = (ri - ci)[None] * sign
    incl = tri >= 0
    strict = tri > 0
    blk = jnp.broadcast_to(same_blk[None], incl.shape)

    kf = k.astype(F32)
    kb = kf * beta
    decay = jnp.where(incl, jnp.exp(jnp.where(incl, gcc - gcr, 0.0)), 0.0)
    a = _bmm_nt(kb, k) * decay
    qk = _bmm_nt(q, k) * (decay * scale)
    dg = jnp.where(blk & strict, a, 0.0)
    lo = jnp.where(jnp.logical_not(blk) & strict, a, 0.0)
    m1 = _bmm(dg, dg)
    p = m1 - dg - _bmm(dg, m1)
    pw = m1
    for _ in range(int(math.log2(SUB)) - 2):
        pw = _bmm(pw, pw)
        p = p + pw + _bmm(p, pw)
    nm = lo + _bmm(p, lo)
    n2 = _bmm(nm, nm)
    t1 = n2 - nm - _bmm(nm, n2)
    tm = t1 + p + _bmm(t1, p)
    e_col = jnp.exp(gcc)
    rhs = jnp.concatenate([v.astype(F32) * beta, kb * e_col], axis=2)
    x = rhs + _bmm(tm, rhs)
    u, w = x[:, :, :HEAD_DIM_A], x[:, :, HEAD_DIM_A:]
    qe = q.astype(F32) * (e_col * scale)
    ke = kf * jnp.exp(glast - gcc)
    dec = jnp.exp(glast)
    zpad = jnp.zeros((c, HEAD_DIM_A - c), BF16)
    for i, (g, d, hh) in enumerate(insts):
        rows = slice(g * c, (g + 1) * c)
        cs = slice(hh * HEAD_DIM_A, (hh + 1) * HEAD_DIM_A)
        u_ref, w_ref, qe_ref, ke_ref, qk_ref = outs[d]
        u_ref[0, rows, cs] = u[i]
        w_ref[0, rows, cs] = w[i].astype(BF16)
        qe_ref[0, rows, cs] = qe[i].astype(BF16)
        ke_ref[0, rows, cs] = ke[i].astype(BF16)
        qk_ref[0, rows, cs] = jnp.concatenate([qk[i].astype(BF16), zpad], axis=1)
        dec_ref[0, g, d * H_A + hh:d * H_A + hh + 1, :] = jnp.broadcast_to(dec[i], (1, HEAD_DIM_A))


def _delta_prep(qkv, bg, bsz, seq_len):
    nblk = seq_len // DELTA_TB
    nc = seq_len // CHUNK
    tokf = jax.ShapeDtypeStruct((bsz, seq_len, A_WIDTH), F32)
    tokb = jax.ShapeDtypeStruct((bsz, seq_len, A_WIDTH), BF16)
    tspec = pl.BlockSpec((1, DELTA_TB, A_WIDTH), lambda b, j: (b, j, 0))
    return pl.pallas_call(
        _delta_prep_kernel,
        grid=(bsz, nblk),
        in_specs=[pl.BlockSpec((DELTA_TB, 3 * A_WIDTH), lambda b, j: (b * nblk + j, 0)),
                  pl.BlockSpec((DELTA_TB, LANES), lambda b, j: (b * nblk + j, 0))],
        out_specs=[tspec] * 10 + [pl.BlockSpec((1, DELTA_G, 2 * H_A, HEAD_DIM_A), lambda b, j: (b, j, 0, 0))],
        out_shape=[tokf, tokb, tokb, tokb, tokb] * 2 + [jax.ShapeDtypeStruct((bsz, nc, 2 * H_A, HEAD_DIM_A), F32)],
        compiler_params=_cp(("parallel", "parallel")),
        name="delta_prep",
    )(qkv, bg)


def _delta_recur_kernel(*refs, nb, nblk, has_s0, emit_state):
    it = iter(refs)
    fwd = [next(it) for _ in range(5)]
    bwd = [next(it) for _ in range(5)]
    decf_ref, decb_ref = next(it), next(it)
    s0_ref = next(it) if has_s0 else None
    of_ref, ob_ref = next(it), next(it)
    sfin_ref = next(it) if emit_state else None
    s_scr = next(it)
    j = pl.program_id(1)
    c = CHUNK

    @pl.when(j == 0)
    def _():
        if has_s0:
            s_scr[...] = s0_ref[...]
        else:
            s_scr[...] = jnp.zeros(s_scr.shape, F32)

    insts = [(bi, d, hh) for bi in range(nb) for d in range(2) for hh in range(H_A)]
    for step in range(DELTA_G):
        gsel = (step, DELTA_G - 1 - step)

        def stk(idx, width=HEAD_DIM_A):
            return jnp.stack([(fwd, bwd)[d][idx][bi, gsel[d] * c:(gsel[d] + 1) * c,
                                                 hh * HEAD_DIM_A:hh * HEAD_DIM_A + width]
                              for bi, d, hh in insts])

        u, w, qe, ke, qk = stk(0), stk(1), stk(2), stk(3), stk(4, c)
        dec = jnp.stack([(decf_ref, decb_ref)[d][bi, gsel[d], d * H_A + hh:d * H_A + hh + 1, :]
                         for bi, d, hh in insts])
        s_old = jnp.stack([s_scr[bi, d, hh] for bi, d, hh in insts])
        r1 = _bmm(jnp.concatenate([w, qe], axis=1), s_old)
        v_new = (u - r1[:, :c, :]).astype(BF16)
        o = r1[:, c:, :] + _bmm(qk, v_new)
        s_new = s_old * dec + _bmm_tn(ke, v_new)
        for i, (bi, d, hh) in enumerate(insts):
            s_scr[bi, d, hh] = s_new[i]
            (of_ref, ob_ref)[d][bi, gsel[d] * c:(gsel[d] + 1) * c, hh * HEAD_DIM_A:(hh + 1) * HEAD_DIM_A] = o[i]

    if emit_state:
        @pl.when(j == nblk - 1)
        def _():
            sfin_ref[...] = s_scr[...]


def _delta_recur(prep, s0, bsz, seq_len, nb, emit_state):
    nblk = seq_len // DELTA_TB
    has_s0 = s0 is not None
    fspec = pl.BlockSpec((nb, DELTA_TB, A_WIDTH), lambda b, j: (b, j, 0))
    bspec = pl.BlockSpec((nb, DELTA_TB, A_WIDTH), lambda b, j: (b, nblk - 1 - j, 0))
    dspec_f = pl.BlockSpec((nb, DELTA_G, 2 * H_A, HEAD_DIM_A), lambda b, j: (b, j, 0, 0))
    dspec_b = pl.BlockSpec((nb, DELTA_G, 2 * H_A, HEAD_DIM_A), lambda b, j: (b, nblk - 1 - j, 0, 0))
    st_spec = pl.BlockSpec((nb, 2, H_A, HEAD_DIM_A, HEAD_DIM_A), lambda b, j: (b, 0, 0, 0, 0))
    in_specs = [fspec] * 5 + [bspec] * 5 + [dspec_f, dspec_b]
    args = list(prep[:10]) + [prep[10], prep[10]]
    if has_s0:
        in_specs.append(st_spec)
        args.append(s0)
    out_specs = [fspec, bspec]
    out_shape = [jax.ShapeDtypeStruct((bsz, seq_len, A_WIDTH), F32)] * 2
    if emit_state:
        out_specs.append(st_spec)
        out_shape.append(jax.ShapeDtypeStruct((bsz, 2, H_A, HEAD_DIM_A, HEAD_DIM_A), F32))
    return pl.pallas_call(
        functools.partial(_delta_recur_kernel, nb=nb, nblk=nblk, has_s0=has_s0, emit_state=emit_state),
        grid=(bsz // nb, nblk),
        in_specs=in_specs,
        out_specs=out_specs,
        out_shape=out_shape,
        scratch_shapes=[pltpu.VMEM((nb, 2, H_A, HEAD_DIM_A, HEAD_DIM_A), F32)],
        compiler_params=_cp(("parallel", "arbitrary")),
        name="delta_recur",
    )(*args)


def _s5prep_kernel(lre_ref, lim_ref, ldt_ref, bre_ref, bim_ref, cre_ref, cim_ref,
                   m_out, ef_out, eb_out, ff_out, fb_out, lt_out, xcat_scr, qcat_scr, ft_scr):
    gs, p, cg, t8, half = S5_HG, S5_STATE, S5_GROUP, S5_T, S5_K // 2
    j = (lax.broadcasted_iota(jnp.int32, (S5_ROWS, 1), 0) // cg).astype(F32)
    tile = lambda m: jnp.concatenate([m] * t8, axis=0)
    one = jnp.ones((1, 1), F32)
    bd = {}
    for d in range(2):
        e_out = (ef_out, eb_out)[d]
        f_out = (ff_out, fb_out)[d]
        xcat_scr[...] = jnp.zeros(xcat_scr.shape, F32)
        qcat_scr[...] = jnp.zeros(qcat_scr.shape, F32)
        ft_scr[...] = jnp.zeros(ft_scr.shape, F32)
        e_out[0] = jnp.zeros((S5_K, S5_K), BF16)
        for gl in range(gs):
            lre, lim = lre_ref[0, d, 0, gl:gl + 1, :], lim_ref[0, d, 0, gl:gl + 1, :]
            dt = jnp.exp(ldt_ref[0, d, 0, gl:gl + 1, :])
            zr, zi = lre * dt, lim * dt

            def lam_pow(e):
                mag = jnp.exp(e * zr)
                return mag * jnp.cos(e * zi), mag * jnp.sin(e * zi)

            l1r, l1i = lam_pow(one)
            den = lre * lre + lim * lim
            nr, ni = l1r - 1.0, l1i
            cfr = (nr * lre + ni * lim) / den
            cfi = (ni * lre - nr * lim) / den
            bre, bim = bre_ref[0, gl], bim_ref[0, gl]
            bbr = cfr * bre - cfi * bim
            bbi = cfr * bim + cfi * bre
            cre, cim = cre_ref[0, gl], cim_ref[0, gl]
            c_r, c_i = tile(cre), tile(cim)
            b_r, b_i = tile(bbr), tile(bbi)
            grow = slice(gl * cg, (gl + 1) * cg)
            lre_c, lim_c = slice(gl * p, (gl + 1) * p), slice(half + gl * p, half + (gl + 1) * p)

            xcat_scr[grow, 2 * gl * p:(2 * gl + 1) * p] = bbr
            xcat_scr[grow, (2 * gl + 1) * p:(2 * gl + 2) * p] = bbi
            pr, pi = lam_pow(j)
            qr = pr * c_r - pi * c_i
            qi = pr * c_i + pi * c_r
            for jj in range(t8):
                qcat_scr[jj, grow, 2 * gl * p:(2 * gl + 1) * p] = qr[jj * cg:(jj + 1) * cg, :]
                qcat_scr[jj, grow, (2 * gl + 1) * p:(2 * gl + 2) * p] = -qi[jj * cg:(jj + 1) * cg, :]

            er, ei = lam_pow(j if d else (t8 - 1.0) - j)
            e_r = (er * b_r - ei * b_i).astype(BF16)
            e_i = (er * b_i + ei * b_r).astype(BF16)
            fr, fi = lam_pow((t8 - j) if d else j + 1.0)
            f_r = fr * c_r - fi * c_i
            f_i = -(fr * c_i + fi * c_r)
            for s in range(t8):
                rows = slice(s * S5_TOKW + gl * cg, s * S5_TOKW + (gl + 1) * cg)
                e_out[0, rows, lre_c] = e_r[s * cg:(s + 1) * cg, :]
                e_out[0, rows, lim_c] = e_i[s * cg:(s + 1) * cg, :]
                ft_scr[rows, lre_c] = f_r[s * cg:(s + 1) * cg, :]
                ft_scr[rows, lim_c] = f_i[s * cg:(s + 1) * cg, :]

            ltr, lti = lam_pow(one * float(t8))
            lt_out[0, d:d + 1, lre_c] = ltr
            lt_out[0, d:d + 1, lim_c] = lti
        f_out[0] = ft_scr[...].T.astype(BF16)
        xcat = xcat_scr[...]
        for jj in range(t8):
            bd[(d, jj)] = _mm3_nt(xcat, qcat_scr[jj])
    for s in range(t8):
        for t in range(t8):
            blk = bd[(0, t - s)] if t > s else (bd[(1, s - t)] if s > t else bd[(0, 0)] + bd[(1, 0)])
            m_out[0, s * S5_TOKW:(s + 1) * S5_TOKW, t * S5_TOKW:(t + 1) * S5_TOKW] = blk.astype(BF16)


def _s5prep(lam_re, lam_im, log_dt, b_re, b_im, c_re, c_im):
    gs, p, cg = S5_HG, S5_STATE, S5_GROUP
    per_layer = S5_SG * S5_HALVES
    nq = DEPTH * per_layer
    lam_idx = lambda q: (q // per_layer, 0, q % per_layer, 0, 0)
    par_idx = lambda q: (q, 0, 0, 0)
    lam5 = lambda v, w: v.reshape(DEPTH, 2, per_layer, gs, w)
    b_t = lambda v: v.reshape(nq, gs, p, cg).transpose(0, 1, 3, 2)
    par4 = lambda v: v.reshape(nq, gs, cg, p)
    mat = pl.BlockSpec((1, S5_K, S5_K), lambda q: (q, 0, 0))
    return pl.pallas_call(
        _s5prep_kernel,
        grid=(nq,),
        in_specs=[pl.BlockSpec((1, 2, 1, gs, p), lam_idx), pl.BlockSpec((1, 2, 1, gs, p), lam_idx),
                  pl.BlockSpec((1, 2, 1, gs, 1), lam_idx)] + [pl.BlockSpec((1, gs, cg, p), par_idx)] * 4,
        out_specs=[mat] * 5 + [pl.BlockSpec((1, 2, S5_K), lambda q: (q, 0, 0))],
        out_shape=[jax.ShapeDtypeStruct((nq, S5_K, S5_K), BF16)] * 5 + [jax.ShapeDtypeStruct((nq, 2, S5_K), F32)],
        scratch_shapes=[pltpu.VMEM((S5_TOKW, S5_K), F32), pltpu.VMEM((S5_T, S5_TOKW, S5_K), F32),
                        pltpu.VMEM((S5_K, S5_K), F32)],
        compiler_params=_cp(("parallel",)),
        name="s5prep",
    )(lam5(lam_re, p), lam5(lam_im, p), lam5(log_dt, 1), b_t(b_re), b_t(b_im), par4(c_re), par4(c_im))


S5_RT = 256


def _s5_kernel(*refs, nchunk, nb, seq_len, has_h0, emit_state):
    it = iter(refs)
    u_ref = next(it)
    m_ref, ef_ref, eb_ref, ff_ref, fb_ref, lt_ref = (next(it) for _ in range(6))
    h0_ref = next(it) if has_h0 else None
    y_ref = next(it)
    hfin_ref = next(it) if emit_state else None
    up_scr, sf_scr, sb_scr = next(it), next(it), next(it)

    r = nchunk * nb
    rtile = min(S5_RT, r)
    half = S5_K // 2
    for t in range(S5_T):
        for b in range(nb):
            up_scr[t, pl.ds(b, nchunk, stride=nb), :] = u_ref[pl.ds(b * seq_len + t, nchunk, stride=S5_T), :]

    def low_lanes(shape):
        return lax.broadcasted_iota(jnp.int32, shape, 1) < S5_TOKW

    def u_tiles(rt):
        rows = slice(rt * rtile, (rt + 1) * rtile)
        low = low_lanes((rtile, LANES))
        halves = [[], []]
        for jt in range(S5_T // 2):
            pa, pb = up_scr[2 * jt, rows, :], up_scr[2 * jt + 1, rows, :]
            halves[0].append(jnp.where(low, pa, pltpu.roll(pb, S5_TOKW, 1)))
            halves[1].append(jnp.where(low, pltpu.roll(pa, S5_TOKW, 1), pb))
        return [jnp.concatenate(hv, axis=1).astype(BF16) for hv in halves]

    for rt in range(r // rtile):
        rows = slice(rt * rtile, (rt + 1) * rtile)
        for hv, ub in enumerate(u_tiles(rt)):
            cs = slice(hv * S5_K, (hv + 1) * S5_K)
            sf_scr[rows, cs] = jnp.dot(ub, ef_ref[0, hv], preferred_element_type=F32)
            sb_scr[rows, cs] = jnp.dot(ub, eb_ref[0, hv], preferred_element_type=F32)

    def swap_re_im(x):
        parts = []
        for hv in range(S5_HALVES):
            parts += [x[:, hv * S5_K + half:(hv + 1) * S5_K], x[:, hv * S5_K:hv * S5_K + half]]
        return jnp.concatenate(parts, axis=1)

    def coeffs(d):
        lt = jnp.concatenate([lt_ref[0, hv, d:d + 1, :] for hv in range(S5_HALVES)], axis=1)
        sw = swap_re_im(lt)
        is_re = (lax.broadcasted_iota(jnp.int32, (1, S5_W), 1) % S5_K) < half
        return jnp.where(is_re, lt, sw), jnp.where(is_re, -sw, lt)

    af, bf = coeffs(0)
    ab, bb = coeffs(1)
    if has_h0:
        h0f, h0b = h0_ref[0, 0, 0], h0_ref[0, 1, 0]
    else:
        h0f = h0b = jnp.zeros((nb, S5_W), F32)

    tile_rows = max(nb, 8)
    cps = tile_rows // nb
    n_iter = nchunk // cps

    def step(h, a, b, seg):
        return a * h + b * swap_re_im(h) + seg

    def body(i, carry):
        hf, hb = carry
        rf = pl.ds(pl.multiple_of(i * tile_rows, tile_rows), tile_rows)
        rb = pl.ds(pl.multiple_of((n_iter - 1 - i) * tile_rows, tile_rows), tile_rows)
        xf, xb = sf_scr[rf, :], sb_scr[rb, :]
        of, ob = [], [None] * cps
        for jj in range(cps):
            of.append(hf)
            hf = step(hf, af, bf, xf[jj * nb:(jj + 1) * nb, :])
        for jj in reversed(range(cps)):
            ob[jj] = hb
            hb = step(hb, ab, bb, xb[jj * nb:(jj + 1) * nb, :])
        sf_scr[rf, :] = of[0] if cps == 1 else jnp.concatenate(of, axis=0)
        sb_scr[rb, :] = ob[0] if cps == 1 else jnp.concatenate(ob, axis=0)
        return hf, hb

    hf, hb = lax.fori_loop(0, n_iter, body, (h0f, h0b))
    if emit_state:
        hfin_ref[0, 0, 0] = hf
        hfin_ref[0, 1, 0] = hb

    for rt in range(r // rtile):
        rows = slice(rt * rtile, (rt + 1) * rtile)
        ys = []
        for hv, ub in enumerate(u_tiles(rt)):
            cs = slice(hv * S5_K, (hv + 1) * S5_K)
            y = jnp.dot(ub, m_ref[0, hv], preferred_element_type=F32)
            y = y + jnp.dot(sf_scr[rows, cs].astype(BF16), ff_ref[0, hv], preferred_element_type=F32)
            ys.append(y + jnp.dot(sb_scr[rows, cs].astype(BF16), fb_ref[0, hv], preferred_element_type=F32))
        low = low_lanes((rtile, LANES))
        for jt in range(S5_T // 2):
            y0, y1 = ys[0][:, jt * LANES:(jt + 1) * LANES], ys[1][:, jt * LANES:(jt + 1) * LANES]
            up_scr[2 * jt, rows, :] = jnp.where(low, y0, pltpu.roll(y1, S5_TOKW, 1))
            up_scr[2 * jt + 1, rows, :] = jnp.where(low, pltpu.roll(y0, S5_TOKW, 1), y1)
    for t in range(S5_T):
        for b in range(nb):
            y_ref[pl.ds(b * seq_len + t, nchunk, stride=S5_T), :] = up_scr[t, pl.ds(b, nchunk, stride=nb), :]


def _s5(u, mats_l, h0, bsz, seq_len, nb, emit_state):
    nchunk = seq_len // S5_T
    r = nchunk * nb
    has_h0 = h0 is not None
    qi = lambda q, j: (q, 0, 0, 0)
    wspec = pl.BlockSpec((1, S5_HALVES, S5_K, S5_K), qi)
    in_specs = [pl.BlockSpec((nb * seq_len, LANES), lambda q, j: (j, q))] + [wspec] * 5 \
        + [pl.BlockSpec((1, S5_HALVES, 2, S5_K), qi)]
    args = [u] + list(mats_l)
    st_spec = pl.BlockSpec((1, 2, 1, nb, S5_W), lambda q, j: (q, 0, j, 0, 0))
    st_shape = (S5_SG, 2, bsz // nb, nb, S5_W)
    if has_h0:
        in_specs.append(st_spec)
        args.append(h0.reshape(st_shape))
    out_specs = [pl.BlockSpec((nb * seq_len, LANES), lambda q, j: (j, q))]
    out_shape = [jax.ShapeDtypeStruct((bsz * seq_len, B_WIDTH), F32)]
    if emit_state:
        out_specs.append(st_spec)
        out_shape.append(jax.ShapeDtypeStruct(st_shape, F32))
    outs = pl.pallas_call(
        functools.partial(_s5_kernel, nchunk=nchunk, nb=nb, seq_len=seq_len, has_h0=has_h0, emit_state=emit_state),
        grid=(S5_SG, bsz // nb),
        in_specs=in_specs,
        out_specs=out_specs,
        out_shape=out_shape,
        scratch_shapes=[pltpu.VMEM((S5_T, r, LANES), F32), pltpu.VMEM((r, S5_W), F32), pltpu.VMEM((r, S5_W), F32)],
        compiler_params=_cp(("parallel", "parallel")),
        name="s5",
    )(*args)
    if emit_state:
        return outs[0], outs[1].reshape(S5_SG, 2, bsz, S5_W)
    return (outs[0],)


def _merge_kernel(x_ref, shift_ref, scale_ref, gate_ref, gnorm_ref, of_ref, ob_ref, zs_ref, gout_ref, ys_ref, u_ref,
                  oc_ref, wg_ref, wba_ref, wbb_ref, wbc_ref, wout_ref, wglu_ref, bglu_ref, dskip_ref, xo_ref):
    x = x_ref[...]
    h = ((_rms(x) * gnorm_ref[...]) * (1.0 + scale_ref[0]) + shift_ref[0]).astype(BF16)
    od = of_ref[...] + ob_ref[...]
    gout = gout_ref[...]
    oa = jnp.concatenate([_rms(od[:, hh * HEAD_DIM_A:(hh + 1) * HEAD_DIM_A]) * gout for hh in range(H_A)], axis=1)
    oa = oa * zs_ref[...]
    yb = jax.nn.gelu(ys_ref[...] + dskip_ref[...] * u_ref[...])
    ob = yb * jax.nn.sigmoid(_mm(yb, wglu_ref[...]) + bglu_ref[...])
    acc = None
    for nbr, (o, w) in enumerate(((oa, wba_ref), (ob, wbb_ref), (oc_ref[...], wbc_ref))):
        gate = jax.nn.sigmoid(jnp.dot(h, wg_ref[:, nbr * D_MODEL:(nbr + 1) * D_MODEL], preferred_element_type=F32))
        term = gate * _mm(o, w[...])
        acc = term if acc is None else acc + term
    out = _mm(acc, wout_ref[...])
    xo_ref[...] = x + gate_ref[0] * out


def _merge(x, mods_rows, row_of_tile, lp, o_f, o_b, zs, ys, u, oc, tl):
    n = x.shape[0]
    const2 = lambda i: (0, 0)
    tok = lambda w: pl.BlockSpec((tl, w), lambda i: (i, 0))
    mod = lambda j: pl.BlockSpec((1, 1, D_MODEL), lambda i: (row_of_tile(i) * 6 + j, 0, 0))
    return pl.pallas_call(
        _merge_kernel,
        grid=(n // tl,),
        in_specs=[tok(D_MODEL), mod(0), mod(1), mod(2), pl.BlockSpec((1, D_MODEL), const2),
                  tok(A_WIDTH), tok(A_WIDTH), tok(A_WIDTH), pl.BlockSpec((1, HEAD_DIM_A), const2),
                  tok(B_WIDTH), tok(B_WIDTH), tok(BRANCH_WIDTH),
                  pl.BlockSpec((D_MODEL, N_BRANCH * D_MODEL), const2),
                  pl.BlockSpec((BRANCH_WIDTH, D_MODEL), const2), pl.BlockSpec((BRANCH_WIDTH, D_MODEL), const2),
                  pl.BlockSpec((BRANCH_WIDTH, D_MODEL), const2), pl.BlockSpec((D_MODEL, D_MODEL), const2),
                  pl.BlockSpec((B_WIDTH, B_WIDTH), const2), pl.BlockSpec((1, B_WIDTH), const2),
                  pl.BlockSpec((1, B_WIDTH), const2)],
        out_specs=tok(D_MODEL),
        out_shape=jax.ShapeDtypeStruct((n, D_MODEL), F32),
        compiler_params=_cp(("parallel",)),
        name="merge",
    )(x, mods_rows, mods_rows, mods_rows, lp['g_norm_mix'], o_f, o_b, zs, lp['g_delta_out'], ys, u, oc,
      lp['w_gates'], lp['w_br_a'], lp['w_br_b'],
      lp['w_br_c'], lp['w_out'], lp['w_glu'], lp['b_glu'], lp['s5_d'])


FF_BLK = 256


def _ffn_kernel(*refs, tl, tiles_per_seq, final):
    it = iter(refs)
    xp_ref, xc_ref, xn_ref = next(it), next(it), next(it)
    shift_ref, scale_ref, gate_ref, gnorm_ref = next(it), next(it), next(it), next(it)
    wup_ref, convw_ref, convb_ref, wdown_ref = next(it), next(it), next(it), next(it)
    gfin_ref = next(it) if final else None
    xo_ref = next(it)
    yo_ref = next(it) if final else None
    act_scr = next(it)

    i = pl.program_id(0)
    pos = i % tiles_per_seq
    mod_scale = 1.0 + scale_ref[0]
    mod_shift = shift_ref[0]
    gain = gnorm_ref[...]

    def norm_mod(x):
        return ((_rms(x) * gain) * mod_scale + mod_shift).astype(BF16)

    x = xc_ref[...]
    h_prev = norm_mod(xp_ref[...]) * (pos > 0).astype(BF16)
    h_next = norm_mod(xn_ref[...]) * (pos < tiles_per_seq - 1).astype(BF16)
    h_ext = jnp.concatenate([h_prev, norm_mod(x), h_next], axis=0)
    n_ext = tl + 2 * HALO
    pad = FFN_CONV // 2

    def conv_act(cols):
        up = jnp.dot(h_ext, wup_ref[:, cols], preferred_element_type=F32)
        acc = None
        for t in range(FFN_CONV):
            sh = (pad - t) % n_ext
            src = up if sh == 0 else pltpu.roll(up, sh, 0)
            term = src[HALO:HALO + tl, :] * convw_ref[t:t + 1, cols]
            acc = term if acc is None else acc + term
        return acc + convb_ref[:, cols]

    for j in range(D_FF // FF_BLK):
        gcols = slice(j * FF_BLK, (j + 1) * FF_BLK)
        vcols = slice(D_FF + j * FF_BLK, D_FF + (j + 1) * FF_BLK)
        act_scr[:, gcols] = (_silu(conv_act(gcols)) * conv_act(vcols)).astype(BF16)
    out = jnp.dot(act_scr[...], wdown_ref[...], preferred_element_type=F32)
    xo = x + gate_ref[0] * out
    xo_ref[...] = xo
    if final:
        yo_ref[...] = _rms(xo) * gfin_ref[...]


def _ffn(x, mods_rows, row_of_tile, lp, seq_len, tl, g_final):
    n = x.shape[0]
    tps = seq_len // tl
    final = g_final is not None
    xprev, xnext = _halo_maps(n, tl)
    const2 = lambda i: (0, 0)
    mod = lambda j: pl.BlockSpec((1, 1, D_MODEL), lambda i: (row_of_tile(i) * 6 + j, 0, 0))
    tok = pl.BlockSpec((tl, D_MODEL), lambda i: (i, 0))
    in_specs = [pl.BlockSpec((HALO, D_MODEL), xprev), tok, pl.BlockSpec((HALO, D_MODEL), xnext),
                mod(3), mod(4), mod(5), pl.BlockSpec((1, D_MODEL), const2),
                pl.BlockSpec((D_MODEL, 2 * D_FF), const2), pl.BlockSpec((FFN_CONV, 2 * D_FF), const2),
                pl.BlockSpec((1, 2 * D_FF), const2), pl.BlockSpec((D_FF, D_MODEL), const2)]
    args = [x, x, x, mods_rows, mods_rows, mods_rows, lp['g_norm_ffn'], lp['w_ffn_up'], lp['conv_ffn'],
            lp['b_conv_ffn'], lp['w_ffn_down']]
    out_specs = [tok]
    out_shape = [jax.ShapeDtypeStruct((n, D_MODEL), F32)]
    if final:
        in_specs.append(pl.BlockSpec((1, D_MODEL), const2))
        args.append(g_final)
        out_specs.append(tok)
        out_shape.append(jax.ShapeDtypeStruct((n, D_MODEL), F32))
    return pl.pallas_call(
        functools.partial(_ffn_kernel, tl=tl, tiles_per_seq=tps, final=final),
        grid=(n // tl,),
        in_specs=in_specs,
        out_specs=out_specs,
        out_shape=out_shape,
        scratch_shapes=[pltpu.VMEM((tl, D_FF), BF16)],
        compiler_params=_cp(("parallel",)),
        name="ffn",
    )(*args)


def _pad_heads(w, head_w, n_heads):
    k = w.shape[0]
    w = w.reshape(k, n_heads, head_w)
    w = jnp.pad(w, ((0, 0), (0, 0), (0, HEAD_PAD - head_w)))
    return w.reshape(k, n_heads * HEAD_PAD)


def _layer_params(l, p):
    w_in = p['w_in'][l]
    o = 0
    parts = {}
    for name, wd in (('qkv', 3 * A_WIDTH), ('z', A_WIDTH), ('beta', 2 * H_A), ('alpha', 2 * H_A), ('u', B_WIDTH),
                     ('qa', Q_LORA), ('kva', KV_LORA), ('kr', QK_ROPE), ('gates', N_BRANCH * D_MODEL)):
        parts[name] = w_in[:, o:o + wd]
        o += wd
    small = jnp.concatenate([parts['beta'], parts['alpha'], parts['kr'],
                             jnp.zeros((D_MODEL, LANES - 4 * H_A - QK_ROPE), F32)], axis=1)
    w_rest = jnp.concatenate([parts['z'], parts['u'], parts['qa'], parts['kva'], small], axis=1)
    pad8 = lambda v: jnp.pad(v.reshape(1, 2 * H_A), ((0, 0), (2 * H_A, LANES - 4 * H_A)))
    w_kv = p['w_kv_b'][l].reshape(KV_LORA, H_C, QK_NOPE + V_HEAD)
    w_k = w_kv[:, :, :QK_NOPE].reshape(KV_LORA, H_C * QK_NOPE)
    w_v = w_kv[:, :, QK_NOPE:].reshape(KV_LORA, H_C * V_HEAD)
    row = lambda v: v.reshape(1, -1)
    return {
        'g_norm_mix': row(p['g_norm_mix'][l]), 'g_norm_ffn': row(p['g_norm_ffn'][l]),
        'w_qkv': parts['qkv'].astype(BF16), 'w_rest': w_rest.astype(BF16), 'w_gates': parts['gates'].astype(BF16),
        'conv_qkv': p['conv_qkv'][l],
        'a_log128': pad8(p['a_log'][l]), 'dt_bias128': pad8(p['dt_bias'][l]),
        'g_delta_out': row(p['g_delta_out'][l]),
        'g_q_a': row(p['g_q_a'][l]), 'g_kv_a': row(p['g_kv_a'][l]),
        'w_q_pad': _pad_heads(p['w_q_b'][l], QK_NOPE + QK_ROPE, H_C).astype(BF16),
        'w_k_pad': _pad_heads(w_k, QK_NOPE, H_C).astype(BF16),
        'w_v_pad': _pad_heads(w_v, V_HEAD, H_C).astype(BF16),
        'w_br_a': p['w_branch'][l, 0].astype(BF16), 'w_br_b': p['w_branch'][l, 1].astype(BF16),
        'w_br_c': p['w_branch'][l, 2].astype(BF16),
        'w_out': p['w_out'][l].astype(BF16),
        'w_glu': p['w_glu'][l].astype(BF16), 'b_glu': row(p['b_glu'][l]), 's5_d': row(p['s5_d'][l]),
        'w_ffn_up': p['w_ffn_up'][l].astype(BF16), 'conv_ffn': p['conv_ffn'][l],
        'b_conv_ffn': row(p['b_conv_ffn'][l]), 'w_ffn_down': p['w_ffn_down'][l].astype(BF16),
    }


def _rope_tables(length):
    rows = length // GRID_W
    row = jnp.repeat(jnp.arange(rows, dtype=F32), GRID_W)
    col = (jnp.arange(length) % GRID_W).astype(F32)
    n_freq = QK_ROPE // 4
    inv_freq = 1.0 / (ROPE_BASE ** (jnp.arange(n_freq, dtype=F32) / n_freq))
    ang = jnp.concatenate([row[:, None] * inv_freq, col[:, None] * inv_freq], axis=-1)
    cos, sin = jnp.cos(ang), jnp.sin(ang)
    ones = jnp.ones((length, QK_NOPE), F32)
    zeros = jnp.zeros((length, QK_NOPE), F32)
    tail = jnp.zeros((length, HEAD_PAD - QK_NOPE - QK_ROPE), F32)
    cos_t = jnp.concatenate([ones, cos, cos, tail], axis=1)
    sin_t = jnp.concatenate([zeros, -sin, sin, tail], axis=1)
    return cos_t, sin_t


def _layer(x, bsz, seq_len, mods_rows, row_of_tile, lp, s5mats, ctx, rope_tabs, tl, g_final):
    is_ctx = ctx is None
    outs = _inproj(x, mods_rows, row_of_tile, lp, seq_len, tl, rope_tabs, emit_ctx=is_ctx)
    if is_ctx:
        qkv, zs, u, bg, ckv, kr, qh, kh, vh = outs
    else:
        qkv, zs, u, bg, qh, kh, vh = outs

    prep = _delta_prep(qkv, bg, bsz, seq_len)
    d_out = _delta_recur(prep, None if is_ctx else ctx['s0'], bsz, seq_len, min(bsz, DELTA_NB), emit_state=is_ctx)
    o_f = d_out[0].reshape(bsz * seq_len, A_WIDTH)
    o_b = d_out[1].reshape(bsz * seq_len, A_WIDTH)

    s5_nb = max(1, min(bsz, S5_MAX_NB, (S5_STEP_ROWS * S5_T) // seq_len))
    s_out = _s5(u, s5mats, None if is_ctx else ctx['h0'], bsz, seq_len, s5_nb, emit_state=is_ctx)
    ys = s_out[0]

    segs = [(kh, vh, seq_len)]
    if not is_ctx:
        segs = [(ctx['kh'], ctx['vh'], ctx['past'])] + segs
    oc = _attention(qh, segs, bsz, seq_len, min(seq_len, ATT_TQ),
                    H_C if seq_len <= ATT_BATCHED_MAX else ATT_HEADS_LONG)

    x = _merge(x, mods_rows, row_of_tile, lp, o_f, o_b, zs, ys, u, oc, tl)
    f_out = _ffn(x, mods_rows, row_of_tile, lp, seq_len, tl, g_final)
    extras = None
    if is_ctx:
        hfin = s_out[1]
        hfin = hfin.reshape(S5_SG, 2, bsz, S5_HALVES, 2, S5_HG, S5_STATE).transpose(4, 2, 1, 0, 3, 5, 6)
        hfin = hfin.reshape(2, bsz, 2, S5_GROUPS, S5_STATE)
        extras = (d_out[2], hfin[0], hfin[1], ckv.reshape(bsz, seq_len, KV_LORA), kr.reshape(bsz, seq_len, QK_ROPE))
    return f_out, extras


def kernel(x_prompt, x_sample, state_delta, state_s5_re, state_s5_im, cache_ckv, cache_krope, c, c_ctx, w_mod, b_mod, g_norm_mix, g_norm_ffn, w_in, conv_qkv, a_log, dt_bias, g_delta_out, s5_lam_re, s5_lam_im, s5_log_dt, s5_b_re, s5_b_im, s5_c_re, s5_c_im, s5_d, w_glu, b_glu, g_q_a, w_q_b, g_kv_a, w_kv_b, w_branch, w_out, w_ffn_up, conv_ffn, b_conv_ffn, w_ffn_down, g_final):
    p = dict(g_norm_mix=g_norm_mix, g_norm_ffn=g_norm_ffn, w_in=w_in, conv_qkv=conv_qkv, a_log=a_log,
             dt_bias=dt_bias, g_delta_out=g_delta_out, s5_d=s5_d, w_glu=w_glu, b_glu=b_glu, g_q_a=g_q_a,
             w_q_b=w_q_b, g_kv_a=g_kv_a, w_kv_b=w_kv_b, w_branch=w_branch, w_out=w_out, w_ffn_up=w_ffn_up,
             conv_ffn=conv_ffn, b_conv_ffn=b_conv_ffn, w_ffn_down=w_ffn_down)
    bp, lp_len, _ = x_prompt.shape
    bs, ls_len, _ = x_sample.shape
    past = cache_ckv.shape[2]
    depth = w_in.shape[0]

    mod_rows = MOD_ROWS
    cvec = jnp.concatenate([c_ctx[None, :], c, jnp.zeros((mod_rows - 1 - bs, D_MODEL), F32)], axis=0).astype(F32)
    mods = _modulation(cvec, w_mod, b_mod)

    s5_all = _s5prep(s5_lam_re, s5_lam_im, s5_log_dt, s5_b_re, s5_b_im, s5_c_re, s5_c_im)

    rope_tabs = _rope_tables(ls_len)
    tl_p = min(lp_len, TOKEN_TILE)
    tl_s = min(ls_len, TOKEN_TILE)
    tiles_per_seq_s = ls_len // tl_s

    def sg_state(re, im):
        t = jnp.stack([re, im], axis=0).astype(F32).reshape(2, bs, 2, S5_SG, S5_HALVES, S5_HG, S5_STATE)
        return t.transpose(3, 2, 1, 4, 0, 5, 6).reshape(S5_SG, 2, bs, S5_W)

    xp = x_prompt.astype(F32).reshape(bp * lp_len, D_MODEL)
    xs = x_sample.astype(F32).reshape(bs * ls_len, D_MODEL)
    deltas, s5_res, s5_ims, ckvs, kropes = [], [], [], [], []
    yp = ys = None
    for l in range(depth):
        lp = _layer_params(l, p)
        mods_rows = mods[l].reshape(mod_rows * 6, 1, D_MODEL)
        per_layer = S5_SG * S5_HALVES
        s5mats = tuple(m[l * per_layer:(l + 1) * per_layer].reshape((S5_SG, S5_HALVES) + m.shape[1:]) for m in s5_all)
        last = l == depth - 1
        gfin = g_final.reshape(1, D_MODEL) if last else None

        out_p, extras = _layer(xp, bp, lp_len, mods_rows, lambda i: 0, lp, s5mats, None, None, tl_p, gfin)
        deltas.append(extras[0])
        s5_res.append(extras[1])
        s5_ims.append(extras[2])
        ckvs.append(extras[3])
        kropes.append(extras[4])

        kh_c, vh_c = _kvcache(cache_ckv[:, l].astype(F32).reshape(bs * past, KV_LORA),
                              cache_krope[:, l].astype(F32).reshape(bs * past, QK_ROPE), lp, past)
        ctx = dict(s0=state_delta[:, l].astype(F32), h0=sg_state(state_s5_re[:, l], state_s5_im[:, l]),
                   kh=kh_c, vh=vh_c, past=past)
        out_s, _ = _layer(xs, bs, ls_len, mods_rows, lambda i: 1 + i // tiles_per_seq_s, lp, s5mats, ctx, rope_tabs,
                          tl_s, gfin)
        xp, xs = out_p[0], out_s[0]
        if last:
            yp, ys = out_p[1], out_s[1]

    y_prompt = yp.reshape(bp, lp_len, D_MODEL)
    y_sample = ys.reshape(bs, ls_len, D_MODEL)
    return (y_prompt, y_sample, jnp.stack(deltas, axis=1), jnp.stack(s5_res, axis=1), jnp.stack(s5_ims, axis=1),
            jnp.stack(ckvs, axis=1), jnp.stack(kropes, axis=1))
```

```python
import functools
import math

import jax
import jax.numpy as jnp
from jax import lax
from jax.experimental import pallas as pl
from jax.experimental.pallas import tpu as pltpu

F32 = jnp.float32
BF16 = jnp.bfloat16

D_MODEL = 1024
DEPTH = 4
GRID_W = 64
H_A = 4
HEAD_DIM_A = 128
A_WIDTH = H_A * HEAD_DIM_A
SHORT_CONV = 5
CHUNK = 64
S5_GROUP = 16
S5_STATE = 64
B_WIDTH = 512
S5_GROUPS = B_WIDTH // S5_GROUP
H_C = 8
QK_NOPE = 64
QK_ROPE = 32
V_HEAD = 64
Q_LORA = 384
KV_LORA = 256
ROPE_BASE = 10000.0
N_BRANCH = 3
BRANCH_WIDTH = 512
D_FF = 2816
FFN_CONV = 3
NORM_EPS = 1e-6

LANES = 128
HEAD_PAD = LANES
SUB = 16
S5_T = 8
S5_ROWS = S5_T * S5_GROUP
S5_SG = B_WIDTH // LANES
S5_HALVES = 2
S5_HG = LANES // S5_HALVES // S5_GROUP
S5_TOKW = S5_HG * S5_GROUP
S5_K = S5_T * S5_TOKW
S5_W = S5_HALVES * S5_K

VMEM_LIMIT = 56 * 1024 * 1024
HALO = 16
TOKEN_TILE = 512
ATT_TQ = 1024
ATT_HEADS_LONG = 4
DELTA_NB = 4
S5_STEP_ROWS = 1024
S5_MAX_NB = 8
MOD_ROWS = 16


def _cp(sem):
    return pltpu.CompilerParams(dimension_semantics=sem, vmem_limit_bytes=VMEM_LIMIT)


def _mm(a, b):
    return jnp.dot(a.astype(BF16), b.astype(BF16), preferred_element_type=F32)


def _bmm(a, b):
    return lax.dot_general(a.astype(BF16), b.astype(BF16), (((2,), (1,)), ((0,), (0,))),
                           preferred_element_type=F32)


def _bmm_nt(a, b):
    return lax.dot_general(a.astype(BF16), b.astype(BF16), (((2,), (2,)), ((0,), (0,))),
                           preferred_element_type=F32)


def _bmm_tn(a, b):
    return lax.dot_general(a.astype(BF16), b.astype(BF16), (((1,), (1,)), ((0,), (0,))),
                           preferred_element_type=F32)


def _split3(x):
    x1 = x.astype(BF16)
    r = x - x1.astype(F32)
    x2 = r.astype(BF16)
    x3 = (r - x2.astype(F32)).astype(BF16)
    return x1, x2, x3


def _mm_exact_rhs(a, b_bf16):
    a1, a2, a3 = _split3(a)
    d = lambda t: jnp.dot(t, b_bf16, preferred_element_type=F32)
    return d(a1) + d(a2) + d(a3)


def _mm_exact_lhs(a_bf16, b):
    b1, b2, b3 = _split3(b)
    d = lambda t: jnp.dot(a_bf16, t, preferred_element_type=F32)
    return d(b1) + d(b2) + d(b3)


def _mm3_nt(a, b):
    a1 = a.astype(BF16)
    a2 = (a - a1.astype(F32)).astype(BF16)
    b1 = b.astype(BF16)
    b2 = (b - b1.astype(F32)).astype(BF16)
    d = lambda s, t: lax.dot_general(s, t, (((1,), (1,)), ((), ())), preferred_element_type=F32)
    return d(a1, b1) + d(a1, b2) + d(a2, b1)


def _silu(x):
    return x * jax.nn.sigmoid(x)


def _rms(x):
    return x * lax.rsqrt(jnp.mean(x * x, axis=-1, keepdims=True) + NORM_EPS)


def _mod_kernel(c_ref, w_ref, b_ref, o_ref):
    s = _silu(c_ref[...])
    o_ref[0] = _mm(s, w_ref[0]) + b_ref[0]


def _modulation(cvec, w_mod, b_mod):
    rows = cvec.shape[0]
    nblk = w_mod.shape[-1] // D_MODEL
    return pl.pallas_call(
        _mod_kernel,
        grid=(DEPTH, nblk),
        in_specs=[
            pl.BlockSpec((rows, D_MODEL), lambda l, j: (0, 0)),
            pl.BlockSpec((1, D_MODEL, D_MODEL), lambda l, j: (l, 0, j)),
            pl.BlockSpec((1, 1, D_MODEL), lambda l, j: (l, 0, j)),
        ],
        out_specs=pl.BlockSpec((1, rows, D_MODEL), lambda l, j: (l, 0, j)),
        out_shape=jax.ShapeDtypeStruct((DEPTH, rows, w_mod.shape[-1]), F32),
        compiler_params=_cp(("parallel", "parallel")),
        name="modulation",
    )(cvec, w_mod, b_mod.reshape(DEPTH, 1, -1))


REST_W = A_WIDTH + B_WIDTH + Q_LORA + KV_LORA + LANES
OFF_Z, OFF_U, OFF_QA = 0, A_WIDTH, A_WIDTH + B_WIDTH
OFF_KVA, OFF_SM = OFF_QA + Q_LORA, OFF_QA + Q_LORA + KV_LORA
SM_KR = 16
MLA_W = H_C * HEAD_PAD
INPROJ_SUB = 256


def _rope_apply(x, cos, sin):
    lane = lax.broadcasted_iota(jnp.int32, x.shape, 1)
    partner = jnp.where(lane < QK_NOPE + QK_ROPE // 2,
                        pltpu.roll(x, HEAD_PAD - QK_ROPE // 2, 1),
                        pltpu.roll(x, QK_ROPE // 2, 1))
    return x * cos + partner * sin


def _inproj_kernel(*refs, tl, tiles_per_seq, rope, emit_ctx):
    it = iter(refs)
    xp_ref, xc_ref, xn_ref = next(it), next(it), next(it)
    shift_ref, scale_ref, gnorm_ref = next(it), next(it), next(it)
    wqkv_ref, convw_ref, wrest_ref = next(it), next(it), next(it)
    gqa_ref, gkva_ref, alog_ref, dtb_ref = next(it), next(it), next(it), next(it)
    wq_ref, wk_ref, wv_ref = next(it), next(it), next(it)
    cos_ref = sin_ref = None
    if rope:
        cos_ref, sin_ref = next(it), next(it)
    qkv_out, zs_out, u_out, bg_out = next(it), next(it), next(it), next(it)
    ckv_out = kr_out = None
    if emit_ctx:
        ckv_out, kr_out = next(it), next(it)
    qh_out, kh_out, vh_out = next(it), next(it), next(it)

    i = pl.program_id(0)
    pos = i % tiles_per_seq
    mod_scale = 1.0 + scale_ref[0]
    mod_shift = shift_ref[0]
    gain = gnorm_ref[...]

    def norm_mod(x):
        return ((_rms(x) * gain) * mod_scale + mod_shift).astype(BF16)

    nsub = tl // INPROJ_SUB
    ts = INPROJ_SUB
    n_ext = ts + 2 * HALO
    rowi = lax.broadcasted_iota(jnp.int32, (n_ext, 1), 0)
    pad = SHORT_CONV // 2
    qscale = (QK_NOPE + QK_ROPE) ** -0.5 * math.log2(math.e)
    for sub in range(nsub):
        r0 = sub * ts
        rows = slice(r0, r0 + ts)
        h_cur = norm_mod(xc_ref[rows, :])
        h_before = norm_mod(xp_ref[...] if sub == 0 else xc_ref[r0 - HALO:r0, :])
        h_after = norm_mod(xn_ref[...] if sub == nsub - 1 else xc_ref[r0 + ts:r0 + ts + HALO, :])
        h_ext = jnp.concatenate([h_before, h_cur, h_after], axis=0)

        qkv_ext = _mm(h_ext, wqkv_ref[...])
        if sub == 0 or sub == nsub - 1:
            ok_lo = jnp.logical_or(pos > 0, rowi >= HALO) if sub == 0 else True
            ok_hi = jnp.logical_or(pos < tiles_per_seq - 1, rowi < HALO + ts) if sub == nsub - 1 else True
            qkv_ext = qkv_ext * jnp.logical_and(ok_lo, ok_hi).astype(F32)
        for blk in range(3):
            cols = slice(blk * A_WIDTH, (blk + 1) * A_WIDTH)
            xblk = qkv_ext[:, cols]
            acc = None
            for t in range(SHORT_CONV):
                sh = (pad - t) % n_ext
                src = xblk if sh == 0 else pltpu.roll(xblk, sh, 0)
                term = src[HALO:HALO + ts, :] * convw_ref[t:t + 1, cols]
                acc = term if acc is None else acc + term
            acc = _silu(acc)
            if blk < 2:
                parts = []
                for hh in range(H_A):
                    a = acc[:, hh * HEAD_DIM_A:(hh + 1) * HEAD_DIM_A]
                    parts.append(a * lax.rsqrt(jnp.sum(a * a, axis=-1, keepdims=True) + NORM_EPS))
                acc = jnp.concatenate(parts, axis=1)
            qkv_out[rows, cols] = acc.astype(BF16)

        rest = _mm(h_cur, wrest_ref[...])
        zs_out[rows, :] = _silu(rest[:, OFF_Z:OFF_Z + A_WIDTH])
        u_out[rows, :] = rest[:, OFF_U:OFF_U + B_WIDTH]

        small = rest[:, OFF_SM:OFF_SM + LANES]
        lane = lax.broadcasted_iota(jnp.int32, small.shape, 1)
        beta = jax.nn.sigmoid(small)
        glog = -jnp.exp(alog_ref[...]) * jax.nn.softplus(small + dtb_ref[...])
        bg_out[rows, :] = jnp.where(lane < 2 * H_A, beta, glog)

        qa = _rms(rest[:, OFF_QA:OFF_QA + Q_LORA]) * gqa_ref[...]
        ckv = _rms(rest[:, OFF_KVA:OFF_KVA + KV_LORA]) * gkva_ref[...]
        if emit_ctx:
            ckv_out[rows, :] = ckv
            kr_out[rows, :] = small[:, SM_KR:SM_KR + QK_ROPE]
        ckv_b = ckv.astype(BF16)
        qhat = _mm(qa, wq_ref[...])
        khat = _mm(ckv_b, wk_ref[...])
        vh_out[rows, :] = _mm(ckv_b, wv_ref[...]).astype(BF16)
        kr_al = jnp.where((lane >= QK_NOPE) & (lane < QK_NOPE + QK_ROPE),
                          pltpu.roll(small, QK_NOPE - SM_KR, 1), 0.0)
        if rope:
            cos, sin = cos_ref[rows, :], sin_ref[rows, :]
            kr_al = _rope_apply(kr_al, cos, sin)
        for hh in range(H_C):
            cs = slice(hh * HEAD_PAD, (hh + 1) * HEAD_PAD)
            qh = qhat[:, cs]
            if rope:
                qh = _rope_apply(qh, cos, sin)
            qh_out[rows, cs] = (qh * qscale).astype(BF16)
            kh_out[rows, cs] = (khat[:, cs] + kr_al).astype(BF16)


def _halo_maps(n, tl):
    nh = n // HALO

    def xprev(i):
        return (jnp.maximum(i * (tl // HALO) - 1, 0), 0)

    def xnext(i):
        return (jnp.minimum((i + 1) * (tl // HALO), nh - 1), 0)

    return xprev, xnext


def _inproj(x, mods_rows, row_of_tile, lp, seq_len, tl, rope_tabs, emit_ctx):
    n = x.shape[0]
    nt = n // tl
    tps = seq_len // tl
    rope = rope_tabs is not None
    xprev, xnext = _halo_maps(n, tl)
    const2 = lambda i: (0, 0)
    in_specs = [
        pl.BlockSpec((HALO, D_MODEL), xprev),
        pl.BlockSpec((tl, D_MODEL), lambda i: (i, 0)),
        pl.BlockSpec((HALO, D_MODEL), xnext),
        pl.BlockSpec((1, 1, D_MODEL), lambda i: (row_of_tile(i) * 6 + 0, 0, 0)),
        pl.BlockSpec((1, 1, D_MODEL), lambda i: (row_of_tile(i) * 6 + 1, 0, 0)),
        pl.BlockSpec((1, D_MODEL), const2),
        pl.BlockSpec((D_MODEL, 3 * A_WIDTH), const2),
        pl.BlockSpec((SHORT_CONV, 3 * A_WIDTH), const2),
        pl.BlockSpec((D_MODEL, REST_W), const2),
        pl.BlockSpec((1, Q_LORA), const2),
        pl.BlockSpec((1, KV_LORA), const2),
        pl.BlockSpec((1, LANES), const2),
        pl.BlockSpec((1, LANES), const2),
        pl.BlockSpec((Q_LORA, MLA_W), const2),
        pl.BlockSpec((KV_LORA, MLA_W), const2),
        pl.BlockSpec((KV_LORA, MLA_W), const2),
    ]
    args = [x, x, x, mods_rows, mods_rows, lp['g_norm_mix'], lp['w_qkv'], lp['conv_qkv'], lp['w_rest'],
            lp['g_q_a'], lp['g_kv_a'], lp['a_log128'], lp['dt_bias128'], lp['w_q_pad'], lp['w_k_pad'], lp['w_v_pad']]
    if rope:
        in_specs += [pl.BlockSpec((tl, HEAD_PAD), lambda i: (i % tps, 0))] * 2
        args += list(rope_tabs)
    tok = lambda w: pl.BlockSpec((tl, w), lambda i: (i, 0))
    out_specs = [tok(3 * A_WIDTH), tok(A_WIDTH), tok(B_WIDTH), tok(LANES)]
    out_shape = [jax.ShapeDtypeStruct((n, 3 * A_WIDTH), BF16), jax.ShapeDtypeStruct((n, A_WIDTH), F32),
                 jax.ShapeDtypeStruct((n, B_WIDTH), F32), jax.ShapeDtypeStruct((n, LANES), F32)]
    if emit_ctx:
        out_specs += [tok(KV_LORA), tok(QK_ROPE)]
        out_shape += [jax.ShapeDtypeStruct((n, KV_LORA), F32), jax.ShapeDtypeStruct((n, QK_ROPE), F32)]
    out_specs += [tok(MLA_W)] * 3
    out_shape += [jax.ShapeDtypeStruct((n, MLA_W), BF16)] * 3
    return pl.pallas_call(
        functools.partial(_inproj_kernel, tl=tl, tiles_per_seq=tps, rope=rope, emit_ctx=emit_ctx),
        grid=(nt,),
        in_specs=in_specs,
        out_specs=out_specs,
        out_shape=out_shape,
        compiler_params=_cp(("parallel",)),
        name="inproj",
    )(*args)


def _kvcache_kernel(ckv_ref, kr_ref, wk_ref, wv_ref, kh_out, vh_out):
    ckv_b = ckv_ref[...].astype(BF16)
    khat = _mm(ckv_b, wk_ref[...])
    vh_out[...] = _mm(ckv_b, wv_ref[...]).astype(BF16)
    kr_al = kr_ref[...]
    for hh in range(H_C):
        cs = slice(hh * HEAD_PAD, (hh + 1) * HEAD_PAD)
        kh_out[:, cs] = (khat[:, cs] + kr_al).astype(BF16)


def _kvcache(ckv, kr, lp, tl):
    n = ckv.shape[0]
    const2 = lambda i: (0, 0)
    kr = jnp.pad(kr, ((0, 0), (QK_NOPE, HEAD_PAD - QK_NOPE - QK_ROPE)))
    return pl.pallas_call(
        _kvcache_kernel,
        grid=(n // tl,),
        in_specs=[pl.BlockSpec((tl, KV_LORA), lambda i: (i, 0)), pl.BlockSpec((tl, HEAD_PAD), lambda i: (i, 0)),
                  pl.BlockSpec((KV_LORA, MLA_W), const2), pl.BlockSpec((KV_LORA, MLA_W), const2)],
        out_specs=[pl.BlockSpec((tl, MLA_W), lambda i: (i, 0))] * 2,
        out_shape=[jax.ShapeDtypeStruct((n, MLA_W), BF16)] * 2,
        compiler_params=_cp(("parallel",)),
        name="kvcache",
    )(ckv, kr, lp['w_k_pad'], lp['w_v_pad'])


ATT_SUB = 256
ATT_BATCHED_MAX = 512


def _attn_kernel(*refs, nseg, tq, hps, batched):
    q_ref = refs[0]
    k_refs = refs[1:1 + nseg]
    v_refs = refs[1 + nseg:1 + 2 * nseg]
    o_ref = refs[1 + 2 * nseg]
    def one_head(hh, rows):
        cs = slice(hh * HEAD_PAD, (hh + 1) * HEAD_PAD)
        q = q_ref[rows, cs]
        s = [lax.dot_general(q, k[:, cs], (((1,), (1,)), ((), ())), preferred_element_type=F32) for k in k_refs]
        m = s[0].max(axis=-1, keepdims=True)
        for t in s[1:]:
            m = jnp.maximum(m, t.max(axis=-1, keepdims=True))
        p = [jnp.exp2(t - m) for t in s]
        den = p[0].sum(axis=-1, keepdims=True)
        for t in p[1:]:
            den = den + t.sum(axis=-1, keepdims=True)
        acc = None
        for t, v in zip(p, v_refs):
            part = jnp.dot(t.astype(BF16), v[:, cs], preferred_element_type=F32)
            acc = part if acc is None else acc + part
        return acc / den

    def store_pair(pair, rows, o_even, o_odd):
        o = o_even + pltpu.roll(o_odd, V_HEAD, 1)
        o_ref[rows, pair * HEAD_PAD:(pair + 1) * HEAD_PAD] = o.astype(BF16)

    if batched:
        hsl = [slice(hh * HEAD_PAD, (hh + 1) * HEAD_PAD) for hh in range(hps)]
        q = jnp.stack([q_ref[:, cs] for cs in hsl])
        s = [_bmm_nt(q, jnp.stack([k[:, cs] for cs in hsl])) for k in k_refs]
        m = s[0].max(axis=-1, keepdims=True)
        for t in s[1:]:
            m = jnp.maximum(m, t.max(axis=-1, keepdims=True))
        p = [jnp.exp2(t - m) for t in s]
        den = p[0].sum(axis=-1, keepdims=True)
        for t in p[1:]:
            den = den + t.sum(axis=-1, keepdims=True)
        acc = None
        for t, v in zip(p, v_refs):
            part = _bmm(t, jnp.stack([v[:, cs] for cs in hsl]))
            acc = part if acc is None else acc + part
        o = acc / den
        for pair in range(hps // 2):
            store_pair(pair, slice(0, tq), o[2 * pair], o[2 * pair + 1])
        return

    for pair in range(hps // 2):
        for sb in range(tq // ATT_SUB):
            rows = slice(sb * ATT_SUB, (sb + 1) * ATT_SUB)
            store_pair(pair, rows, one_head(2 * pair, rows), one_head(2 * pair + 1, rows))


def _attention(qh, segs, bsz, seq_len, tq, hps):
    nq = seq_len // tq
    wblk = hps * HEAD_PAD
    in_specs = [pl.BlockSpec((tq, wblk), lambda b, h, i: (b * nq + i, h))]
    args = [qh]
    for which in (0, 1):
        for seg in segs:
            in_specs.append(pl.BlockSpec((seg[2], wblk), lambda b, h, i: (b, h)))
            args.append(seg[which])
    return pl.pallas_call(
        functools.partial(_attn_kernel, nseg=len(segs), tq=tq, hps=hps, batched=hps == H_C),
        grid=(bsz, H_C // hps, nq),
        in_specs=in_specs,
        out_specs=pl.BlockSpec((tq, hps * V_HEAD), lambda b, h, i: (b * nq + i, h)),
        out_shape=jax.ShapeDtypeStruct((bsz * seq_len, H_C * V_HEAD), BF16),
        compiler_params=_cp(("parallel", "parallel", "parallel")),
        name="attention",
    )(*args)


DELTA_TB = 256
DELTA_G = DELTA_TB // CHUNK


def _delta_prep_kernel(qkv_ref, bg_ref, uf_ref, wf_ref, qef_ref, kef_ref, qkf_ref,
                       ub_ref, wb_ref, qeb_ref, keb_ref, qkb_ref, dec_ref):
    c = CHUNK
    outs = ((uf_ref, wf_ref, qef_ref, kef_ref, qkf_ref), (ub_ref, wb_ref, qeb_ref, keb_ref, qkb_ref))
    ri = lax.broadcasted_iota(jnp.int32, (c, c), 0)
    ci = lax.broadcasted_iota(jnp.int32, (c, c), 1)
    tril = (ri >= ci).astype(BF16)
    triu = (ri <= ci).astype(BF16)
    same_blk = (ri // SUB) == (ci // SUB)
    scale = HEAD_DIM_A ** -0.5

    insts = [(g, d, hh) for g in range(DELTA_G) for d in range(2) for hh in range(H_A)]
    n = len(insts)
    qkv = qkv_ref[...]
    bg = bg_ref[...]

    def head_stack(base):
        return jnp.stack([qkv[g * c:(g + 1) * c, base + hh * HEAD_DIM_A:base + (hh + 1) * HEAD_DIM_A]
                          for g, d, hh in insts])

    q, k, v = head_stack(0), head_stack(A_WIDTH), head_stack(2 * A_WIDTH)

    per_slab = LANES // c
    bg_t = [bg[s * LANES:(s + 1) * LANES, :].T for s in range(DELTA_TB // LANES)]
    cols, rws = {}, {}
    for g in range(DELTA_G):
        bgc = bg[g * c:(g + 1) * c, :]
        bgr = bg_t[g // per_slab][0:16, (g % per_slab) * c:(g % per_slab + 1) * c]
        cols[g] = (bgc, _mm_exact_lhs(tril, bgc), _mm_exact_lhs(triu, bgc))
        rws[g] = (_mm_exact_rhs(bgr, triu), _mm_exact_rhs(bgr, tril))
    beta = jnp.stack([cols[g][0][:, d * H_A + hh:d * H_A + hh + 1] for g, d, hh in insts])
    gcc = jnp.stack([cols[g][1 + d][:, 2 * H_A + d * H_A + hh:2 * H_A + d * H_A + hh + 1]
                     for g, d, hh in insts])
    gcr = jnp.stack([rws[g][d][2 * H_A + d * H_A + hh:2 * H_A + d * H_A + hh + 1, :] for g, d, hh in insts])
    glast = jnp.stack([gcc[i, (c - 1 if insts[i][1] == 0 else 0):(c if insts[i][1] == 0 else 1), :]
                       for i in range(n)])

    sign = jnp.stack([jnp.full((1, 1), 1 - 2 * d, jnp.int32) for g, d, hh in insts])
    tri = (ri - ci)[None] * sign
    incl = tri >= 0
    strict = tri > 0
    blk = jnp.broadcast_to(same_blk[None], incl.shape)

    kf = k.astype(F32)
    kb = kf * beta
    decay = jnp.where(incl, jnp.exp(jnp.where(incl, gcc - gcr, 0.0)), 0.0)
    a = _bmm_nt(kb, k) * decay
    qk = _bmm_nt(q, k) * (decay * scale)
    dg = jnp.where(blk & strict, a, 0.0)
    lo = jnp.where(jnp.logical_not(blk) & strict, a, 0.0)
    m1 = _bmm(dg, dg)
    p = m1 - dg - _bmm(dg, m1)
    pw = m1
    for _ in range(int(math.log2(SUB)) - 2):
        pw = _bmm(pw, pw)
        p = p + pw + _bmm(p, pw)
    nm = lo + _bmm(p, lo)
    n2 = _bmm(nm, nm)
    t1 = n2 - nm - _bmm(nm, n2)
    tm = t1 + p + _bmm(t1, p)
    e_col = jnp.exp(gcc)
    rhs = jnp.concatenate([v.astype(F32) * beta, kb * e_col], axis=2)
    x = rhs + _bmm(tm, rhs)
    u, w = x[:, :, :HEAD_DIM_A], x[:, :, HEAD_DIM_A:]
    qe = q.astype(F32) * (e_col * scale)
    ke = kf * jnp.exp(glast - gcc)
    dec = jnp.exp(glast)
    zpad = jnp.zeros((c, HEAD_DIM_A - c), BF16)
    for i, (g, d, hh) in enumerate(insts):
        rows = slice(g * c, (g + 1) * c)
        cs = slice(hh * HEAD_DIM_A, (hh + 1) * HEAD_DIM_A)
        u_ref, w_ref, qe_ref, ke_ref, qk_ref = outs[d]
        u_ref[0, rows, cs] = u[i].astype(BF16)
        w_ref[0, rows, cs] = w[i].astype(BF16)
        qe_ref[0, rows, cs] = qe[i].astype(BF16)
        ke_ref[0, rows, cs] = ke[i].astype(BF16)
        qk_ref[0, rows, cs] = jnp.concatenate([qk[i].astype(BF16), zpad], axis=1)
        dec_ref[0, g, d * H_A + hh:d * H_A + hh + 1, :] = jnp.broadcast_to(dec[i], (1, HEAD_DIM_A))


def _delta_prep(qkv, bg, bsz, seq_len):
    nblk = seq_len // DELTA_TB
    nc = seq_len // CHUNK
    tokb = jax.ShapeDtypeStruct((bsz, seq_len, A_WIDTH), BF16)
    tspec = pl.BlockSpec((1, DELTA_TB, A_WIDTH), lambda b, j: (b, j, 0))
    return pl.pallas_call(
        _delta_prep_kernel,
        grid=(bsz, nblk),
        in_specs=[pl.BlockSpec((DELTA_TB, 3 * A_WIDTH), lambda b, j: (b * nblk + j, 0)),
                  pl.BlockSpec((DELTA_TB, LANES), lambda b, j: (b * nblk + j, 0))],
        out_specs=[tspec] * 10 + [pl.BlockSpec((1, DELTA_G, 2 * H_A, HEAD_DIM_A), lambda b, j: (b, j, 0, 0))],
        out_shape=[tokb] * 10 + [jax.ShapeDtypeStruct((bsz, nc, 2 * H_A, HEAD_DIM_A), F32)],
        compiler_params=_cp(("parallel", "parallel")),
        name="delta_prep",
    )(qkv, bg)


def _delta_recur_kernel(*refs, nb, nblk, has_s0, emit_state):
    it = iter(refs)
    fwd = [next(it) for _ in range(5)]
    bwd = [next(it) for _ in range(5)]
    decf_ref, decb_ref = next(it), next(it)
    s0_ref = next(it) if has_s0 else None
    of_ref, ob_ref = next(it), next(it)
    sfin_ref = next(it) if emit_state else None
    s_scr = next(it)
    j = pl.program_id(1)
    c = CHUNK

    @pl.when(j == 0)
    def _():
        if has_s0:
            s_scr[...] = s0_ref[...]
        else:
            s_scr[...] = jnp.zeros(s_scr.shape, F32)

    insts = [(bi, d, hh) for bi in range(nb) for d in range(2) for hh in range(H_A)]
    for step in range(DELTA_G):
        gsel = (step, DELTA_G - 1 - step)

        def stk(idx, width=HEAD_DIM_A):
            return jnp.stack([(fwd, bwd)[d][idx][bi, gsel[d] * c:(gsel[d] + 1) * c,
                                                 hh * HEAD_DIM_A:hh * HEAD_DIM_A + width]
                              for bi, d, hh in insts])

        u, w, qe, ke, qk = stk(0), stk(1), stk(2), stk(3), stk(4, c)
        dec = jnp.stack([(decf_ref, decb_ref)[d][bi, gsel[d], d * H_A + hh:d * H_A + hh + 1, :]
                         for bi, d, hh in insts])
        s_old = jnp.stack([s_scr[bi, d, hh] for bi, d, hh in insts])
        r1 = _bmm(jnp.concatenate([w, qe], axis=1), s_old)
        v_new = (u.astype(F32) - r1[:, :c, :]).astype(BF16)
        o = r1[:, c:, :] + _bmm(qk, v_new)
        s_new = s_old * dec + _bmm_tn(ke, v_new)
        for i, (bi, d, hh) in enumerate(insts):
            s_scr[bi, d, hh] = s_new[i]
            (of_ref, ob_ref)[d][bi, gsel[d] * c:(gsel[d] + 1) * c, hh * HEAD_DIM_A:(hh + 1) * HEAD_DIM_A] = o[i]

    if emit_state:
        @pl.when(j == nblk - 1)
        def _():
            sfin_ref[...] = s_scr[...]


def _delta_recur(prep, s0, bsz, seq_len, nb, emit_state):
    nblk = seq_len // DELTA_TB
    has_s0 = s0 is not None
    fspec = pl.BlockSpec((nb, DELTA_TB, A_WIDTH), lambda b, j: (b, j, 0))
    bspec = pl.BlockSpec((nb, DELTA_TB, A_WIDTH), lambda b, j: (b, nblk - 1 - j, 0))
    dspec_f = pl.BlockSpec((nb, DELTA_G, 2 * H_A, HEAD_DIM_A), lambda b, j: (b, j, 0, 0))
    dspec_b = pl.BlockSpec((nb, DELTA_G, 2 * H_A, HEAD_DIM_A), lambda b, j: (b, nblk - 1 - j, 0, 0))
    st_spec = pl.BlockSpec((nb, 2, H_A, HEAD_DIM_A, HEAD_DIM_A), lambda b, j: (b, 0, 0, 0, 0))
    in_specs = [fspec] * 5 + [bspec] * 5 + [dspec_f, dspec_b]
    args = list(prep[:10]) + [prep[10], prep[10]]
    if has_s0:
        in_specs.append(st_spec)
        args.append(s0)
    out_specs = [fspec, bspec]
    out_shape = [jax.ShapeDtypeStruct((bsz, seq_len, A_WIDTH), F32)] * 2
    if emit_state:
        out_specs.append(st_spec)
        out_shape.append(jax.ShapeDtypeStruct((bsz, 2, H_A, HEAD_DIM_A, HEAD_DIM_A), F32))
    return pl.pallas_call(
        functools.partial(_delta_recur_kernel, nb=nb, nblk=nblk, has_s0=has_s0, emit_state=emit_state),
        grid=(bsz // nb, nblk),
        in_specs=in_specs,
        out_specs=out_specs,
        out_shape=out_shape,
        scratch_shapes=[pltpu.VMEM((nb, 2, H_A, HEAD_DIM_A, HEAD_DIM_A), F32)],
        compiler_params=_cp(("parallel", "arbitrary")),
        name="delta_recur",
    )(*args)


def _s5prep_kernel(lre_ref, lim_ref, ldt_ref, bre_ref, bim_ref, cre_ref, cim_ref,
                   m_out, ef_out, eb_out, ff_out, fb_out, lt_out, xcat_scr, qcat_scr, ft_scr):
    gs, p, cg, t8, half = S5_HG, S5_STATE, S5_GROUP, S5_T, S5_K // 2
    j = (lax.broadcasted_iota(jnp.int32, (S5_ROWS, 1), 0) // cg).astype(F32)
    tile = lambda m: jnp.concatenate([m] * t8, axis=0)
    one = jnp.ones((1, 1), F32)
    bd = {}
    for d in range(2):
        e_out = (ef_out, eb_out)[d]
        f_out = (ff_out, fb_out)[d]
        xcat_scr[...] = jnp.zeros(xcat_scr.shape, F32)
        qcat_scr[...] = jnp.zeros(qcat_scr.shape, F32)
        ft_scr[...] = jnp.zeros(ft_scr.shape, F32)
        e_out[0] = jnp.zeros((S5_K, S5_K), BF16)
        for gl in range(gs):
            lre, lim = lre_ref[0, d, 0, gl:gl + 1, :], lim_ref[0, d, 0, gl:gl + 1, :]
            dt = jnp.exp(ldt_ref[0, d, 0, gl:gl + 1, :])
            zr, zi = lre * dt, lim * dt

            def lam_pow(e):
                mag = jnp.exp(e * zr)
                return mag * jnp.cos(e * zi), mag * jnp.sin(e * zi)

            l1r, l1i = lam_pow(one)
            den = lre * lre + lim * lim
            nr, ni = l1r - 1.0, l1i
            cfr = (nr * lre + ni * lim) / den
            cfi = (ni * lre - nr * lim) / den
            bre, bim = bre_ref[0, gl], bim_ref[0, gl]
            bbr = cfr * bre - cfi * bim
            bbi = cfr * bim + cfi * bre
            cre, cim = cre_ref[0, gl], cim_ref[0, gl]
            c_r, c_i = tile(cre), tile(cim)
            b_r, b_i = tile(bbr), tile(bbi)
            grow = slice(gl * cg, (gl + 1) * cg)
            lre_c, lim_c = slice(gl * p, (gl + 1) * p), slice(half + gl * p, half + (gl + 1) * p)

            xcat_scr[grow, 2 * gl * p:(2 * gl + 1) * p] = bbr
            xcat_scr[grow, (2 * gl + 1) * p:(2 * gl + 2) * p] = bbi
            pr, pi = lam_pow(j)
            qr = pr * c_r - pi * c_i
            qi = pr * c_i + pi * c_r
            for jj in range(t8):
                qcat_scr[jj, grow, 2 * gl * p:(2 * gl + 1) * p] = qr[jj * cg:(jj + 1) * cg, :]
                qcat_scr[jj, grow, (2 * gl + 1) * p:(2 * gl + 2) * p] = -qi[jj * cg:(jj + 1) * cg, :]

            er, ei = lam_pow(j if d else (t8 - 1.0) - j)
            e_r = (er * b_r - ei * b_i).astype(BF16)
            e_i = (er * b_i + ei * b_r).astype(BF16)
            fr, fi = lam_pow((t8 - j) if d else j + 1.0)
            f_r = fr * c_r - fi * c_i
            f_i = -(fr * c_i + fi * c_r)
            for s in range(t8):
                rows = slice(s * S5_TOKW + gl * cg, s * S5_TOKW + (gl + 1) * cg)
                e_out[0, rows, lre_c] = e_r[s * cg:(s + 1) * cg, :]
                e_out[0, rows, lim_c] = e_i[s * cg:(s + 1) * cg, :]
                ft_scr[rows, lre_c] = f_r[s * cg:(s + 1) * cg, :]
                ft_scr[rows, lim_c] = f_i[s * cg:(s + 1) * cg, :]

            ltr, lti = lam_pow(one * float(t8))
            lt_out[0, d:d + 1, lre_c] = ltr
            lt_out[0, d:d + 1, lim_c] = lti
        f_out[0] = ft_scr[...].T.astype(BF16)
        xcat = xcat_scr[...]
        for jj in range(t8):
            bd[(d, jj)] = _mm3_nt(xcat, qcat_scr[jj])
    for s in range(t8):
        for t in range(t8):
            blk = bd[(0, t - s)] if t > s else (bd[(1, s - t)] if s > t else bd[(0, 0)] + bd[(1, 0)])
            m_out[0, s * S5_TOKW:(s + 1) * S5_TOKW, t * S5_TOKW:(t + 1) * S5_TOKW] = blk.astype(BF16)


def _s5prep(lam_re, lam_im, log_dt, b_re, b_im, c_re, c_im):
    gs, p, cg = S5_HG, S5_STATE, S5_GROUP
    per_layer = S5_SG * S5_HALVES
    nq = DEPTH * per_layer
    lam_idx = lambda q: (q // per_layer, 0, q % per_layer, 0, 0)
    par_idx = lambda q: (q, 0, 0, 0)
    lam5 = lambda v, w: v.reshape(DEPTH, 2, per_layer, gs, w)
    b_t = lambda v: v.reshape(nq, gs, p, cg).transpose(0, 1, 3, 2)
    par4 = lambda v: v.reshape(nq, gs, cg, p)
    mat = pl.BlockSpec((1, S5_K, S5_K), lambda q: (q, 0, 0))
    return pl.pallas_call(
        _s5prep_kernel,
        grid=(nq,),
        in_specs=[pl.BlockSpec((1, 2, 1, gs, p), lam_idx), pl.BlockSpec((1, 2, 1, gs, p), lam_idx),
                  pl.BlockSpec((1, 2, 1, gs, 1), lam_idx)] + [pl.BlockSpec((1, gs, cg, p), par_idx)] * 4,
        out_specs=[mat] * 5 + [pl.BlockSpec((1, 2, S5_K), lambda q: (q, 0, 0))],
        out_shape=[jax.ShapeDtypeStruct((nq, S5_K, S5_K), BF16)] * 5 + [jax.ShapeDtypeStruct((nq, 2, S5_K), F32)],
        scratch_shapes=[pltpu.VMEM((S5_TOKW, S5_K), F32), pltpu.VMEM((S5_T, S5_TOKW, S5_K), F32),
                        pltpu.VMEM((S5_K, S5_K), F32)],
        compiler_params=_cp(("parallel",)),
        name="s5prep",
    )(lam5(lam_re, p), lam5(lam_im, p), lam5(log_dt, 1), b_t(b_re), b_t(b_im), par4(c_re), par4(c_im))


S5_RT = 256


def _s5_kernel(*refs, nchunk, nb, seq_len, has_h0, emit_state):
    it = iter(refs)
    u_ref = next(it)
    m_ref, ef_ref, eb_ref, ff_ref, fb_ref, lt_ref = (next(it) for _ in range(6))
    h0_ref = next(it) if has_h0 else None
    y_ref = next(it)
    hfin_ref = next(it) if emit_state else None
    up_scr, sf_scr, sb_scr = next(it), next(it), next(it)

    r = nchunk * nb
    rtile = min(S5_RT, r)
    half = S5_K // 2
    for t in range(S5_T):
        for b in range(nb):
            up_scr[t, pl.ds(b, nchunk, stride=nb), :] = u_ref[pl.ds(b * seq_len + t, nchunk, stride=S5_T), :]

    def low_lanes(shape):
        return lax.broadcasted_iota(jnp.int32, shape, 1) < S5_TOKW

    def u_tiles(rt):
        rows = slice(rt * rtile, (rt + 1) * rtile)
        low = low_lanes((rtile, LANES))
        halves = [[], []]
        for jt in range(S5_T // 2):
            pa, pb = up_scr[2 * jt, rows, :], up_scr[2 * jt + 1, rows, :]
            halves[0].append(jnp.where(low, pa, pltpu.roll(pb, S5_TOKW, 1)))
            halves[1].append(jnp.where(low, pltpu.roll(pa, S5_TOKW, 1), pb))
        return [jnp.concatenate(hv, axis=1).astype(BF16) for hv in halves]

    for rt in range(r // rtile):
        rows = slice(rt * rtile, (rt + 1) * rtile)
        for hv, ub in enumerate(u_tiles(rt)):
            cs = slice(hv * S5_K, (hv + 1) * S5_K)
            sf_scr[rows, cs] = jnp.dot(ub, ef_ref[0, hv], preferred_element_type=F32)
            sb_scr[rows, cs] = jnp.dot(ub, eb_ref[0, hv], preferred_element_type=F32)

    def swap_re_im(x):
        parts = []
        for hv in range(S5_HALVES):
            parts += [x[:, hv * S5_K + half:(hv + 1) * S5_K], x[:, hv * S5_K:hv * S5_K + half]]
        return jnp.concatenate(parts, axis=1)

    def coeffs(d):
        lt = jnp.concatenate([lt_ref[0, hv, d:d + 1, :] for hv in range(S5_HALVES)], axis=1)
        sw = swap_re_im(lt)
        is_re = (lax.broadcasted_iota(jnp.int32, (1, S5_W), 1) % S5_K) < half
        return jnp.where(is_re, lt, sw), jnp.where(is_re, -sw, lt)

    af, bf = coeffs(0)
    ab, bb = coeffs(1)
    if has_h0:
        h0f, h0b = h0_ref[0, 0, 0], h0_ref[0, 1, 0]
    else:
        h0f = h0b = jnp.zeros((nb, S5_W), F32)

    tile_rows = max(nb, 8)
    cps = tile_rows // nb
    n_iter = nchunk // cps

    def step(h, a, b, seg):
        return a * h + b * swap_re_im(h) + seg

    def body(i, carry):
        hf, hb = carry
        rf = pl.ds(pl.multiple_of(i * tile_rows, tile_rows), tile_rows)
        rb = pl.ds(pl.multiple_of((n_iter - 1 - i) * tile_rows, tile_rows), tile_rows)
        xf, xb = sf_scr[rf, :], sb_scr[rb, :]
        of, ob = [], [None] * cps
        for jj in range(cps):
            of.append(hf)
            hf = step(hf, af, bf, xf[jj * nb:(jj + 1) * nb, :])
        for jj in reversed(range(cps)):
            ob[jj] = hb
            hb = step(hb, ab, bb, xb[jj * nb:(jj + 1) * nb, :])
        sf_scr[rf, :] = of[0] if cps == 1 else jnp.concatenate(of, axis=0)
        sb_scr[rb, :] = ob[0] if cps == 1 else jnp.concatenate(ob, axis=0)
        return hf, hb

    hf, hb = lax.fori_loop(0, n_iter, body, (h0f, h0b))
    if emit_state:
        hfin_ref[0, 0, 0] = hf
        hfin_ref[0, 1, 0] = hb

    for rt in range(r // rtile):
        rows = slice(rt * rtile, (rt + 1) * rtile)
        ys = []
        for hv, ub in enumerate(u_tiles(rt)):
            cs = slice(hv * S5_K, (hv + 1) * S5_K)
            y = jnp.dot(ub, m_ref[0, hv], preferred_element_type=F32)
            y = y + jnp.dot(sf_scr[rows, cs].astype(BF16), ff_ref[0, hv], preferred_element_type=F32)
            ys.append(y + jnp.dot(sb_scr[rows, cs].astype(BF16), fb_ref[0, hv], preferred_element_type=F32))
        low = low_lanes((rtile, LANES))
        for jt in range(S5_T // 2):
            y0, y1 = ys[0][:, jt * LANES:(jt + 1) * LANES], ys[1][:, jt * LANES:(jt + 1) * LANES]
            up_scr[2 * jt, rows, :] = jnp.where(low, y0, pltpu.roll(y1, S5_TOKW, 1))
            up_scr[2 * jt + 1, rows, :] = jnp.where(low, pltpu.roll(y0, S5_TOKW, 1), y1)
    for t in range(S5_T):
        for b in range(nb):
            y_ref[pl.ds(b * seq_len + t, nchunk, stride=S5_T), :] = up_scr[t, pl.ds(b, nchunk, stride=nb), :]


def _s5(u, mats_l, h0, bsz, seq_len, nb, emit_state):
    nchunk = seq_len // S5_T
    r = nchunk * nb
    has_h0 = h0 is not None
    qi = lambda q, j: (q, 0, 0, 0)
    wspec = pl.BlockSpec((1, S5_HALVES, S5_K, S5_K), qi)
    in_specs = [pl.BlockSpec((nb * seq_len, LANES), lambda q, j: (j, q))] + [wspec] * 5 \
        + [pl.BlockSpec((1, S5_HALVES, 2, S5_K), qi)]
    args = [u] + list(mats_l)
    st_spec = pl.BlockSpec((1, 2, 1, nb, S5_W), lambda q, j: (q, 0, j, 0, 0))
    st_shape = (S5_SG, 2, bsz // nb, nb, S5_W)
    if has_h0:
        in_specs.append(st_spec)
        args.append(h0.reshape(st_shape))
    out_specs = [pl.BlockSpec((nb * seq_len, LANES), lambda q, j: (j, q))]
    out_shape = [jax.ShapeDtypeStruct((bsz * seq_len, B_WIDTH), F32)]
    if emit_state:
        out_specs.append(st_spec)
        out_shape.append(jax.ShapeDtypeStruct(st_shape, F32))
    outs = pl.pallas_call(
        functools.partial(_s5_kernel, nchunk=nchunk, nb=nb, seq_len=seq_len, has_h0=has_h0, emit_state=emit_state),
        grid=(S5_SG, bsz // nb),
        in_specs=in_specs,
        out_specs=out_specs,
        out_shape=out_shape,
        scratch_shapes=[pltpu.VMEM((S5_T, r, LANES), F32), pltpu.VMEM((r, S5_W), F32), pltpu.VMEM((r, S5_W), F32)],
        compiler_params=_cp(("parallel", "parallel")),
        name="s5",
    )(*args)
    if emit_state:
        return outs[0], outs[1].reshape(S5_SG, 2, bsz, S5_W)
    return (outs[0],)


def _merge_kernel(x_ref, shift_ref, scale_ref, gate_ref, gnorm_ref, of_ref, ob_ref, zs_ref, gout_ref, ys_ref, u_ref,
                  oc_ref, wg_ref, wba_ref, wbb_ref, wbc_ref, wout_ref, wglu_ref, bglu_ref, dskip_ref, xo_ref):
    x = x_ref[...]
    h = ((_rms(x) * gnorm_ref[...]) * (1.0 + scale_ref[0]) + shift_ref[0]).astype(BF16)
    od = of_ref[...] + ob_ref[...]
    gout = gout_ref[...]
    oa = jnp.concatenate([_rms(od[:, hh * HEAD_DIM_A:(hh + 1) * HEAD_DIM_A]) * gout for hh in range(H_A)], axis=1)
    oa = oa * zs_ref[...]
    yb = jax.nn.gelu(ys_ref[...] + dskip_ref[...] * u_ref[...])
    ob = yb * jax.nn.sigmoid(_mm(yb, wglu_ref[...]) + bglu_ref[...])
    acc = None
    for nbr, (o, w) in enumerate(((oa, wba_ref), (ob, wbb_ref), (oc_ref[...], wbc_ref))):
        gate = jax.nn.sigmoid(jnp.dot(h, wg_ref[:, nbr * D_MODEL:(nbr + 1) * D_MODEL], preferred_element_type=F32))
        term = gate * _mm(o, w[...])
        acc = term if acc is None else acc + term
    out = _mm(acc, wout_ref[...])
    xo_ref[...] = x + gate_ref[0] * out


def _merge(x, mods_rows, row_of_tile, lp, o_f, o_b, zs, ys, u, oc, tl):
    n = x.shape[0]
    const2 = lambda i: (0, 0)
    tok = lambda w: pl.BlockSpec((tl, w), lambda i: (i, 0))
    mod = lambda j: pl.BlockSpec((1, 1, D_MODEL), lambda i: (row_of_tile(i) * 6 + j, 0, 0))
    return pl.pallas_call(
        _merge_kernel,
        grid=(n // tl,),
        in_specs=[tok(D_MODEL), mod(0), mod(1), mod(2), pl.BlockSpec((1, D_MODEL), const2),
                  tok(A_WIDTH), tok(A_WIDTH), tok(A_WIDTH), pl.BlockSpec((1, HEAD_DIM_A), const2),
                  tok(B_WIDTH), tok(B_WIDTH), tok(BRANCH_WIDTH),
                  pl.BlockSpec((D_MODEL, N_BRANCH * D_MODEL), const2),
                  pl.BlockSpec((BRANCH_WIDTH, D_MODEL), const2), pl.BlockSpec((BRANCH_WIDTH, D_MODEL), const2),
                  pl.BlockSpec((BRANCH_WIDTH, D_MODEL), const2), pl.BlockSpec((D_MODEL, D_MODEL), const2),
                  pl.BlockSpec((B_WIDTH, B_WIDTH), const2), pl.BlockSpec((1, B_WIDTH), const2),
                  pl.BlockSpec((1, B_WIDTH), const2)],
        out_specs=tok(D_MODEL),
        out_shape=jax.ShapeDtypeStruct((n, D_MODEL), F32),
        compiler_params=_cp(("parallel",)),
        name="merge",
    )(x, mods_rows, mods_rows, mods_rows, lp['g_norm_mix'], o_f, o_b, zs, lp['g_delta_out'], ys, u, oc,
      lp['w_gates'], lp['w_br_a'], lp['w_br_b'],
      lp['w_br_c'], lp['w_out'], lp['w_glu'], lp['b_glu'], lp['s5_d'])


FF_BLK = 256


def _ffn_kernel(*refs, tl, tiles_per_seq, final):
    it = iter(refs)
    xp_ref, xc_ref, xn_ref = next(it), next(it), next(it)
    shift_ref, scale_ref, gate_ref, gnorm_ref = next(it), next(it), next(it), next(it)
    wup_ref, convw_ref, convb_ref, wdown_ref = next(it), next(it), next(it), next(it)
    gfin_ref = next(it) if final else None
    xo_ref = next(it)
    yo_ref = next(it) if final else None
    act_scr = next(it)

    i = pl.program_id(0)
    pos = i % tiles_per_seq
    mod_scale = 1.0 + scale_ref[0]
    mod_shift = shift_ref[0]
    gain = gnorm_ref[...]

    def norm_mod(x):
        return ((_rms(x) * gain) * mod_scale + mod_shift).astype(BF16)

    x = xc_ref[...]
    h_prev = norm_mod(xp_ref[...]) * (pos > 0).astype(BF16)
    h_next = norm_mod(xn_ref[...]) * (pos < tiles_per_seq - 1).astype(BF16)
    h_ext = jnp.concatenate([h_prev, norm_mod(x), h_next], axis=0)
    n_ext = tl + 2 * HALO
    pad = FFN_CONV // 2

    def conv_act(cols):
        up = jnp.dot(h_ext, wup_ref[:, cols], preferred_element_type=F32)
        acc = None
        for t in range(FFN_CONV):
            sh = (pad - t) % n_ext
            src = up if sh == 0 else pltpu.roll(up, sh, 0)
            term = src[HALO:HALO + tl, :] * convw_ref[t:t + 1, cols]
            acc = term if acc is None else acc + term
        return acc + convb_ref[:, cols]

    for j in range(D_FF // FF_BLK):
        gcols = slice(j * FF_BLK, (j + 1) * FF_BLK)
        vcols = slice(D_FF + j * FF_BLK, D_FF + (j + 1) * FF_BLK)
        act_scr[:, gcols] = (_silu(conv_act(gcols)) * conv_act(vcols)).astype(BF16)
    out = jnp.dot(act_scr[...], wdown_ref[...], preferred_element_type=F32)
    xo = x + gate_ref[0] * out
    xo_ref[...] = xo
    if final:
        yo_ref[...] = _rms(xo) * gfin_ref[...]


def _ffn(x, mods_rows, row_of_tile, lp, seq_len, tl, g_final):
    n = x.shape[0]
    tps = seq_len // tl
    final = g_final is not None
    xprev, xnext = _halo_maps(n, tl)
    const2 = lambda i: (0, 0)
    mod = lambda j: pl.BlockSpec((1, 1, D_MODEL), lambda i: (row_of_tile(i) * 6 + j, 0, 0))
    tok = pl.BlockSpec((tl, D_MODEL), lambda i: (i, 0))
    in_specs = [pl.BlockSpec((HALO, D_MODEL), xprev), tok, pl.BlockSpec((HALO, D_MODEL), xnext),
                mod(3), mod(4), mod(5), pl.BlockSpec((1, D_MODEL), const2),
                pl.BlockSpec((D_MODEL, 2 * D_FF), const2), pl.BlockSpec((FFN_CONV, 2 * D_FF), const2),
                pl.BlockSpec((1, 2 * D_FF), const2), pl.BlockSpec((D_FF, D_MODEL), const2)]
    args = [x, x, x, mods_rows, mods_rows, mods_rows, lp['g_norm_ffn'], lp['w_ffn_up'], lp['conv_ffn'],
            lp['b_conv_ffn'], lp['w_ffn_down']]
    out_specs = [tok]
    out_shape = [jax.ShapeDtypeStruct((n, D_MODEL), F32)]
    if final:
        in_specs.append(pl.BlockSpec((1, D_MODEL), const2))
        args.append(g_final)
        out_specs.append(tok)
        out_shape.append(jax.ShapeDtypeStruct((n, D_MODEL), F32))
    return pl.pallas_call(
        functools.partial(_ffn_kernel, tl=tl, tiles_per_seq=tps, final=final),
        grid=(n // tl,),
        in_specs=in_specs,
        out_specs=out_specs,
        out_shape=out_shape,
        scratch_shapes=[pltpu.VMEM((tl, D_FF), BF16)],
        compiler_params=_cp(("parallel",)),
        name="ffn",
    )(*args)


def _pad_heads(w, head_w, n_heads):
    k = w.shape[0]
    w = w.reshape(k, n_heads, head_w)
    w = jnp.pad(w, ((0, 0), (0, 0), (0, HEAD_PAD - head_w)))
    return w.reshape(k, n_heads * HEAD_PAD)


def _layer_params(l, p):
    w_in = p['w_in'][l]
    o = 0
    parts = {}
    for name, wd in (('qkv', 3 * A_WIDTH), ('z', A_WIDTH), ('beta', 2 * H_A), ('alpha', 2 * H_A), ('u', B_WIDTH),
                     ('qa', Q_LORA), ('kva', KV_LORA), ('kr', QK_ROPE), ('gates', N_BRANCH * D_MODEL)):
        parts[name] = w_in[:, o:o + wd]
        o += wd
    small = jnp.concatenate([parts['beta'], parts['alpha'], parts['kr'],
                             jnp.zeros((D_MODEL, LANES - 4 * H_A - QK_ROPE), F32)], axis=1)
    w_rest = jnp.concatenate([parts['z'], parts['u'], parts['qa'], parts['kva'], small], axis=1)
    pad8 = lambda v: jnp.pad(v.reshape(1, 2 * H_A), ((0, 0), (2 * H_A, LANES - 4 * H_A)))
    w_kv = p['w_kv_b'][l].reshape(KV_LORA, H_C, QK_NOPE + V_HEAD)
    w_k = w_kv[:, :, :QK_NOPE].reshape(KV_LORA, H_C * QK_NOPE)
    w_v = w_kv[:, :, QK_NOPE:].reshape(KV_LORA, H_C * V_HEAD)
    row = lambda v: v.reshape(1, -1)
    return {
        'g_norm_mix': row(p['g_norm_mix'][l]), 'g_norm_ffn': row(p['g_norm_ffn'][l]),
        'w_qkv': parts['qkv'].astype(BF16), 'w_rest': w_rest.astype(BF16), 'w_gates': parts['gates'].astype(BF16),
        'conv_qkv': p['conv_qkv'][l],
        'a_log128': pad8(p['a_log'][l]), 'dt_bias128': pad8(p['dt_bias'][l]),
        'g_delta_out': row(p['g_delta_out'][l]),
        'g_q_a': row(p['g_q_a'][l]), 'g_kv_a': row(p['g_kv_a'][l]),
        'w_q_pad': _pad_heads(p['w_q_b'][l], QK_NOPE + QK_ROPE, H_C).astype(BF16),
        'w_k_pad': _pad_heads(w_k, QK_NOPE, H_C).astype(BF16),
        'w_v_pad': _pad_heads(w_v, V_HEAD, H_C).astype(BF16),
        'w_br_a': p['w_branch'][l, 0].astype(BF16), 'w_br_b': p['w_branch'][l, 1].astype(BF16),
        'w_br_c': p['w_branch'][l, 2].astype(BF16),
        'w_out': p['w_out'][l].astype(BF16),
        'w_glu': p['w_glu'][l].astype(BF16), 'b_glu': row(p['b_glu'][l]), 's5_d': row(p['s5_d'][l]),
        'w_ffn_up': p['w_ffn_up'][l].astype(BF16), 'conv_ffn': p['conv_ffn'][l],
        'b_conv_ffn': row(p['b_conv_ffn'][l]), 'w_ffn_down': p['w_ffn_down'][l].astype(BF16),
    }


def _rope_tables(length):
    rows = length // GRID_W
    row = jnp.repeat(jnp.arange(rows, dtype=F32), GRID_W)
    col = (jnp.arange(length) % GRID_W).astype(F32)
    n_freq = QK_ROPE // 4
    inv_freq = 1.0 / (ROPE_BASE ** (jnp.arange(n_freq, dtype=F32) / n_freq))
    ang = jnp.concatenate([row[:, None] * inv_freq, col[:, None] * inv_freq], axis=-1)
    cos, sin = jnp.cos(ang), jnp.sin(ang)
    ones = jnp.ones((length, QK_NOPE), F32)
    zeros = jnp.zeros((length, QK_NOPE), F32)
    tail = jnp.zeros((length, HEAD_PAD - QK_NOPE - QK_ROPE), F32)
    cos_t = jnp.concatenate([ones, cos, cos, tail], axis=1)
    sin_t = jnp.concatenate([zeros, -sin, sin, tail], axis=1)
    return cos_t, sin_t


def _layer(x, bsz, seq_len, mods_rows, row_of_tile, lp, s5mats, ctx, rope_tabs, tl, g_final):
    is_ctx = ctx is None
    outs = _inproj(x, mods_rows, row_of_tile, lp, seq_len, tl, rope_tabs, emit_ctx=is_ctx)
    if is_ctx:
        qkv, zs, u, bg, ckv, kr, qh, kh, vh = outs
    else:
        qkv, zs, u, bg, qh, kh, vh = outs

    prep = _delta_prep(qkv, bg, bsz, seq_len)
    d_out = _delta_recur(prep, None if is_ctx else ctx['s0'], bsz, seq_len, min(bsz, DELTA_NB), emit_state=is_ctx)
    o_f = d_out[0].reshape(bsz * seq_len, A_WIDTH)
    o_b = d_out[1].reshape(bsz * seq_len, A_WIDTH)

    s5_nb = max(1, min(bsz, S5_MAX_NB, (S5_STEP_ROWS * S5_T) // seq_len))
    s_out = _s5(u, s5mats, None if is_ctx else ctx['h0'], bsz, seq_len, s5_nb, emit_state=is_ctx)
    ys = s_out[0]

    segs = [(kh, vh, seq_len)]
    if not is_ctx:
        segs = [(ctx['kh'], ctx['vh'], ctx['past'])] + segs
    oc = _attention(qh, segs, bsz, seq_len, min(seq_len, ATT_TQ),
                    H_C if seq_len <= ATT_BATCHED_MAX else ATT_HEADS_LONG)

    x = _merge(x, mods_rows, row_of_tile, lp, o_f, o_b, zs, ys, u, oc, tl)
    f_out = _ffn(x, mods_rows, row_of_tile, lp, seq_len, tl, g_final)
    extras = None
    if is_ctx:
        hfin = s_out[1]
        hfin = hfin.reshape(S5_SG, 2, bsz, S5_HALVES, 2, S5_HG, S5_STATE).transpose(4, 2, 1, 0, 3, 5, 6)
        hfin = hfin.reshape(2, bsz, 2, S5_GROUPS, S5_STATE)
        extras = (d_out[2], hfin[0], hfin[1], ckv.reshape(bsz, seq_len, KV_LORA), kr.reshape(bsz, seq_len, QK_ROPE))
    return f_out, extras


def kernel(x_prompt, x_sample, state_delta, state_s5_re, state_s5_im, cache_ckv, cache_krope, c, c_ctx, w_mod, b_mod, g_norm_mix, g_norm_ffn, w_in, conv_qkv, a_log, dt_bias, g_delta_out, s5_lam_re, s5_lam_im, s5_log_dt, s5_b_re, s5_b_im, s5_c_re, s5_c_im, s5_d, w_glu, b_glu, g_q_a, w_q_b, g_kv_a, w_kv_b, w_branch, w_out, w_ffn_up, conv_ffn, b_conv_ffn, w_ffn_down, g_final):
    p = dict(g_norm_mix=g_norm_mix, g_norm_ffn=g_norm_ffn, w_in=w_in, conv_qkv=conv_qkv, a_log=a_log,
             dt_bias=dt_bias, g_delta_out=g_delta_out, s5_d=s5_d, w_glu=w_glu, b_glu=b_glu, g_q_a=g_q_a,
             w_q_b=w_q_b, g_kv_a=g_kv_a, w_kv_b=w_kv_b, w_branch=w_branch, w_out=w_out, w_ffn_up=w_ffn_up,
             conv_ffn=conv_ffn, b_conv_ffn=b_conv_ffn, w_ffn_down=w_ffn_down)
    bp, lp_len, _ = x_prompt.shape
    bs, ls_len, _ = x_sample.shape
    past = cache_ckv.shape[2]
    depth = w_in.shape[0]

    mod_rows = MOD_ROWS
    cvec = jnp.concatenate([c_ctx[None, :], c, jnp.zeros((mod_rows - 1 - bs, D_MODEL), F32)], axis=0).astype(F32)
    mods = _modulation(cvec, w_mod, b_mod)

    s5_all = _s5prep(s5_lam_re, s5_lam_im, s5_log_dt, s5_b_re, s5_b_im, s5_c_re, s5_c_im)

    rope_tabs = _rope_tables(ls_len)
    tl_p = min(lp_len, TOKEN_TILE)
    tl_s = min(ls_len, TOKEN_TILE)
    tiles_per_seq_s = ls_len // tl_s

    def sg_state(re, im):
        t = jnp.stack([re, im], axis=0).astype(F32).reshape(2, bs, 2, S5_SG, S5_HALVES, S5_HG, S5_STATE)
        return t.transpose(3, 2, 1, 4, 0, 5, 6).reshape(S5_SG, 2, bs, S5_W)

    xp = x_prompt.astype(F32).reshape(bp * lp_len, D_MODEL)
    xs = x_sample.astype(F32).reshape(bs * ls_len, D_MODEL)
    deltas, s5_res, s5_ims, ckvs, kropes = [], [], [], [], []
    yp = ys = None
    for l in range(depth):
        lp = _layer_params(l, p)
        mods_rows = mods[l].reshape(mod_rows * 6, 1, D_MODEL)
        per_layer = S5_SG * S5_HALVES
        s5mats = tuple(m[l * per_layer:(l + 1) * per_layer].reshape((S5_SG, S5_HALVES) + m.shape[1:]) for m in s5_all)
        last = l == depth - 1
        gfin = g_final.reshape(1, D_MODEL) if last else None

        out_p, extras = _layer(xp, bp, lp_len, mods_rows, lambda i: 0, lp, s5mats, None, None, tl_p, gfin)
        deltas.append(extras[0])
        s5_res.append(extras[1])
        s5_ims.append(extras[2])
        ckvs.append(extras[3])
        kropes.append(extras[4])

        kh_c, vh_c = _kvcache(cache_ckv[:, l].astype(F32).reshape(bs * past, KV_LORA),
                              cache_krope[:, l].astype(F32).reshape(bs * past, QK_ROPE), lp, past)
        ctx = dict(s0=state_delta[:, l].astype(F32), h0=sg_state(state_s5_re[:, l], state_s5_im[:, l]),
                   kh=kh_c, vh=vh_c, past=past)
        out_s, _ = _layer(xs, bs, ls_len, mods_rows, lambda i: 1 + i // tiles_per_seq_s, lp, s5mats, ctx, rope_tabs,
                          tl_s, gfin)
        xp, xs = out_p[0], out_s[0]
        if last:
            yp, ys = out_p[1], out_s[1]

    y_prompt = yp.reshape(bp, lp_len, D_MODEL)
    y_sample = ys.reshape(bs, ls_len, D_MODEL)
    return (y_prompt, y_sample, jnp.stack(deltas, axis=1), jnp.stack(s5_res, axis=1), jnp.stack(s5_ims, axis=1),
            jnp.stack(ckvs, axis=1), jnp.stack(kropes, axis=1))
```
